```python
import math
import numpy as np
import jax
import jax.numpy as jnp
from jax import lax

D_MODEL = 2048
BATCH = 2
SEQ = 16384
DEPTH = 2

N_MIXERS = 2
GM_WIDTH = D_MODEL
GM_GROUPS = 8
GM_GROUP_DIM = GM_WIDTH // GM_GROUPS
GM_CHUNK = 128
NSA_HEAD_DIM = 128
NSA_HEADS = D_MODEL // NSA_HEAD_DIM
NSA_KV_HEADS = 4
NSA_GQA = NSA_HEADS // NSA_KV_HEADS
CMP_LEN = 32
CMP_STRIDE = 16
SEL_BLK = 64
SEL_TOPN = 16
SEL_LOCAL = 2
WINDOW = 512
Q_BLK = 128
FORCE = 1.0e4
NSA_Q_DIM = NSA_HEADS * NSA_HEAD_DIM
NSA_KV_DIM = NSA_KV_HEADS * NSA_HEAD_DIM
NSA_IN_DIM = NSA_Q_DIM + 6 * NSA_KV_DIM + 3 * NSA_HEADS
N_GROUPS = 4
EXP_PER_GROUP = 8
N_EXPERTS = N_GROUPS * EXP_PER_GROUP
TOP_K = 2
D_EXPERT = 512
MOE_BLK = 256
EPS = 1e-6
NEG = -1e30

kernel_name = "hybrid_gmlp_nsa_hmoe"


def rmsnorm(x, g):
    xf = x.astype(jnp.float32)
    y = xf * lax.rsqrt(jnp.mean(xf * xf, axis=-1, keepdims=True) + EPS)
    return (y * g.astype(jnp.float32)).astype(x.dtype)


def layernorm(x, g, b):
    xf = x.astype(jnp.float32)
    mu = jnp.mean(xf, axis=-1, keepdims=True)
    xc = xf - mu
    y = xc * lax.rsqrt(jnp.mean(xc * xc, axis=-1, keepdims=True) + EPS)
    return (y * g.astype(jnp.float32) + b.astype(jnp.float32)).astype(x.dtype)


def masked_softmax(s, mask):
    s = jnp.where(mask, s.astype(jnp.float32), NEG)
    m = jnp.max(s, axis=-1, keepdims=True)
    p = jnp.where(mask, jnp.exp(s - m), 0.0)
    return p / jnp.maximum(jnp.sum(p, axis=-1, keepdims=True), 1e-30)


def chunked_gmlp(h, w_in, ln_g, ln_b, w_s, b_s, w_out):
    B, S, _ = h.shape
    z = jax.nn.gelu(h @ w_in)
    u, v = jnp.split(z, 2, axis=-1)
    v = layernorm(v, ln_g, ln_b)
    v = v.reshape(B, S // GM_CHUNK, GM_CHUNK, GM_GROUPS, GM_GROUP_DIM)
    causal = jnp.tril(jnp.ones((GM_CHUNK, GM_CHUNK), dtype=w_s.dtype))
    sv = jnp.einsum('gts,bcsgd->bctgd', w_s * causal, v) + b_s.T[None, None, :, :, None]
    y = u * sv.reshape(B, S, GM_WIDTH)
    return y @ w_out


def compress_kv(k, pe, w1, w2):
    B, S, Hk, Dh = k.shape
    n_c = (S - CMP_LEN) // CMP_STRIDE + 1
    idx = np.arange(n_c)[:, None] * CMP_STRIDE + np.arange(CMP_LEN)[None, :]
    blocks = k[:, idx] + pe[None, None, :, None, :]
    flat = blocks.transpose(0, 1, 3, 2, 4).reshape(B, n_c, Hk, CMP_LEN * Dh)
    return jax.nn.gelu(flat @ w1) @ w2


def nsa_attention(h, w_in, ck_pe, ck_w1, ck_w2, cv_pe, cv_w1, cv_w2, w_out):
    B, S, _ = h.shape
    Hk, G, Dh = NSA_KV_HEADS, NSA_GQA, NSA_HEAD_DIM
    proj = h @ w_in
    splits = [NSA_Q_DIM + i * NSA_KV_DIM for i in range(7)]
    q, k_c, v_c, k_s, v_s, k_w, v_w, g = jnp.split(proj, splits, axis=-1)
    q = q.reshape(B, S, Hk, G, Dh) * (Dh ** -0.5)
    kv = lambda a: a.reshape(B, S, Hk, Dh)
    gates = jax.nn.sigmoid(g.astype(jnp.float32)).reshape(B, S, Hk, G, 3).astype(h.dtype)

    kc = compress_kv(kv(k_c), ck_pe, ck_w1, ck_w2)
    vc = compress_kv(kv(v_c), cv_pe, cv_w1, cv_w2)
    n_c = kc.shape[1]
    cmp_end = jnp.asarray(np.arange(n_c) * CMP_STRIDE + CMP_LEN - 1)
    n_sel = S // SEL_BLK
    n_top = min(SEL_TOPN, n_sel)
    ci = np.arange(n_c)[:, None] * CMP_STRIDE
    sj = np.arange(n_sel)[None, :] * SEL_BLK
    overlap = jnp.asarray(((ci < sj + SEL_BLK) & (ci + CMP_LEN > sj)).astype(np.float32))

    ks_blk = kv(k_s).reshape(B, n_sel, SEL_BLK, Hk, Dh).transpose(0, 3, 1, 2, 4)
    vs_blk = kv(v_s).reshape(B, n_sel, SEL_BLK, Hk, Dh).transpose(0, 3, 1, 2, 4)
    kw_pad = jnp.pad(kv(k_w), ((0, 0), (WINDOW, 0), (0, 0), (0, 0)))
    vw_pad = jnp.pad(kv(v_w), ((0, 0), (WINDOW, 0), (0, 0), (0, 0)))
    gather = jax.vmap(jax.vmap(lambda blocks, idx: blocks[idx]))
    blk = jnp.arange(n_sel)

    def query_block(qb):
        q0 = qb * Q_BLK
        t = q0 + jnp.arange(Q_BLK)
        qblk = lax.dynamic_slice_in_dim(q, q0, Q_BLK, axis=1)
        gblk = lax.dynamic_slice_in_dim(gates, q0, Q_BLK, axis=1)
        mask_c = cmp_end[None, :] <= t[:, None]
        p_c = masked_softmax(jnp.einsum('bqhgd,bchd->bhgqc', qblk, kc), mask_c)
        o_c = jnp.einsum('bhgqc,bchd->bqhgd', p_c.astype(vc.dtype), vc)
        imp = jnp.einsum('bhgqc,cj->bhqj', p_c, overlap)
        cur = (t // SEL_BLK)[:, None]
        valid = blk[None, :] * SEL_BLK <= t[:, None]
        dist = cur - blk[None, :]
        forced = (blk[None, :] == 0) | ((dist >= 0) & (dist < SEL_LOCAL))
        score = jnp.where(valid, jnp.where(forced, FORCE, imp), NEG)
        _, top_idx = lax.top_k(score, n_top)
        k_sel = gather(ks_blk, top_idx).reshape(B, Hk, Q_BLK, n_top * SEL_BLK, Dh)
        v_sel = gather(vs_blk, top_idx).reshape(B, Hk, Q_BLK, n_top * SEL_BLK, Dh)
        pos_s = (top_idx[..., None] * SEL_BLK + jnp.arange(SEL_BLK)).reshape(B, Hk, Q_BLK, n_top * SEL_BLK)
        mask_s = (pos_s <= t[:, None])[:, :, None]
        p_s = masked_softmax(jnp.einsum('bqhgd,bhqkd->bhgqk', qblk, k_sel), mask_s)
        o_s = jnp.einsum('bhgqk,bhqkd->bqhgd', p_s.astype(v_sel.dtype), v_sel)
        k_win = lax.dynamic_slice_in_dim(kw_pad, q0, WINDOW + Q_BLK, axis=1)
        v_win = lax.dynamic_slice_in_dim(vw_pad, q0, WINDOW + Q_BLK, axis=1)
        pos_w = q0 - WINDOW + jnp.arange(WINDOW + Q_BLK)
        mask_w = (pos_w[None, :] <= t[:, None]) & (pos_w[None, :] > t[:, None] - WINDOW) & (pos_w[None, :] >= 0)
        p_w = masked_softmax(jnp.einsum('bqhgd,bkhd->bhgqk', qblk, k_win), mask_w)
        o_w = jnp.einsum('bhgqk,bkhd->bqhgd', p_w.astype(v_win.dtype), v_win)
        o = gblk[..., 0:1] * o_c + gblk[..., 1:2] * o_s + gblk[..., 2:3] * o_w
        return o.reshape(B, Q_BLK, NSA_Q_DIM)

    o = lax.map(query_block, jnp.arange(S // Q_BLK))
    o = o.transpose(1, 0, 2, 3).reshape(B, S, NSA_Q_DIM)
    return o @ w_out


def hier_moe(h, w_group, b_group, w_exp, b_exp, w_gate, w_up, w_down):
    B, S, D = h.shape
    T = B * S
    xt = h.reshape(T, D)
    g_prob = jax.nn.softmax((xt @ w_group).astype(jnp.float32) + b_group.astype(jnp.float32), axis=-1)
    g_w, g_idx = lax.top_k(g_prob, 1)
    e_logits = ((xt @ w_exp).astype(jnp.float32) + b_exp.astype(jnp.float32)).reshape(T, N_GROUPS, EXP_PER_GROUP)
    e_logits = jnp.take_along_axis(e_logits, g_idx[:, :, None], axis=1)[:, 0]
    e_w, e_idx = lax.top_k(jax.nn.softmax(e_logits, axis=-1), TOP_K)
    e_w = e_w / jnp.sum(e_w, axis=-1, keepdims=True)
    weight = (g_w * e_w).reshape(-1)
    expert = (g_idx * EXP_PER_GROUP + e_idx).reshape(-1).astype(jnp.int32)
    token = jnp.repeat(jnp.arange(T, dtype=jnp.int32), TOP_K)
    n_assign = T * TOP_K
    order = jnp.argsort(expert)
    s_exp, s_tok, s_w = expert[order], token[order], weight[order]
    counts = jnp.bincount(expert, length=N_EXPERTS)
    padded = (counts + MOE_BLK - 1) // MOE_BLK * MOE_BLK
    pad_end = jnp.cumsum(padded)
    pad_start = pad_end - padded
    start = jnp.cumsum(counts) - counts
    dest = pad_start[s_exp] + jnp.arange(n_assign, dtype=jnp.int32) - start[s_exp]
    n_rows = (n_assign + MOE_BLK - 1) // MOE_BLK * MOE_BLK + N_EXPERTS * MOE_BLK
    n_blocks = n_rows // MOE_BLK
    row_tok = jnp.full((n_rows,), T, jnp.int32).at[dest].set(s_tok)
    row_w = jnp.zeros((n_rows,), jnp.float32).at[dest].set(s_w)
    block_exp = jnp.minimum(jnp.searchsorted(pad_end, jnp.arange(n_blocks) * MOE_BLK, side='right'),
                            N_EXPERTS - 1)
    x_pad = jnp.concatenate([xt, jnp.zeros((1, D), xt.dtype)], axis=0)

    def run_block(args):
        rows, e = args
        xb = x_pad[rows]
        return (jax.nn.silu(xb @ w_gate[e]) * (xb @ w_up[e])) @ w_down[e]

    y = lax.map(run_block, (row_tok.reshape(n_blocks, MOE_BLK), block_exp))
    y = y.reshape(n_rows, D) * row_w[:, None].astype(y.dtype)
    out = jnp.zeros((T + 1, D), y.dtype).at[row_tok].add(y)[:T]
    return out.reshape(B, S, D)


def setup_inputs(seed: int = 0) -> dict:
    key = jax.random.key(seed)
    keys = list(jax.random.split(key, 32))

    def nrm(shape, scale):
        return jax.random.normal(keys.pop(), shape, jnp.float32) * scale

    n_a = (DEPTH + N_MIXERS - 1) // N_MIXERS
    n_b = DEPTH // N_MIXERS
    Dh = NSA_HEAD_DIM
    return {
        "x": nrm((BATCH, SEQ, D_MODEL), 1.0),
        "norm_mix": 1.0 + nrm((DEPTH, D_MODEL), 0.02),
        "norm_ffn": 1.0 + nrm((DEPTH, D_MODEL), 0.02),
        "norm_final": 1.0 + nrm((D_MODEL,), 0.02),
        "a_w_in": nrm((n_a, D_MODEL, 2 * GM_WIDTH), D_MODEL ** -0.5),
        "a_ln_g": 1.0 + nrm((n_a, GM_WIDTH), 0.02),
        "a_ln_b": nrm((n_a, GM_WIDTH), 0.02),
        "a_w_s": nrm((n_a, GM_GROUPS, GM_CHUNK, GM_CHUNK), GM_CHUNK ** -0.5),
        "a_b_s": 1.0 + nrm((n_a, GM_GROUPS, GM_CHUNK), 0.02),
        "a_w_out": nrm((n_a, GM_WIDTH, D_MODEL), GM_WIDTH ** -0.5),
        "b_w_in": nrm((n_b, D_MODEL, NSA_IN_DIM), D_MODEL ** -0.5),
        "b_ck_pe": nrm((n_b, CMP_LEN, Dh), 0.1),
        "b_ck_w1": nrm((n_b, CMP_LEN * Dh, Dh), (CMP_LEN * Dh) ** -0.5),
        "b_ck_w2": nrm((n_b, Dh, Dh), Dh ** -0.5),
        "b_cv_pe": nrm((n_b, CMP_LEN, Dh), 0.1),
        "b_cv_w1": nrm((n_b, CMP_LEN * Dh, Dh), (CMP_LEN * Dh) ** -0.5),
        "b_cv_w2": nrm((n_b, Dh, Dh), Dh ** -0.5),
        "b_w_out": nrm((n_b, NSA_Q_DIM, D_MODEL), NSA_Q_DIM ** -0.5),
        "r_w_group": nrm((DEPTH, D_MODEL, N_GROUPS), D_MODEL ** -0.5),
        "r_b_group": nrm((DEPTH, N_GROUPS), 0.01),
        "r_w_exp": nrm((DEPTH, D_MODEL, N_EXPERTS), D_MODEL ** -0.5),
        "r_b_exp": nrm((DEPTH, N_EXPERTS), 0.01),
        "e_w_gate": nrm((DEPTH, N_EXPERTS, D_MODEL, D_EXPERT), D_MODEL ** -0.5),
        "e_w_up": nrm((DEPTH, N_EXPERTS, D_MODEL, D_EXPERT), D_MODEL ** -0.5),
        "e_w_down": nrm((DEPTH, N_EXPERTS, D_EXPERT, D_MODEL), D_EXPERT ** -0.5),
    }


def reference(x, norm_mix, norm_ffn, norm_final, a_w_in, a_ln_g, a_ln_b, a_w_s, a_b_s, a_w_out,
              b_w_in, b_ck_pe, b_ck_w1, b_ck_w2, b_cv_pe, b_cv_w1, b_cv_w2, b_w_out,
              r_w_group, r_b_group, r_w_exp, r_b_exp, e_w_gate, e_w_up, e_w_down):
    h = x
    for i in range(DEPTH):
        hn = rmsnorm(h, norm_mix[i])
        j = i // N_MIXERS
        if i % N_MIXERS == 0:
            mix = chunked_gmlp(hn, a_w_in[j], a_ln_g[j], a_ln_b[j], a_w_s[j], a_b_s[j], a_w_out[j])
        else:
            mix = nsa_attention(hn, b_w_in[j], b_ck_pe[j], b_ck_w1[j], b_ck_w2[j],
                                b_cv_pe[j], b_cv_w1[j], b_cv_w2[j], b_w_out[j])
        h = h + mix
        h = h + hier_moe(rmsnorm(h, norm_ffn[i]), r_w_group[i], r_b_group[i], r_w_exp[i], r_b_exp[i],
                         e_w_gate[i], e_w_up[i], e_w_down[i])
    return rmsnorm(h, norm_final)
```

```python
import functools

import numpy as np
import jax
import jax.numpy as jnp
from jax import lax
from jax.experimental import pallas as pl
from jax.experimental.pallas import tpu as pltpu

F32 = jnp.float32
BF16 = jnp.bfloat16

D_MODEL = 2048
LANE = 128
GM_GROUPS = 8
GM_CHUNK = 128
HEAD_DIM = 128
N_HEADS = 16
N_KV_HEADS = 4
GQA = N_HEADS // N_KV_HEADS
CMP_LEN = 32
CMP_STRIDE = 16
SEL_BLK = 64
SEL_TOPN = 16
SEL_LOCAL = 2
WINDOW = 512
Q_BLK = 128
FORCE = 1.0e4
N_GROUPS = 4
EXP_PER_GROUP = 8
N_EXPERTS = N_GROUPS * EXP_PER_GROUP
TOP_K = 2
D_EXPERT = 512
EPS = 1e-6
NEG = -1e30
SEL_TILE = 512
MOE_BLK = 512
VMEM_LIMIT = 56 * 1024 * 1024

_NT = (((1,), (1,)), ((), ()))


def _params(sem):
    return pltpu.CompilerParams(dimension_semantics=sem, vmem_limit_bytes=VMEM_LIMIT)


def _gelu(x):
    return 0.5 * x * (1.0 + jnp.tanh(0.7978845608028654 * (x + 0.044715 * (x * x * x))))


def _rms(x, g):
    y = x * lax.rsqrt(jnp.mean(x * x, axis=-1, keepdims=True) + EPS)
    return y * g


def _rmsnorm_kernel(x_ref, g_ref, o_ref):
    o_ref[...] = _rms(x_ref[...], g_ref[...]).astype(o_ref.dtype)


def rmsnorm(x, g, out_dtype=BF16, tm=512):
    T = x.shape[0]
    return pl.pallas_call(
        _rmsnorm_kernel,
        grid=(T // tm,),
        in_specs=[pl.BlockSpec((tm, D_MODEL), lambda i: (i, 0)),
                  pl.BlockSpec((1, D_MODEL), lambda i: (0, 0))],
        out_specs=pl.BlockSpec((tm, D_MODEL), lambda i: (i, 0)),
        out_shape=jax.ShapeDtypeStruct((T, D_MODEL), out_dtype),
        compiler_params=_params(("parallel",)),
    )(x, g.reshape(1, D_MODEL))


def _router_kernel(x_ref, g_ref, wr_ref, lg_ref):
    y = _rms(x_ref[...], g_ref[...])
    lg_ref[...] = jnp.dot(y, wr_ref[...], precision=lax.Precision.HIGHEST,
                          preferred_element_type=F32)


def router_logits(x, g, wr, tm=512):
    T = x.shape[0]
    return pl.pallas_call(
        _router_kernel,
        grid=(T // tm,),
        in_specs=[pl.BlockSpec((tm, D_MODEL), lambda i: (i, 0)),
                  pl.BlockSpec((1, D_MODEL), lambda i: (0, 0)),
                  pl.BlockSpec((D_MODEL, LANE), lambda i: (0, 0))],
        out_specs=pl.BlockSpec((tm, LANE), lambda i: (i, 0)),
        out_shape=jax.ShapeDtypeStruct((T, LANE), F32),
        compiler_params=_params(("parallel",)),
    )(x, g.reshape(1, D_MODEL), wr)


def _mm_kernel(*refs, act, scale, has_resid, layout):
    a_ref, w_ref = refs[0], refs[1]
    o_ref = refs[-1]
    acc = jnp.dot(a_ref[...], w_ref[...], preferred_element_type=F32)
    if scale is not None:
        acc = acc * scale
    if act == "gelu":
        acc = _gelu(acc)
    elif act == "sigmoid":
        acc = jax.nn.sigmoid(acc)
    if has_resid:
        acc = acc + refs[2][...]
    tm, tn = acc.shape
    if layout == "plain":
        o_ref[...] = acc.astype(o_ref.dtype)
    elif layout == "heads":
        for h in range(tn // HEAD_DIM):
            o_ref[0, h] = acc[:, h * HEAD_DIM:(h + 1) * HEAD_DIM].astype(o_ref.dtype)
    else:
        for h in range(tn // HEAD_DIM):
            for c in range(tm // LANE):
                blk = acc[c * LANE:(c + 1) * LANE, h * HEAD_DIM:(h + 1) * HEAD_DIM]
                o_ref[0, h, c] = blk.T.astype(o_ref.dtype)


def matmul(a, w, *, out_dtype, act=None, scale=None, resid=None, layout="plain",
           seq=None, tm=512, tn=1024):
    M, K = a.shape
    N = w.shape[1]
    tn = min(tn, N)
    grid = (N // tn, M // tm)
    in_specs = [pl.BlockSpec((tm, K), lambda j, i: (i, 0)),
                pl.BlockSpec((K, tn), lambda j, i: (0, j))]
    args = [a, w]
    if resid is not None:
        in_specs.append(pl.BlockSpec((tm, tn), lambda j, i: (i, j)))
        args.append(resid)
    if layout == "plain":
        out_shape = jax.ShapeDtypeStruct((M, N), out_dtype)
        out_spec = pl.BlockSpec((tm, tn), lambda j, i: (i, j))
    else:
        nb = seq // tm
        nh = tn // HEAD_DIM
        if layout == "heads":
            out_shape = jax.ShapeDtypeStruct((M // seq, N // HEAD_DIM, seq, HEAD_DIM), out_dtype)
            out_spec = pl.BlockSpec((1, nh, tm, HEAD_DIM), lambda j, i: (i // nb, j, i % nb, 0))
        else:
            out_shape = jax.ShapeDtypeStruct(
                (M // seq, N // HEAD_DIM, seq // LANE, HEAD_DIM, LANE), out_dtype)
            out_spec = pl.BlockSpec((1, nh, tm // LANE, HEAD_DIM, LANE),
                                    lambda j, i: (i // nb, j, i % nb, 0, 0))
    kern = functools.partial(_mm_kernel, act=act, scale=scale, has_resid=resid is not None,
                             layout=layout)
    return pl.pallas_call(
        kern, grid=grid, in_specs=in_specs, out_specs=out_spec, out_shape=out_shape,
        compiler_params=_params(("parallel", "parallel")),
    )(*args)


def _gate_kernel(z_ref, lng_ref, lnb_ref, ws_ref, bs_ref, y_ref):
    tm = z_ref.shape[0]
    u = z_ref[:, :D_MODEL]
    v = z_ref[:, D_MODEL:]
    mu = jnp.mean(v, axis=-1, keepdims=True)
    vc = v - mu
    vn = vc * lax.rsqrt(jnp.mean(vc * vc, axis=-1, keepdims=True) + EPS)
    vn = (vn * lng_ref[...] + lnb_ref[...]).astype(BF16)
    gd = D_MODEL // GM_GROUPS
    for c in range(tm // GM_CHUNK):
        rows = slice(c * GM_CHUNK, (c + 1) * GM_CHUNK)
        for g in range(GM_GROUPS):
            cols = slice(g * gd, (g + 1) * gd)
            sv = jnp.dot(ws_ref[g], vn[rows, cols], preferred_element_type=F32)
            y_ref[rows, cols] = (u[rows, cols] * (sv + bs_ref[:, cols])).astype(y_ref.dtype)


def gmlp_gate(z, ln_g, ln_b, ws_masked, bs_full, tm=256):
    T = z.shape[0]
    return pl.pallas_call(
        _gate_kernel,
        grid=(T // tm,),
        in_specs=[pl.BlockSpec((tm, 2 * D_MODEL), lambda i: (i, 0)),
                  pl.BlockSpec((1, D_MODEL), lambda i: (0, 0)),
                  pl.BlockSpec((1, D_MODEL), lambda i: (0, 0)),
                  pl.BlockSpec((GM_GROUPS, GM_CHUNK, GM_CHUNK), lambda i: (0, 0, 0)),
                  pl.BlockSpec((GM_CHUNK, D_MODEL), lambda i: (0, 0))],
        out_specs=pl.BlockSpec((tm, D_MODEL), lambda i: (i, 0)),
        out_shape=jax.ShapeDtypeStruct((T, D_MODEL), BF16),
        compiler_params=_params(("parallel",)),
    )(z, ln_g.reshape(1, D_MODEL), ln_b.reshape(1, D_MODEL), ws_masked, bs_full)


def _compress_kernel(x_ref, pe_ref, w1_ref, w2_ref, o_ref, ot_ref):
    x = x_ref[0]
    half = CMP_STRIDE * HEAD_DIM
    xa = (x + pe_ref[0, 0:1, :]).astype(BF16)
    xb = (x + pe_ref[0, 1:2, :]).astype(BF16)
    a = jnp.dot(xa, w1_ref[0, :half, :], preferred_element_type=F32)
    b = jnp.dot(xb, w1_ref[0, half:, :], preferred_element_type=F32)
    n = a.shape[0]
    pre = a + pltpu.roll(b, n - 1, axis=0)
    out = jnp.dot(_gelu(pre).astype(BF16), w2_ref[0], preferred_element_type=F32)
    o_ref[0] = out.astype(o_ref.dtype)
    ot_ref[0] = out.T.astype(ot_ref.dtype)


def compress_kv(kvc, pe2, w1, w2):
    B, H2, S, _ = kvc.shape
    n_grp = S // CMP_STRIDE
    x = kvc.reshape(B * H2, n_grp, CMP_STRIDE * HEAD_DIM)
    sel = lambda i: ((i % H2) // N_KV_HEADS, 0, 0)
    return pl.pallas_call(
        _compress_kernel,
        grid=(B * H2,),
        in_specs=[pl.BlockSpec((1, n_grp, CMP_STRIDE * HEAD_DIM), lambda i: (i, 0, 0)),
                  pl.BlockSpec((1, 2, CMP_STRIDE * HEAD_DIM), sel),
                  pl.BlockSpec((1, CMP_LEN * HEAD_DIM, HEAD_DIM), sel),
                  pl.BlockSpec((1, HEAD_DIM, HEAD_DIM), sel)],
        out_specs=[pl.BlockSpec((1, n_grp, HEAD_DIM), lambda i: (i, 0, 0)),
                   pl.BlockSpec((1, HEAD_DIM, n_grp), lambda i: (i, 0, 0))],
        out_shape=[jax.ShapeDtypeStruct((B * H2, n_grp, HEAD_DIM), BF16),
                   jax.ShapeDtypeStruct((B * H2, HEAD_DIM, n_grp), BF16)],
        compiler_params=_params(("parallel",)),
    )(x, pe2, w1, w2)


def _nsa_kernel(q_ref, kc_ref, vct_ref, ks_ref, vst_ref, kw_ref, vwt_ref, gt_ref, ovt_ref,
                o_ref, bias_ref):
    nq = GQA * Q_BLK
    qb = pl.program_id(2)
    q0 = qb * Q_BLK
    q = q_ref[0].reshape(nq, HEAD_DIM)
    t_lane = q0 + (lax.broadcasted_iota(jnp.int32, (1, nq), 1) & (Q_BLK - 1))

    n_c = kc_ref.shape[2]
    s = lax.dot_general(kc_ref[0, 0], q, _NT, preferred_element_type=F32)
    c_end = lax.broadcasted_iota(jnp.int32, (n_c, 1), 0) * CMP_STRIDE + (CMP_LEN - 1)
    mask = c_end <= t_lane
    s = jnp.where(mask, s, NEG)
    m = jnp.max(s, axis=0, keepdims=True)
    p = jnp.where(mask, jnp.exp(s - m), 0.0)
    l = jnp.sum(p, axis=0, keepdims=True)
    p = p * (1.0 / jnp.maximum(l, 1e-30))
    o_cmp = jnp.dot(vct_ref[0, 0], p.astype(BF16), preferred_element_type=F32)

    psum = p[:, 0:Q_BLK]
    for h in range(1, GQA):
        psum = psum + p[:, h * Q_BLK:(h + 1) * Q_BLK]
    p_hi = psum.astype(BF16)
    p_lo = (psum - p_hi.astype(F32)).astype(BF16)
    imp = (jnp.dot(ovt_ref[...], p_hi, preferred_element_type=F32)
           + jnp.dot(ovt_ref[...], p_lo, preferred_element_type=F32))
    n_sel = imp.shape[0]
    blk = lax.broadcasted_iota(jnp.int32, (n_sel, Q_BLK), 0)
    t_q = q0 + lax.broadcasted_iota(jnp.int32, (n_sel, Q_BLK), 1)
    dist = (t_q // SEL_BLK) - blk
    forced = (blk == 0) | ((dist >= 0) & (dist < SEL_LOCAL))
    score = jnp.where(blk * SEL_BLK <= t_q, jnp.where(forced, FORCE, imp), NEG)
    bias = jnp.full((n_sel, Q_BLK), NEG, F32)
    for _ in range(SEL_TOPN):
        top = jnp.max(score, axis=0, keepdims=True)
        first = jnp.min(jnp.where(score == top, blk, n_sel), axis=0, keepdims=True)
        hit = blk == first
        bias = jnp.where(hit, 0.0, bias)
        score = jnp.where(hit, -jnp.inf, score)
    bias_ref[...] = jnp.concatenate([bias] * GQA, axis=1)

    bpt = SEL_TILE // SEL_BLK
    vpt = SEL_TILE // LANE

    def sel_tile(kt, carry, causal):
        m_i, l_i, acc = carry
        k0 = pl.multiple_of(kt * SEL_TILE, SEL_TILE)
        st = lax.dot_general(ks_ref[0, 0, pl.ds(k0, SEL_TILE), :], q, _NT,
                             preferred_element_type=F32)
        parts = []
        for j in range(bpt):
            b_row = bias_ref[pl.ds(kt * bpt + j, 1), :]
            parts.append(st[j * SEL_BLK:(j + 1) * SEL_BLK, :] + b_row)
        st = jnp.concatenate(parts, axis=0)
        if causal:
            kpos = k0 + lax.broadcasted_iota(jnp.int32, (SEL_TILE, 1), 0)
            st = jnp.where(kpos <= t_lane, st, NEG)
        m_new = jnp.maximum(m_i, jnp.max(st, axis=0, keepdims=True))
        alpha = jnp.exp(m_i - m_new)
        pt = jnp.exp(st - m_new)
        l_new = alpha * l_i + jnp.sum(pt, axis=0, keepdims=True)
        vt = jnp.concatenate([vst_ref[0, 0, kt * vpt + j] for j in range(vpt)], axis=1)
        acc = alpha * acc + jnp.dot(vt, pt.astype(BF16), preferred_element_type=F32)
        return m_new, l_new, acc

    init = (jnp.full((1, nq), NEG, F32), jnp.zeros((1, nq), F32), jnp.zeros((HEAD_DIM, nq), F32))
    n_full = q0 // SEL_TILE
    carry = lax.fori_loop(0, n_full, lambda kt, c: sel_tile(kt, c, False), init)
    _, l_s, acc_s = sel_tile(n_full, carry, True)
    o_sel = acc_s * (1.0 / jnp.maximum(l_s, 1e-30))

    wk = WINDOW + Q_BLK
    kstart = pl.multiple_of(jnp.maximum(q0 - WINDOW, 0), LANE)
    sw = lax.dot_general(kw_ref[0, 0, pl.ds(kstart, wk), :], q, _NT,
                         preferred_element_type=F32)
    kpos = kstart + lax.broadcasted_iota(jnp.int32, (wk, 1), 0)
    mask_w = (kpos <= t_lane) & (kpos > t_lane - WINDOW)
    sw = jnp.where(mask_w, sw, NEG)
    m_w = jnp.max(sw, axis=0, keepdims=True)
    p_w = jnp.where(mask_w, jnp.exp(sw - m_w), 0.0)
    l_w = jnp.sum(p_w, axis=0, keepdims=True)
    vw = jnp.concatenate([vwt_ref[0, 0, kstart // LANE + j] for j in range(wk // LANE)], axis=1)
    o_win = jnp.dot(vw, p_w.astype(BF16), preferred_element_type=F32)
    o_win = o_win * (1.0 / jnp.maximum(l_w, 1e-30))

    for h in range(GQA):
        cols = slice(h * Q_BLK, (h + 1) * Q_BLK)
        o_t = (gt_ref[0, 0, 0, h:h + 1, :] * o_cmp[:, cols]
               + gt_ref[0, 0, 1, h:h + 1, :] * o_sel[:, cols]
               + gt_ref[0, 0, 2, h:h + 1, :] * o_win[:, cols])
        o_ref[0, :, h * HEAD_DIM:(h + 1) * HEAD_DIM] = o_t.T.astype(o_ref.dtype)


def nsa_attention_core(q, kc, vct, ksw, vswt, gates_t, ov_t):
    B, _, S, _ = q.shape
    n_c = S // CMP_STRIDE
    n_sel = S // SEL_BLK
    nq = GQA * Q_BLK
    hk = N_KV_HEADS
    return pl.pallas_call(
        _nsa_kernel,
        grid=(B, hk, S // Q_BLK),
        in_specs=[
            pl.BlockSpec((1, GQA, Q_BLK, HEAD_DIM), lambda b, h, i: (b, h, i, 0)),
            pl.BlockSpec((1, 1, n_c, HEAD_DIM), lambda b, h, i: (b, h, 0, 0)),
            pl.BlockSpec((1, 1, HEAD_DIM, n_c), lambda b, h, i: (b, hk + h, 0, 0)),
            pl.BlockSpec((1, 1, S, HEAD_DIM), lambda b, h, i: (b, h, 0, 0)),
            pl.BlockSpec((1, 1, S // LANE, HEAD_DIM, LANE), lambda b, h, i: (b, h, 0, 0, 0)),
            pl.BlockSpec((1, 1, S, HEAD_DIM), lambda b, h, i: (b, hk + h, 0, 0)),
            pl.BlockSpec((1, 1, S // LANE, HEAD_DIM, LANE), lambda b, h, i: (b, hk + h, 0, 0, 0)),
            pl.BlockSpec((1, 1, 3, GQA, Q_BLK), lambda b, h, i: (b, h, 0, 0, i)),
            pl.BlockSpec((n_sel, n_c), lambda b, h, i: (0, 0)),
        ],
        out_specs=pl.BlockSpec((1, Q_BLK, GQA * HEAD_DIM), lambda b, h, i: (b, i, h)),
        out_shape=jax.ShapeDtypeStruct((B, S, N_HEADS * HEAD_DIM), BF16),
        scratch_shapes=[pltpu.VMEM((n_sel, nq), F32)],
        compiler_params=_params(("parallel", "parallel", "arbitrary")),
    )(q, kc, vct, ksw, vswt, ksw, vswt, gates_t, ov_t)


def _moe_kernel(nused_ref, bexp_ref, rtok_ref, h_hbm, g_ref, wg_ref, wu_ref, wd_ref, y_ref,
                xbuf, sem):
    i = pl.program_id(0)
    n_used = nused_ref[0]
    blk = xbuf.shape[1]

    def issue(b, slot):
        def body(r, c):
            tok = rtok_ref[b * blk + r]
            pltpu.make_async_copy(h_hbm.at[pl.ds(tok, 1)], xbuf.at[slot, pl.ds(r, 1)],
                                  sem.at[slot]).start()
            return c
        lax.fori_loop(0, blk, body, 0, unroll=8)

    @pl.when(i == 0)
    def _():
        issue(0, 0)

    @pl.when(i + 1 < n_used)
    def _():
        issue(i + 1, (i + 1) % 2)

    @pl.when(i < n_used)
    def _():
        slot = i % 2
        pltpu.make_async_copy(h_hbm.at[pl.ds(0, blk)], xbuf.at[slot], sem.at[slot]).wait()
        x = _rms(xbuf[slot], g_ref[...]).astype(BF16)
        gate = jnp.dot(x, wg_ref[0], preferred_element_type=F32)
        up = jnp.dot(x, wu_ref[0], preferred_element_type=F32)
        hid = (gate * jax.nn.sigmoid(gate) * up).astype(BF16)
        y_ref[...] = jnp.dot(hid, wd_ref[0], preferred_element_type=F32)

    @pl.when(i >= n_used)
    def _():
        y_ref[...] = jnp.zeros_like(y_ref)


def moe_experts(h, g, n_used, block_exp, row_tok, wg, wu, wd, blk=MOE_BLK):
    n_rows = row_tok.shape[0]
    n_blocks = n_rows // blk
    grid_spec = pltpu.PrefetchScalarGridSpec(
        num_scalar_prefetch=3,
        grid=(n_blocks,),
        in_specs=[
            pl.BlockSpec(memory_space=pl.ANY),
            pl.BlockSpec((1, D_MODEL), lambda i, nu, be, rt: (0, 0)),
            pl.BlockSpec((1, D_MODEL, D_EXPERT), lambda i, nu, be, rt: (be[i], 0, 0)),
            pl.BlockSpec((1, D_MODEL, D_EXPERT), lambda i, nu, be, rt: (be[i], 0, 0)),
            pl.BlockSpec((1, D_EXPERT, D_MODEL), lambda i, nu, be, rt: (be[i], 0, 0)),
        ],
        out_specs=pl.BlockSpec((blk, D_MODEL), lambda i, nu, be, rt: (i, 0)),
        scratch_shapes=[pltpu.VMEM((2, blk, D_MODEL), F32), pltpu.SemaphoreType.DMA((2,))],
    )
    return pl.pallas_call(
        _moe_kernel, grid_spec=grid_spec,
        out_shape=jax.ShapeDtypeStruct((n_rows, D_MODEL), F32),
        compiler_params=_params(("arbitrary",)),
    )(n_used, block_exp, row_tok, h, g.reshape(1, D_MODEL), wg, wu, wd)


def _combine_kernel(pos_ref, h_ref, w_ref, g_ref, y_hbm, o_ref, n_ref, ybuf, sem):
    i = pl.program_id(0)
    n = pl.num_programs(0)
    tm = h_ref.shape[0]

    def issue(b, slot):
        def body(j, c):
            for k in range(TOP_K):
                row = pos_ref[(b * tm + j) * TOP_K + k]
                pltpu.make_async_copy(y_hbm.at[pl.ds(row, 1)], ybuf.at[slot, pl.ds(k * tm + j, 1)],
                                      sem.at[slot]).start()
            return c
        lax.fori_loop(0, tm, body, 0, unroll=4)

    @pl.when(i == 0)
    def _():
        issue(0, 0)

    @pl.when(i + 1 < n)
    def _():
        issue(i + 1, (i + 1) % 2)

    slot = i % 2
    pltpu.make_async_copy(y_hbm.at[pl.ds(0, TOP_K * tm)], ybuf.at[slot], sem.at[slot]).wait()
    out = h_ref[...]
    for k in range(TOP_K):
        out = out + ybuf[slot, k * tm:(k + 1) * tm, :] * w_ref[:, k:k + 1]
    o_ref[...] = out
    n_ref[...] = _rms(out, g_ref[...]).astype(n_ref.dtype)


def moe_combine(h, y, pos, weight, g_next, norm_dtype, tm=256):
    T = h.shape[0]
    grid_spec = pltpu.PrefetchScalarGridSpec(
        num_scalar_prefetch=1,
        grid=(T // tm,),
        in_specs=[
            pl.BlockSpec((tm, D_MODEL), lambda i, p: (i, 0)),
            pl.BlockSpec((tm, TOP_K), lambda i, p: (i, 0)),
            pl.BlockSpec((1, D_MODEL), lambda i, p: (0, 0)),
            pl.BlockSpec(memory_space=pl.ANY),
        ],
        out_specs=[pl.BlockSpec((tm, D_MODEL), lambda i, p: (i, 0)),
                   pl.BlockSpec((tm, D_MODEL), lambda i, p: (i, 0))],
        scratch_shapes=[pltpu.VMEM((2, TOP_K * tm, D_MODEL), F32), pltpu.SemaphoreType.DMA((2,))],
    )
    return pl.pallas_call(
        _combine_kernel, grid_spec=grid_spec,
        out_shape=[jax.ShapeDtypeStruct((T, D_MODEL), F32),
                   jax.ShapeDtypeStruct((T, D_MODEL), norm_dtype)],
        compiler_params=_params(("arbitrary",)),
    )(pos.reshape(-1), h, weight, g_next.reshape(1, D_MODEL), y)


def _route(logits, b_group, b_exp, blk):
    T = logits.shape[0]
    g_prob = jax.nn.softmax(logits[:, :N_GROUPS] + b_group, axis=-1)
    g_w, g_idx = lax.top_k(g_prob, 1)
    e_logits = (logits[:, N_GROUPS:N_GROUPS + N_EXPERTS] + b_exp).reshape(T, N_GROUPS, EXP_PER_GROUP)
    e_logits = jnp.take_along_axis(e_logits, g_idx[:, :, None], axis=1)[:, 0]
    e_w, e_idx = lax.top_k(jax.nn.softmax(e_logits, axis=-1), TOP_K)
    e_w = e_w / jnp.sum(e_w, axis=-1, keepdims=True)
    weight = g_w * e_w
    expert = (g_idx * EXP_PER_GROUP + e_idx).astype(jnp.int32).reshape(-1)
    onehot = (expert[:, None] == jnp.arange(N_EXPERTS, dtype=jnp.int32)[None, :]).astype(jnp.int32)
    csum = jnp.cumsum(onehot, axis=0)
    rank = jnp.take_along_axis(csum, expert[:, None], axis=1)[:, 0] - 1
    counts = csum[-1]
    padded = (counts + blk - 1) // blk * blk
    pad_end = jnp.cumsum(padded)
    pad_start = pad_end - padded
    pos = (pad_start[expert] + rank).astype(jnp.int32)
    n_rows = T * TOP_K + N_EXPERTS * blk
    token = jnp.repeat(jnp.arange(T, dtype=jnp.int32), TOP_K)
    row_tok = jnp.zeros((n_rows,), jnp.int32).at[pos].set(token, unique_indices=True)
    n_blocks = n_rows // blk
    block_exp = jnp.minimum(
        jnp.searchsorted(pad_end, jnp.arange(n_blocks, dtype=jnp.int32) * blk, side="right"),
        N_EXPERTS - 1).astype(jnp.int32)
    n_used = (pad_end[-1] // blk).astype(jnp.int32).reshape(1)
    return weight, pos.reshape(T, TOP_K), row_tok, block_exp, n_used


def _moe_layer(h, norm_g, w_group, b_group, w_exp, b_exp, wg, wu, wd, g_next, norm_dtype):
    wr = jnp.zeros((D_MODEL, LANE), F32)
    wr = wr.at[:, :N_GROUPS].set(w_group).at[:, N_GROUPS:N_GROUPS + N_EXPERTS].set(w_exp)
    logits = router_logits(h, norm_g, wr)
    weight, pos, row_tok, block_exp, n_used = _route(logits, b_group, b_exp, MOE_BLK)
    y = moe_experts(h, norm_g, n_used, block_exp, row_tok,
                    wg.astype(BF16), wu.astype(BF16), wd.astype(BF16))
    return moe_combine(h, y, pos, weight, g_next, norm_dtype)


def _gmlp_layer(h, hn, w_in, ln_g, ln_b, w_s, b_s, w_out):
    z = matmul(hn, w_in.astype(BF16), out_dtype=F32, act="gelu")
    causal = jnp.tril(jnp.ones((GM_CHUNK, GM_CHUNK), w_s.dtype))
    bs_full = jnp.repeat(b_s.T, D_MODEL // GM_GROUPS, axis=1)
    y = gmlp_gate(z, ln_g, ln_b, (w_s * causal).astype(BF16), bs_full)
    return matmul(y, w_out.astype(BF16), out_dtype=F32, resid=h)


def _overlap_t(S):
    n_c = S // CMP_STRIDE
    n_sel = S // SEL_BLK
    ci = np.arange(n_c)[None, :] * CMP_STRIDE
    sj = np.arange(n_sel)[:, None] * SEL_BLK
    ov = (ci < sj + SEL_BLK) & (ci + CMP_LEN > sj) & (np.arange(n_c)[None, :] < n_c - 1)
    return jnp.asarray(ov.astype(np.float32), dtype=BF16)


def _nsa_layer(h, hn, B, S, w_in, ck_pe, ck_w1, ck_w2, cv_pe, cv_w1, cv_w2, w_out):
    qd = N_HEADS * HEAD_DIM
    kvd = N_KV_HEADS * HEAD_DIM
    w = w_in.astype(BF16)
    o = [qd + i * kvd for i in range(7)]
    w_q, w_kc, w_vc, w_ks, w_vs, w_kw, w_vw = (
        w[:, :o[0]], w[:, o[0]:o[1]], w[:, o[1]:o[2]], w[:, o[2]:o[3]], w[:, o[3]:o[4]],
        w[:, o[4]:o[5]], w[:, o[5]:o[6]])
    w_g = jnp.zeros((D_MODEL, LANE), BF16).at[:, :3 * N_HEADS].set(w[:, o[6]:])
    q = matmul(hn, w_q, out_dtype=BF16, scale=HEAD_DIM ** -0.5, layout="heads", seq=S)
    kvc = matmul(hn, jnp.concatenate([w_kc, w_vc], 1), out_dtype=F32, layout="heads", seq=S)
    ksw = matmul(hn, jnp.concatenate([w_ks, w_kw], 1), out_dtype=BF16, layout="heads", seq=S)
    vswt = matmul(hn, jnp.concatenate([w_vs, w_vw], 1), out_dtype=BF16, layout="heads_t", seq=S)
    gates = matmul(hn, w_g, out_dtype=F32, act="sigmoid")[:, :3 * N_HEADS]
    gates_t = gates.reshape(B, S, N_KV_HEADS, GQA, 3).transpose(0, 2, 4, 3, 1)

    half = CMP_STRIDE
    pe2 = jnp.stack([jnp.stack([pe[:half].reshape(-1), pe[half:].reshape(-1)])
                     for pe in (ck_pe, cv_pe)])
    w1 = jnp.stack([ck_w1, cv_w1]).astype(BF16)
    w2 = jnp.stack([ck_w2, cv_w2]).astype(BF16)
    cmp, cmp_t = compress_kv(kvc, pe2, w1, w2)
    n_c = S // CMP_STRIDE
    cmp = cmp.reshape(B, 2 * N_KV_HEADS, n_c, HEAD_DIM)
    cmp_t = cmp_t.reshape(B, 2 * N_KV_HEADS, HEAD_DIM, n_c)
    o_att = nsa_attention_core(q, cmp, cmp_t, ksw, vswt, gates_t, _overlap_t(S))
    return matmul(o_att.reshape(B * S, qd), w_out.astype(BF16), out_dtype=F32, resid=h)


def kernel(x, norm_mix, norm_ffn, norm_final, a_w_in, a_ln_g, a_ln_b, a_w_s, a_b_s, a_w_out,
           b_w_in, b_ck_pe, b_ck_w1, b_ck_w2, b_cv_pe, b_cv_w1, b_cv_w2, b_w_out,
           r_w_group, r_b_group, r_w_exp, r_b_exp, e_w_gate, e_w_up, e_w_down):
    B, S, _ = x.shape
    h = x.reshape(B * S, D_MODEL)
    hn = rmsnorm(h, norm_mix[0])
    h = _gmlp_layer(h, hn, a_w_in[0], a_ln_g[0], a_ln_b[0], a_w_s[0], a_b_s[0], a_w_out[0])
    h, hn = _moe_layer(h, norm_ffn[0], r_w_group[0], r_b_group[0], r_w_exp[0], r_b_exp[0],
                       e_w_gate[0], e_w_up[0], e_w_down[0], norm_mix[1], BF16)
    h = _nsa_layer(h, hn, B, S, b_w_in[0], b_ck_pe[0], b_ck_w1[0], b_ck_w2[0],
                   b_cv_pe[0], b_cv_w1[0], b_cv_w2[0], b_w_out[0])
    _, out = _moe_layer(h, norm_ffn[1], r_w_group[1], r_b_group[1], r_w_exp[1], r_b_exp[1],
                        e_w_gate[1], e_w_up[1], e_w_down[1], norm_final, F32)
    return out.reshape(B, S, D_MODEL)
```

```python
import functools

import numpy as np
import jax
import jax.numpy as jnp
from jax import lax
from jax.experimental import pallas as pl
from jax.experimental.pallas import tpu as pltpu

F32 = jnp.float32
BF16 = jnp.bfloat16

D_MODEL = 2048
LANE = 128
GM_GROUPS = 8
GM_CHUNK = 128
HEAD_DIM = 128
N_HEADS = 16
N_KV_HEADS = 4
GQA = N_HEADS // N_KV_HEADS
CMP_LEN = 32
CMP_STRIDE = 16
SEL_BLK = 64
SEL_TOPN = 16
SEL_LOCAL = 2
WINDOW = 512
Q_BLK = 128
FORCE = 1.0e4
N_GROUPS = 4
EXP_PER_GROUP = 8
N_EXPERTS = N_GROUPS * EXP_PER_GROUP
TOP_K = 2
D_EXPERT = 512
EPS = 1e-6
NEG = -1e30
LOG2E = 1.4426950408889634
SEL_TILE = 512
MOE_BLK = 512
VMEM_LIMIT = 56 * 1024 * 1024

_NT = (((1,), (1,)), ((), ()))


def _params(sem):
    return pltpu.CompilerParams(dimension_semantics=sem, vmem_limit_bytes=VMEM_LIMIT)


def _gelu(x):
    return 0.5 * x * (1.0 + jnp.tanh(0.7978845608028654 * (x + 0.044715 * (x * x * x))))


def _rms(x, g):
    y = x * lax.rsqrt(jnp.mean(x * x, axis=-1, keepdims=True) + EPS)
    return y * g


def _rmsnorm_kernel(x_ref, g_ref, o_ref):
    o_ref[...] = _rms(x_ref[...], g_ref[...]).astype(o_ref.dtype)


def rmsnorm(x, g, out_dtype=BF16, tm=512):
    T = x.shape[0]
    return pl.pallas_call(
        _rmsnorm_kernel,
        grid=(T // tm,),
        in_specs=[pl.BlockSpec((tm, D_MODEL), lambda i: (i, 0)),
                  pl.BlockSpec((1, D_MODEL), lambda i: (0, 0))],
        out_specs=pl.BlockSpec((tm, D_MODEL), lambda i: (i, 0)),
        out_shape=jax.ShapeDtypeStruct((T, D_MODEL), out_dtype),
        compiler_params=_params(("parallel",)),
    )(x, g.reshape(1, D_MODEL))


def _router_kernel(x_ref, g_ref, wr_ref, br_ref, e_ref, w_ref):
    y = _rms(x_ref[...], g_ref[...])
    lg = jnp.dot(y, wr_ref[...], precision=lax.Precision.HIGHEST, preferred_element_type=F32)
    lt = lg.T + br_ref[...]
    row = lambda i: lt[i:i + 1, :]

    def softmax(rows):
        m = functools.reduce(jnp.maximum, rows)
        e = [jnp.exp(r - m) for r in rows]
        s = functools.reduce(lambda a, b: a + b, e)
        return [v / s for v in e]

    def top1(vals, skip=None):
        best = jnp.full_like(vals[0], -1.0)
        idx = jnp.zeros(vals[0].shape, jnp.int32)
        for k, v in enumerate(vals):
            if skip is not None:
                v = jnp.where(skip == k, -1.0, v)
            upd = v > best
            best = jnp.where(upd, v, best)
            idx = jnp.where(upd, k, idx)
        return best, idx

    g_w, g_idx = top1(softmax([row(j) for j in range(N_GROUPS)]))
    e_logits = []
    for k in range(EXP_PER_GROUP):
        v = row(N_GROUPS + k)
        for gg in range(1, N_GROUPS):
            v = jnp.where(g_idx == gg, row(N_GROUPS + gg * EXP_PER_GROUP + k), v)
        e_logits.append(v)
    e_prob = softmax(e_logits)
    w1, i1 = top1(e_prob)
    w2, i2 = top1(e_prob, skip=i1)
    tot = w1 + w2
    zi = jnp.zeros((6, lt.shape[1]), jnp.int32)
    e_ref[...] = jnp.concatenate([g_idx * EXP_PER_GROUP + i1, g_idx * EXP_PER_GROUP + i2, zi], axis=0)
    w_ref[...] = jnp.concatenate([g_w * (w1 / tot), g_w * (w2 / tot), zi.astype(F32)], axis=0)


def router(x, g, wr, br, tm=512):
    T = x.shape[0]
    return pl.pallas_call(
        _router_kernel,
        grid=(T // tm,),
        in_specs=[pl.BlockSpec((tm, D_MODEL), lambda i: (i, 0)),
                  pl.BlockSpec((1, D_MODEL), lambda i: (0, 0)),
                  pl.BlockSpec((D_MODEL, LANE), lambda i: (0, 0)),
                  pl.BlockSpec((LANE, 1), lambda i: (0, 0))],
        out_specs=[pl.BlockSpec((8, tm), lambda i: (0, i)),
                   pl.BlockSpec((8, tm), lambda i: (0, i))],
        out_shape=[jax.ShapeDtypeStruct((8, T), jnp.int32),
                   jax.ShapeDtypeStruct((8, T), F32)],
        compiler_params=_params(("parallel",)),
    )(x, g.reshape(1, D_MODEL), wr, br)


def _mm_kernel(*refs, act, scale, has_resid, layout):
    a_ref, w_ref = refs[0], refs[1]
    o_ref = refs[-1]
    acc = jnp.dot(a_ref[...], w_ref[...], preferred_element_type=F32)
    if scale is not None:
        acc = acc * scale
    if act == "gelu":
        acc = _gelu(acc)
    elif act == "sigmoid":
        acc = jax.nn.sigmoid(acc)
    if has_resid:
        acc = acc + refs[2][...]
    tm, tn = acc.shape
    if layout == "plain":
        o_ref[...] = acc.astype(o_ref.dtype)
    elif layout == "heads":
        for h in range(tn // HEAD_DIM):
            o_ref[0, h] = acc[:, h * HEAD_DIM:(h + 1) * HEAD_DIM].astype(o_ref.dtype)
    else:
        for h in range(tn // HEAD_DIM):
            for c in range(tm // LANE):
                blk = acc[c * LANE:(c + 1) * LANE, h * HEAD_DIM:(h + 1) * HEAD_DIM]
                o_ref[0, h, c] = blk.T.astype(o_ref.dtype)


def matmul(a, w, *, out_dtype, act=None, scale=None, resid=None, layout="plain",
           seq=None, tm=512, tn=1024):
    M, K = a.shape
    N = w.shape[1]
    tn = min(tn, N)
    grid = (N // tn, M // tm)
    in_specs = [pl.BlockSpec((tm, K), lambda j, i: (i, 0)),
                pl.BlockSpec((K, tn), lambda j, i: (0, j))]
    args = [a, w]
    if resid is not None:
        in_specs.append(pl.BlockSpec((tm, tn), lambda j, i: (i, j)))
        args.append(resid)
    if layout == "plain":
        out_shape = jax.ShapeDtypeStruct((M, N), out_dtype)
        out_spec = pl.BlockSpec((tm, tn), lambda j, i: (i, j))
    else:
        nb = seq // tm
        nh = tn // HEAD_DIM
        if layout == "heads":
            out_shape = jax.ShapeDtypeStruct((M // seq, N // HEAD_DIM, seq, HEAD_DIM), out_dtype)
            out_spec = pl.BlockSpec((1, nh, tm, HEAD_DIM), lambda j, i: (i // nb, j, i % nb, 0))
        else:
            out_shape = jax.ShapeDtypeStruct(
                (M // seq, N // HEAD_DIM, seq // LANE, HEAD_DIM, LANE), out_dtype)
            out_spec = pl.BlockSpec((1, nh, tm // LANE, HEAD_DIM, LANE),
                                    lambda j, i: (i // nb, j, i % nb, 0, 0))
    kern = functools.partial(_mm_kernel, act=act, scale=scale, has_resid=resid is not None,
                             layout=layout)
    return pl.pallas_call(
        kern, grid=grid, in_specs=in_specs, out_specs=out_spec, out_shape=out_shape,
        compiler_params=_params(("parallel", "parallel")),
    )(*args)


def _gate_kernel(z_ref, lng_ref, lnb_ref, ws_ref, bs_ref, y_ref):
    tm = z_ref.shape[0]
    u = z_ref[:, :D_MODEL]
    v = z_ref[:, D_MODEL:]
    mu = jnp.mean(v, axis=-1, keepdims=True)
    vc = v - mu
    vn = vc * lax.rsqrt(jnp.mean(vc * vc, axis=-1, keepdims=True) + EPS)
    vn = (vn * lng_ref[...] + lnb_ref[...]).astype(BF16)
    gd = D_MODEL // GM_GROUPS
    for c in range(tm // GM_CHUNK):
        rows = slice(c * GM_CHUNK, (c + 1) * GM_CHUNK)
        for g in range(GM_GROUPS):
            cols = slice(g * gd, (g + 1) * gd)
            sv = jnp.dot(ws_ref[g], vn[rows, cols], preferred_element_type=F32)
            y_ref[rows, cols] = (u[rows, cols] * (sv + bs_ref[:, cols])).astype(y_ref.dtype)


def gmlp_gate(z, ln_g, ln_b, ws_masked, bs_full, tm=256):
    T = z.shape[0]
    return pl.pallas_call(
        _gate_kernel,
        grid=(T // tm,),
        in_specs=[pl.BlockSpec((tm, 2 * D_MODEL), lambda i: (i, 0)),
                  pl.BlockSpec((1, D_MODEL), lambda i: (0, 0)),
                  pl.BlockSpec((1, D_MODEL), lambda i: (0, 0)),
                  pl.BlockSpec((GM_GROUPS, GM_CHUNK, GM_CHUNK), lambda i: (0, 0, 0)),
                  pl.BlockSpec((GM_CHUNK, D_MODEL), lambda i: (0, 0))],
        out_specs=pl.BlockSpec((tm, D_MODEL), lambda i: (i, 0)),
        out_shape=jax.ShapeDtypeStruct((T, D_MODEL), BF16),
        compiler_params=_params(("parallel",)),
    )(z, ln_g.reshape(1, D_MODEL), ln_b.reshape(1, D_MODEL), ws_masked, bs_full)


def _compress_kernel(x_ref, pe_ref, w1_ref, w2_ref, o_ref, ot_ref):
    x = x_ref[0]
    half = CMP_STRIDE * HEAD_DIM
    xa = (x + pe_ref[0, 0:1, :]).astype(BF16)
    xb = (x + pe_ref[0, 1:2, :]).astype(BF16)
    a = jnp.dot(xa, w1_ref[0, :half, :], preferred_element_type=F32)
    b = jnp.dot(xb, w1_ref[0, half:, :], preferred_element_type=F32)
    n = a.shape[0]
    pre = a + pltpu.roll(b, n - 1, axis=0)
    out = jnp.dot(_gelu(pre).astype(BF16), w2_ref[0], preferred_element_type=F32)
    o_ref[0] = out.astype(o_ref.dtype)
    ot_ref[0] = out.T.astype(ot_ref.dtype)


def compress_kv(kvc, pe2, w1, w2):
    B, H2, S, _ = kvc.shape
    n_grp = S // CMP_STRIDE
    x = kvc.reshape(B * H2, n_grp, CMP_STRIDE * HEAD_DIM)
    sel = lambda i: ((i % H2) // N_KV_HEADS, 0, 0)
    return pl.pallas_call(
        _compress_kernel,
        grid=(B * H2,),
        in_specs=[pl.BlockSpec((1, n_grp, CMP_STRIDE * HEAD_DIM), lambda i: (i, 0, 0)),
                  pl.BlockSpec((1, 2, CMP_STRIDE * HEAD_DIM), sel),
                  pl.BlockSpec((1, CMP_LEN * HEAD_DIM, HEAD_DIM), sel),
                  pl.BlockSpec((1, HEAD_DIM, HEAD_DIM), sel)],
        out_specs=[pl.BlockSpec((1, n_grp, HEAD_DIM), lambda i: (i, 0, 0)),
                   pl.BlockSpec((1, HEAD_DIM, n_grp), lambda i: (i, 0, 0))],
        out_shape=[jax.ShapeDtypeStruct((B * H2, n_grp, HEAD_DIM), BF16),
                   jax.ShapeDtypeStruct((B * H2, HEAD_DIM, n_grp), BF16)],
        compiler_params=_params(("parallel",)),
    )(x, pe2, w1, w2)


def _nsa_kernel(q_ref, kc_ref, vct_ref, ksa_ref, vst_ref, kw_ref, vwt_ref, gt_ref, ovt_ref, wm_ref,
                o_ref, qaug_ref, s_a, s_b):
    nq = GQA * Q_BLK
    qb = pl.program_id(2)
    q0 = qb * Q_BLK
    q = q_ref[0].reshape(nq, HEAD_DIM)
    t_lane = q0 + (lax.broadcasted_iota(jnp.int32, (1, nq), 1) & (Q_BLK - 1))

    n_c = kc_ref.shape[2]
    s = lax.dot_general(kc_ref[0, 0], q, _NT, preferred_element_type=F32)
    c_end = lax.broadcasted_iota(jnp.int32, (n_c, 1), 0) * CMP_STRIDE + (CMP_LEN - 1)
    s = jnp.where(c_end <= t_lane, s, NEG)
    m = jnp.max(s, axis=0, keepdims=True)
    p = jnp.exp2(s - m)
    l = jnp.sum(p, axis=0, keepdims=True)
    inv = jnp.where(m > 0.5 * NEG, 1.0 / l, 0.0)
    p = p * inv
    o_cmp = jnp.dot(vct_ref[0, 0], p.astype(BF16), preferred_element_type=F32)

    psum = p[:, 0:Q_BLK]
    for h in range(1, GQA):
        psum = psum + p[:, h * Q_BLK:(h + 1) * Q_BLK]
    p_hi = psum.astype(BF16)
    p_lo = (psum - p_hi.astype(F32)).astype(BF16)
    imp = (jnp.dot(ovt_ref[...], p_hi, preferred_element_type=F32)
           + jnp.dot(ovt_ref[...], p_lo, preferred_element_type=F32))

    n_sel = imp.shape[0]
    blk = lax.broadcasted_iota(jnp.int32, (n_sel, Q_BLK), 0)
    t_q = q0 + lax.broadcasted_iota(jnp.int32, (n_sel, Q_BLK), 1)
    dist = (t_q // SEL_BLK) - blk
    forced = (blk == 0) | ((dist >= 0) & (dist < SEL_LOCAL))
    score = jnp.where(forced, -jnp.inf, jnp.where(blk * SEL_BLK <= t_q, imp, NEG))
    blk_f = blk.astype(F32)
    for _ in range(SEL_TOPN - 1 - SEL_LOCAL):
        top = jnp.max(score, axis=0, keepdims=True)
        first = jnp.min(jnp.where(score == top, blk_f, float(n_sel)), axis=0, keepdims=True)
        score = jnp.where(blk_f == first, -jnp.inf, score)
    bias = jnp.where(score == -jnp.inf, 0.0, NEG)

    n_half = qaug_ref.shape[0]
    if n_sel < LANE:
        bias = jnp.concatenate([bias, jnp.zeros((LANE - n_sel, Q_BLK), F32)], axis=0)
    for hf in range(n_half):
        bq = bias[hf * LANE:(hf + 1) * LANE, :].T.astype(BF16)
        qaug_ref[hf] = jnp.concatenate([q, jnp.concatenate([bq] * GQA, axis=0)], axis=1)

    wk = WINDOW + Q_BLK
    kstart = pl.multiple_of(jnp.maximum(q0 - WINDOW, 0), LANE)
    sw = lax.dot_general(kw_ref[0, 0, pl.ds(kstart, wk), :], q, _NT,
                         preferred_element_type=F32)
    sw = sw + jnp.concatenate([wm_ref[0]] * GQA, axis=1)
    m_w = jnp.max(sw, axis=0, keepdims=True)
    p_w = jnp.exp2(sw - m_w)
    l_w = jnp.sum(p_w, axis=0, keepdims=True)
    vw = jnp.concatenate([vwt_ref[0, 0, kstart // LANE + j] for j in range(wk // LANE)], axis=1)
    o_win = jnp.dot(vw, p_w.astype(BF16), preferred_element_type=F32) * (1.0 / l_w)

    bpt = SEL_TILE // SEL_BLK
    vpt = SEL_TILE // LANE

    def scores(kt):
        k0 = pl.multiple_of(kt * SEL_TILE, SEL_TILE)
        hf = (kt * bpt) // LANE
        return lax.dot_general(ksa_ref[0, 0, pl.ds(k0, SEL_TILE), :], qaug_ref[hf], _NT,
                               preferred_element_type=F32)

    def softmax_pv(kt, st, carry):
        m_i, l_i, acc = carry
        m_new = jnp.maximum(m_i, jnp.max(st, axis=0, keepdims=True))
        alpha = jnp.exp2(m_i - m_new)
        pt = jnp.exp2(st - m_new)
        l_new = alpha * l_i + jnp.sum(pt, axis=0, keepdims=True)
        vt = jnp.concatenate([vst_ref[0, 0, kt * vpt + j] for j in range(vpt)], axis=1)
        acc = alpha * acc + jnp.dot(vt, pt.astype(BF16), preferred_element_type=F32)
        return m_new, l_new, acc

    def pair(j, carry):
        st = s_a[...]
        s_b[...] = scores(2 * j + 1)
        carry = softmax_pv(2 * j, st, carry)
        st = s_b[...]
        s_a[...] = scores(2 * j + 2)
        return softmax_pv(2 * j + 1, st, carry)

    def causal(kt, st):
        kpos = kt * SEL_TILE + lax.broadcasted_iota(jnp.int32, (SEL_TILE, 1), 0)
        return jnp.where(kpos <= t_lane, st, NEG)

    init = (jnp.full((1, nq), NEG, F32), jnp.zeros((1, nq), F32), jnp.zeros((HEAD_DIM, nq), F32))
    n_pairs = q0 // (2 * SEL_TILE)
    s_a[...] = scores(0)
    carry = lax.fori_loop(0, n_pairs, pair, init)
    carry = softmax_pv(2 * n_pairs, causal(2 * n_pairs, s_a[...]), carry)
    last = 2 * n_pairs + 1
    _, l_s, acc_s = lax.cond(
        q0 + Q_BLK > last * SEL_TILE,
        lambda c: softmax_pv(last, causal(last, scores(last)), c),
        lambda c: c, carry)
    o_sel = acc_s * (1.0 / l_s)

    for h in range(GQA):
        cols = slice(h * Q_BLK, (h + 1) * Q_BLK)
        o_t = (gt_ref[0, 0, 0, h:h + 1, :] * o_cmp[:, cols]
               + gt_ref[0, 0, 1, h:h + 1, :] * o_sel[:, cols]
               + gt_ref[0, 0, 2, h:h + 1, :] * o_win[:, cols])
        o_ref[0, :, h * HEAD_DIM:(h + 1) * HEAD_DIM] = o_t.T.astype(o_ref.dtype)


def _window_mask(S):
    wk = WINDOW + Q_BLK
    i = np.arange(wk)[:, None]
    qi = np.arange(Q_BLK)[None, :]
    out = []
    for v in range(WINDOW // Q_BLK + 1):
        t = Q_BLK * v + qi if v < WINDOW // Q_BLK else WINDOW + qi
        out.append(np.where((i <= t) & (i > t - WINDOW), 0.0, NEG))
    return jnp.asarray(np.stack(out), dtype=F32)


def nsa_attention_core(q, kc, vct, ks_aug, kw, vswt, gates_t, ov_t):
    B, _, S, _ = q.shape
    n_c = S // CMP_STRIDE
    n_sel = S // SEL_BLK
    nq = GQA * Q_BLK
    hk = N_KV_HEADS
    wk = WINDOW + Q_BLK
    nv = WINDOW // Q_BLK
    once = pl.Buffered(1)
    return pl.pallas_call(
        _nsa_kernel,
        grid=(B, hk, S // Q_BLK),
        in_specs=[
            pl.BlockSpec((1, GQA, Q_BLK, HEAD_DIM), lambda b, h, i: (b, h, i, 0)),
            pl.BlockSpec((1, 1, n_c, HEAD_DIM), lambda b, h, i: (b, h, 0, 0)),
            pl.BlockSpec((1, 1, HEAD_DIM, n_c), lambda b, h, i: (b, hk + h, 0, 0)),
            pl.BlockSpec((1, 1, S, 2 * HEAD_DIM), lambda b, h, i: (b, h, 0, 0), pipeline_mode=once),
            pl.BlockSpec((1, 1, S // LANE, HEAD_DIM, LANE), lambda b, h, i: (b, h, 0, 0, 0),
                         pipeline_mode=once),
            pl.BlockSpec((1, 1, S, HEAD_DIM), lambda b, h, i: (b, hk + h, 0, 0), pipeline_mode=once),
            pl.BlockSpec((1, 1, S // LANE, HEAD_DIM, LANE), lambda b, h, i: (b, hk + h, 0, 0, 0),
                         pipeline_mode=once),
            pl.BlockSpec((1, 1, 3, GQA, Q_BLK), lambda b, h, i: (b, h, 0, 0, i)),
            pl.BlockSpec((n_sel, n_c), lambda b, h, i: (0, 0)),
            pl.BlockSpec((1, wk, Q_BLK), lambda b, h, i: (jnp.minimum(i, nv), 0, 0)),
        ],
        out_specs=pl.BlockSpec((1, Q_BLK, GQA * HEAD_DIM), lambda b, h, i: (b, i, h)),
        out_shape=jax.ShapeDtypeStruct((B, S, N_HEADS * HEAD_DIM), BF16),
        scratch_shapes=[pltpu.VMEM((max(1, n_sel // LANE), nq, 2 * HEAD_DIM), BF16),
                        pltpu.VMEM((SEL_TILE, nq), F32), pltpu.VMEM((SEL_TILE, nq), F32)],
        compiler_params=_params(("parallel", "parallel", "arbitrary")),
    )(q, kc, vct, ks_aug, vswt, kw, vswt, gates_t, ov_t, _window_mask(S))


def _moe_kernel(nused_ref, bexp_ref, rtok_ref, h_hbm, g_ref, wg_ref, wu_ref, wd_ref, y_ref,
                xbuf, sem):
    i = pl.program_id(0)
    n_used = nused_ref[0]
    blk = xbuf.shape[1]

    def issue(b, slot):
        def body(r, c):
            tok = rtok_ref[b * blk + r]
            pltpu.make_async_copy(h_hbm.at[pl.ds(tok, 1)], xbuf.at[slot, pl.ds(r, 1)],
                                  sem.at[slot]).start()
            return c
        lax.fori_loop(0, blk, body, 0, unroll=8)

    @pl.when(i == 0)
    def _():
        issue(0, 0)

    @pl.when(i + 1 < n_used)
    def _():
        issue(i + 1, (i + 1) % 2)

    @pl.when(i < n_used)
    def _():
        slot = i % 2
        pltpu.make_async_copy(h_hbm.at[pl.ds(0, blk)], xbuf.at[slot], sem.at[slot]).wait()
        x = _rms(xbuf[slot], g_ref[...]).astype(BF16)
        gate = jnp.dot(x, wg_ref[0], preferred_element_type=F32)
        up = jnp.dot(x, wu_ref[0], preferred_element_type=F32)
        hid = (gate * jax.nn.sigmoid(gate) * up).astype(BF16)
        y_ref[...] = jnp.dot(hid, wd_ref[0], preferred_element_type=F32)

    @pl.when(i >= n_used)
    def _():
        y_ref[...] = jnp.zeros_like(y_ref)


def moe_experts(h, g, n_used, block_exp, row_tok, wg, wu, wd, blk=MOE_BLK):
    n_rows = row_tok.shape[0]
    n_blocks = n_rows // blk
    grid_spec = pltpu.PrefetchScalarGridSpec(
        num_scalar_prefetch=3,
        grid=(n_blocks,),
        in_specs=[
            pl.BlockSpec(memory_space=pl.ANY),
            pl.BlockSpec((1, D_MODEL), lambda i, nu, be, rt: (0, 0)),
            pl.BlockSpec((1, D_MODEL, D_EXPERT), lambda i, nu, be, rt: (be[i], 0, 0)),
            pl.BlockSpec((1, D_MODEL, D_EXPERT), lambda i, nu, be, rt: (be[i], 0, 0)),
            pl.BlockSpec((1, D_EXPERT, D_MODEL), lambda i, nu, be, rt: (be[i], 0, 0)),
        ],
        out_specs=pl.BlockSpec((blk, D_MODEL), lambda i, nu, be, rt: (i, 0)),
        scratch_shapes=[pltpu.VMEM((2, blk, D_MODEL), F32), pltpu.SemaphoreType.DMA((2,))],
    )
    return pl.pallas_call(
        _moe_kernel, grid_spec=grid_spec,
        out_shape=jax.ShapeDtypeStruct((n_rows, D_MODEL), F32),
        compiler_params=_params(("arbitrary",)),
    )(n_used, block_exp, row_tok, h, g.reshape(1, D_MODEL), wg, wu, wd)


def _combine_kernel(pos_ref, h_ref, w_ref, g_ref, y_hbm, o_ref, n_ref, ybuf, sem):
    i = pl.program_id(0)
    n = pl.num_programs(0)
    tm = h_ref.shape[0]

    def issue(b, slot):
        def body(j, c):
            for k in range(TOP_K):
                row = pos_ref[(b * tm + j) * TOP_K + k]
                pltpu.make_async_copy(y_hbm.at[pl.ds(row, 1)], ybuf.at[slot, pl.ds(k * tm + j, 1)],
                                      sem.at[slot]).start()
            return c
        lax.fori_loop(0, tm, body, 0, unroll=4)

    @pl.when(i == 0)
    def _():
        issue(0, 0)

    @pl.when(i + 1 < n)
    def _():
        issue(i + 1, (i + 1) % 2)

    slot = i % 2
    pltpu.make_async_copy(y_hbm.at[pl.ds(0, TOP_K * tm)], ybuf.at[slot], sem.at[slot]).wait()
    out = h_ref[...]
    for k in range(TOP_K):
        out = out + ybuf[slot, k * tm:(k + 1) * tm, :] * w_ref[:, k:k + 1]
    o_ref[...] = out
    n_ref[...] = _rms(out, g_ref[...]).astype(n_ref.dtype)


def moe_combine(h, y, pos, weight, g_next, norm_dtype, tm=256):
    T = h.shape[0]
    grid_spec = pltpu.PrefetchScalarGridSpec(
        num_scalar_prefetch=1,
        grid=(T // tm,),
        in_specs=[
            pl.BlockSpec((tm, D_MODEL), lambda i, p: (i, 0)),
            pl.BlockSpec((tm, TOP_K), lambda i, p: (i, 0)),
            pl.BlockSpec((1, D_MODEL), lambda i, p: (0, 0)),
            pl.BlockSpec(memory_space=pl.ANY),
        ],
        out_specs=[pl.BlockSpec((tm, D_MODEL), lambda i, p: (i, 0)),
                   pl.BlockSpec((tm, D_MODEL), lambda i, p: (i, 0))],
        scratch_shapes=[pltpu.VMEM((2, TOP_K * tm, D_MODEL), F32), pltpu.SemaphoreType.DMA((2,))],
    )
    return pl.pallas_call(
        _combine_kernel, grid_spec=grid_spec,
        out_shape=[jax.ShapeDtypeStruct((T, D_MODEL), F32),
                   jax.ShapeDtypeStruct((T, D_MODEL), norm_dtype)],
        compiler_params=_params(("arbitrary",)),
    )(pos.reshape(-1), h, weight, g_next.reshape(1, D_MODEL), y)


def _dispatch(expert, T, blk):
    onehot = (expert[:, None] == jnp.arange(N_EXPERTS, dtype=jnp.int32)[None, :]).astype(jnp.int32)
    csum = jnp.cumsum(onehot, axis=0)
    rank = jnp.sum(csum * onehot, axis=1) - 1
    counts = csum[-1]
    padded = (counts + blk - 1) // blk * blk
    pad_end = jnp.cumsum(padded)
    pad_start = pad_end - padded
    pos = (jnp.sum(pad_start[None, :] * onehot, axis=1) + rank).astype(jnp.int32)
    n_rows = T * TOP_K + N_EXPERTS * blk
    token = jnp.repeat(jnp.arange(T, dtype=jnp.int32), TOP_K)
    row_tok = jnp.zeros((n_rows,), jnp.int32).at[pos].set(token, unique_indices=True)
    n_blocks = n_rows // blk
    starts = jnp.arange(n_blocks, dtype=jnp.int32) * blk
    block_exp = jnp.minimum(jnp.sum((pad_end[None, :] <= starts[:, None]).astype(jnp.int32), axis=1),
                            N_EXPERTS - 1).astype(jnp.int32)
    n_used = (pad_end[-1] // blk).astype(jnp.int32).reshape(1)
    return pos.reshape(T, TOP_K), row_tok, block_exp, n_used


def _moe_layer(h, norm_g, w_group, b_group, w_exp, b_exp, wg, wu, wd, g_next, norm_dtype):
    T = h.shape[0]
    wr = jnp.zeros((D_MODEL, LANE), F32)
    wr = wr.at[:, :N_GROUPS].set(w_group).at[:, N_GROUPS:N_GROUPS + N_EXPERTS].set(w_exp)
    br = jnp.zeros((LANE,), F32).at[:N_GROUPS].set(b_group).at[N_GROUPS:N_GROUPS + N_EXPERTS].set(b_exp)
    e_t, w_t = router(h, norm_g, wr, br.reshape(LANE, 1))
    expert = e_t[:TOP_K].T.reshape(-1)
    weight = w_t[:TOP_K].T
    pos, row_tok, block_exp, n_used = _dispatch(expert, T, MOE_BLK)
    y = moe_experts(h, norm_g, n_used, block_exp, row_tok,
                    wg.astype(BF16), wu.astype(BF16), wd.astype(BF16))
    return moe_combine(h, y, pos, weight, g_next, norm_dtype)


def _gmlp_layer(h, hn, w_in, ln_g, ln_b, w_s, b_s, w_out):
    z = matmul(hn, w_in.astype(BF16), out_dtype=F32, act="gelu")
    causal = jnp.tril(jnp.ones((GM_CHUNK, GM_CHUNK), w_s.dtype))
    bs_full = jnp.repeat(b_s.T, D_MODEL // GM_GROUPS, axis=1)
    y = gmlp_gate(z, ln_g, ln_b, (w_s * causal).astype(BF16), bs_full)
    return matmul(y, w_out.astype(BF16), out_dtype=F32, resid=h)


def _overlap_t(S):
    n_c = S // CMP_STRIDE
    n_sel = S // SEL_BLK
    ci = np.arange(n_c)[None, :] * CMP_STRIDE
    sj = np.arange(n_sel)[:, None] * SEL_BLK
    ov = (ci < sj + SEL_BLK) & (ci + CMP_LEN > sj) & (np.arange(n_c)[None, :] < n_c - 1)
    return jnp.asarray(ov.astype(np.float32), dtype=BF16)


def _nsa_layer(h, hn, B, S, w_in, ck_pe, ck_w1, ck_w2, cv_pe, cv_w1, cv_w2, w_out):
    qd = N_HEADS * HEAD_DIM
    kvd = N_KV_HEADS * HEAD_DIM
    w = w_in.astype(BF16)
    o = [qd + i * kvd for i in range(7)]
    w_q, w_kc, w_vc, w_ks, w_vs, w_kw, w_vw = (
        w[:, :o[0]], w[:, o[0]:o[1]], w[:, o[1]:o[2]], w[:, o[2]:o[3]], w[:, o[3]:o[4]],
        w[:, o[4]:o[5]], w[:, o[5]:o[6]])
    w_g = jnp.zeros((D_MODEL, LANE), BF16).at[:, :3 * N_HEADS].set(w[:, o[6]:])
    q = matmul(hn, w_q, out_dtype=BF16, scale=HEAD_DIM ** -0.5 * LOG2E, layout="heads", seq=S)
    kvc = matmul(hn, jnp.concatenate([w_kc, w_vc], 1), out_dtype=F32, layout="heads", seq=S)
    ksw = matmul(hn, jnp.concatenate([w_ks, w_kw], 1), out_dtype=BF16, layout="heads", seq=S)
    vswt = matmul(hn, jnp.concatenate([w_vs, w_vw], 1), out_dtype=BF16, layout="heads_t", seq=S)
    gates = matmul(hn, w_g, out_dtype=F32, act="sigmoid")[:, :3 * N_HEADS]
    gates_t = gates.reshape(B, S, N_KV_HEADS, GQA, 3).transpose(0, 2, 4, 3, 1)

    half = CMP_STRIDE
    pe2 = jnp.stack([jnp.stack([pe[:half].reshape(-1), pe[half:].reshape(-1)])
                     for pe in (ck_pe, cv_pe)])
    w1 = jnp.stack([ck_w1, cv_w1]).astype(BF16)
    w2 = jnp.stack([ck_w2, cv_w2]).astype(BF16)
    cmp, cmp_t = compress_kv(kvc, pe2, w1, w2)
    n_c = S // CMP_STRIDE
    cmp = cmp.reshape(B, 2 * N_KV_HEADS, n_c, HEAD_DIM)
    cmp_t = cmp_t.reshape(B, 2 * N_KV_HEADS, HEAD_DIM, n_c)
    blk_id = (np.arange(S) // SEL_BLK) % LANE
    onehot = jnp.asarray(blk_id[:, None] == np.arange(LANE)[None, :], dtype=BF16)
    ks_aug = jnp.concatenate(
        [ksw[:, :N_KV_HEADS], jnp.broadcast_to(onehot, (B, N_KV_HEADS, S, LANE))], axis=-1)
    o_att = nsa_attention_core(q, cmp, cmp_t, ks_aug, ksw, vswt, gates_t, _overlap_t(S))
    return matmul(o_att.reshape(B * S, qd), w_out.astype(BF16), out_dtype=F32, resid=h)


def kernel(x, norm_mix, norm_ffn, norm_final, a_w_in, a_ln_g, a_ln_b, a_w_s, a_b_s, a_w_out,
           b_w_in, b_ck_pe, b_ck_w1, b_ck_w2, b_cv_pe, b_cv_w1, b_cv_w2, b_w_out,
           r_w_group, r_b_group, r_w_exp, r_b_exp, e_w_gate, e_w_up, e_w_down):
    B, S, _ = x.shape
    h = x.reshape(B * S, D_MODEL)
    hn = rmsnorm(h, norm_mix[0])
    h = _gmlp_layer(h, hn, a_w_in[0], a_ln_g[0], a_ln_b[0], a_w_s[0], a_b_s[0], a_w_out[0])
    h, hn = _moe_layer(h, norm_ffn[0], r_w_group[0], r_b_group[0], r_w_exp[0], r_b_exp[0],
                       e_w_gate[0], e_w_up[0], e_w_down[0], norm_mix[1], BF16)
    h = _nsa_layer(h, hn, B, S, b_w_in[0], b_ck_pe[0], b_ck_w1[0], b_ck_w2[0],
                   b_cv_pe[0], b_cv_w1[0], b_cv_w2[0], b_w_out[0])
    _, out = _moe_layer(h, norm_ffn[1], r_w_group[1], r_b_group[1], r_w_exp[1], r_b_exp[1],
                        e_w_gate[1], e_w_up[1], e_w_down[1], norm_final, F32)
    return out.reshape(B, S, D_MODEL)
```

```python
import functools

import numpy as np
import jax
import jax.numpy as jnp
from jax import lax
from jax.experimental import pallas as pl
from jax.experimental.pallas import tpu as pltpu

F32 = jnp.float32
BF16 = jnp.bfloat16

D_MODEL = 2048
LANE = 128
GM_GROUPS = 8
GM_CHUNK = 128
HEAD_DIM = 128
N_HEADS = 16
N_KV_HEADS = 4
GQA = N_HEADS // N_KV_HEADS
CMP_LEN = 32
CMP_STRIDE = 16
SEL_BLK = 64
SEL_TOPN = 16
SEL_LOCAL = 2
WINDOW = 512
Q_BLK = 128
FORCE = 1.0e4
N_GROUPS = 4
EXP_PER_GROUP = 8
N_EXPERTS = N_GROUPS * EXP_PER_GROUP
TOP_K = 2
D_EXPERT = 512
EPS = 1e-6
NEG = -1e30
LOG2E = 1.4426950408889634
ONES_ROWS = 16
SEL_TILE = 512
MOE_BLK = 512
VMEM_LIMIT = 56 * 1024 * 1024

_NT = (((1,), (1,)), ((), ()))


def _params(sem):
    return pltpu.CompilerParams(dimension_semantics=sem, vmem_limit_bytes=VMEM_LIMIT)


def _gelu(x):
    return 0.5 * x * (1.0 + jnp.tanh(0.7978845608028654 * (x + 0.044715 * (x * x * x))))


def _rms(x, g):
    y = x * lax.rsqrt(jnp.mean(x * x, axis=-1, keepdims=True) + EPS)
    return y * g


def _rmsnorm_kernel(x_ref, g_ref, o_ref):
    o_ref[...] = _rms(x_ref[...], g_ref[...]).astype(o_ref.dtype)


def rmsnorm(x, g, out_dtype=BF16, tm=512):
    T = x.shape[0]
    return pl.pallas_call(
        _rmsnorm_kernel,
        grid=(T // tm,),
        in_specs=[pl.BlockSpec((tm, D_MODEL), lambda i: (i, 0)),
                  pl.BlockSpec((1, D_MODEL), lambda i: (0, 0))],
        out_specs=pl.BlockSpec((tm, D_MODEL), lambda i: (i, 0)),
        out_shape=jax.ShapeDtypeStruct((T, D_MODEL), out_dtype),
        compiler_params=_params(("parallel",)),
    )(x, g.reshape(1, D_MODEL))


def _router_kernel(x_ref, g_ref, wh_ref, wl_ref, br_ref, e_ref, w_ref):
    y = _rms(x_ref[...], g_ref[...])
    y_hi = y.astype(BF16)
    y_lo = (y - y_hi.astype(F32)).astype(BF16)
    lg = (jnp.dot(y_hi, wh_ref[...], preferred_element_type=F32)
          + jnp.dot(y_hi, wl_ref[...], preferred_element_type=F32)
          + jnp.dot(y_lo, wh_ref[...], preferred_element_type=F32))
    lt = lg.T + br_ref[...]
    row = lambda i: lt[i:i + 1, :]

    def softmax(rows):
        m = functools.reduce(jnp.maximum, rows)
        e = [jnp.exp(r - m) for r in rows]
        s = functools.reduce(lambda a, b: a + b, e)
        return [v / s for v in e]

    def top1(vals, skip=None):
        best = jnp.full_like(vals[0], -1.0)
        idx = jnp.zeros(vals[0].shape, jnp.int32)
        for k, v in enumerate(vals):
            if skip is not None:
                v = jnp.where(skip == k, -1.0, v)
            upd = v > best
            best = jnp.where(upd, v, best)
            idx = jnp.where(upd, k, idx)
        return best, idx

    g_w, g_idx = top1(softmax([row(j) for j in range(N_GROUPS)]))
    e_logits = []
    for k in range(EXP_PER_GROUP):
        v = row(N_GROUPS + k)
        for gg in range(1, N_GROUPS):
            v = jnp.where(g_idx == gg, row(N_GROUPS + gg * EXP_PER_GROUP + k), v)
        e_logits.append(v)
    e_prob = softmax(e_logits)
    w1, i1 = top1(e_prob)
    w2, i2 = top1(e_prob, skip=i1)
    tot = w1 + w2
    zi = jnp.zeros((6, lt.shape[1]), jnp.int32)
    e_ref[...] = jnp.concatenate([g_idx * EXP_PER_GROUP + i1, g_idx * EXP_PER_GROUP + i2, zi], axis=0)
    w_ref[...] = jnp.concatenate([g_w * (w1 / tot), g_w * (w2 / tot), zi.astype(F32)], axis=0)


def router(x, g, wr, br, tm=512):
    T = x.shape[0]
    wr_hi = wr.astype(BF16)
    wr_lo = (wr - wr_hi.astype(F32)).astype(BF16)
    return pl.pallas_call(
        _router_kernel,
        grid=(T // tm,),
        in_specs=[pl.BlockSpec((tm, D_MODEL), lambda i: (i, 0)),
                  pl.BlockSpec((1, D_MODEL), lambda i: (0, 0)),
                  pl.BlockSpec((D_MODEL, LANE), lambda i: (0, 0)),
                  pl.BlockSpec((D_MODEL, LANE), lambda i: (0, 0)),
                  pl.BlockSpec((LANE, 1), lambda i: (0, 0))],
        out_specs=[pl.BlockSpec((8, tm), lambda i: (0, i)),
                   pl.BlockSpec((8, tm), lambda i: (0, i))],
        out_shape=[jax.ShapeDtypeStruct((8, T), jnp.int32),
                   jax.ShapeDtypeStruct((8, T), F32)],
        compiler_params=_params(("parallel",)),
    )(x, g.reshape(1, D_MODEL), wr_hi, wr_lo, br)


def _mm_kernel(*refs, act, scale, has_resid, layout):
    a_ref, w_ref = refs[0], refs[1]
    o_ref = refs[-1]
    acc = jnp.dot(a_ref[...], w_ref[...], preferred_element_type=F32)
    if scale is not None:
        acc = acc * scale
    if act == "gelu":
        acc = _gelu(acc)
    elif act == "sigmoid":
        acc = jax.nn.sigmoid(acc)
    if has_resid:
        acc = acc + refs[2][...]
    tm, tn = acc.shape
    if layout == "plain":
        o_ref[...] = acc.astype(o_ref.dtype)
    elif layout == "heads":
        for h in range(tn // HEAD_DIM):
            o_ref[0, h] = acc[:, h * HEAD_DIM:(h + 1) * HEAD_DIM].astype(o_ref.dtype)
    else:
        for h in range(tn // HEAD_DIM):
            for c in range(tm // LANE):
                blk = acc[c * LANE:(c + 1) * LANE, h * HEAD_DIM:(h + 1) * HEAD_DIM]
                o_ref[0, h, c] = blk.T.astype(o_ref.dtype)


def matmul(a, w, *, out_dtype, act=None, scale=None, resid=None, layout="plain",
           seq=None, tm=512, tn=1024):
    M, K = a.shape
    N = w.shape[1]
    tn = min(tn, N)
    grid = (N // tn, M // tm)
    in_specs = [pl.BlockSpec((tm, K), lambda j, i: (i, 0)),
                pl.BlockSpec((K, tn), lambda j, i: (0, j))]
    args = [a, w]
    if resid is not None:
        in_specs.append(pl.BlockSpec((tm, tn), lambda j, i: (i, j)))
        args.append(resid)
    if layout == "plain":
        out_shape = jax.ShapeDtypeStruct((M, N), out_dtype)
        out_spec = pl.BlockSpec((tm, tn), lambda j, i: (i, j))
    else:
        nb = seq // tm
        nh = tn // HEAD_DIM
        if layout == "heads":
            out_shape = jax.ShapeDtypeStruct((M // seq, N // HEAD_DIM, seq, HEAD_DIM), out_dtype)
            out_spec = pl.BlockSpec((1, nh, tm, HEAD_DIM), lambda j, i: (i // nb, j, i % nb, 0))
        else:
            out_shape = jax.ShapeDtypeStruct(
                (M // seq, N // HEAD_DIM, seq // LANE, HEAD_DIM, LANE), out_dtype)
            out_spec = pl.BlockSpec((1, nh, tm // LANE, HEAD_DIM, LANE),
                                    lambda j, i: (i // nb, j, i % nb, 0, 0))
    kern = functools.partial(_mm_kernel, act=act, scale=scale, has_resid=resid is not None,
                             layout=layout)
    return pl.pallas_call(
        kern, grid=grid, in_specs=in_specs, out_specs=out_spec, out_shape=out_shape,
        compiler_params=_params(("parallel", "parallel")),
    )(*args)


def _gate_kernel(z_ref, lng_ref, lnb_ref, ws_ref, bs_ref, y_ref):
    tm = z_ref.shape[0]
    u = z_ref[:, :D_MODEL]
    v = z_ref[:, D_MODEL:]
    mu = jnp.mean(v, axis=-1, keepdims=True)
    vc = v - mu
    vn = vc * lax.rsqrt(jnp.mean(vc * vc, axis=-1, keepdims=True) + EPS)
    vn = (vn * lng_ref[...] + lnb_ref[...]).astype(BF16)
    gd = D_MODEL // GM_GROUPS
    for c in range(tm // GM_CHUNK):
        rows = slice(c * GM_CHUNK, (c + 1) * GM_CHUNK)
        for g in range(GM_GROUPS):
            cols = slice(g * gd, (g + 1) * gd)
            sv = jnp.dot(ws_ref[g], vn[rows, cols], preferred_element_type=F32)
            y_ref[rows, cols] = (u[rows, cols] * (sv + bs_ref[:, cols])).astype(y_ref.dtype)


def gmlp_gate(z, ln_g, ln_b, ws_masked, bs_full, tm=256):
    T = z.shape[0]
    return pl.pallas_call(
        _gate_kernel,
        grid=(T // tm,),
        in_specs=[pl.BlockSpec((tm, 2 * D_MODEL), lambda i: (i, 0)),
                  pl.BlockSpec((1, D_MODEL), lambda i: (0, 0)),
                  pl.BlockSpec((1, D_MODEL), lambda i: (0, 0)),
                  pl.BlockSpec((GM_GROUPS, GM_CHUNK, GM_CHUNK), lambda i: (0, 0, 0)),
                  pl.BlockSpec((GM_CHUNK, D_MODEL), lambda i: (0, 0))],
        out_specs=pl.BlockSpec((tm, D_MODEL), lambda i: (i, 0)),
        out_shape=jax.ShapeDtypeStruct((T, D_MODEL), BF16),
        compiler_params=_params(("parallel",)),
    )(z, ln_g.reshape(1, D_MODEL), ln_b.reshape(1, D_MODEL), ws_masked, bs_full)


def _compress_kernel(x_ref, pe_ref, w1_ref, w2_ref, o_ref, ot_ref):
    x = x_ref[0]
    half = CMP_STRIDE * HEAD_DIM
    xa = (x + pe_ref[0, 0:1, :]).astype(BF16)
    xb = (x + pe_ref[0, 1:2, :]).astype(BF16)
    a = jnp.dot(xa, w1_ref[0, :half, :], preferred_element_type=F32)
    b = jnp.dot(xb, w1_ref[0, half:, :], preferred_element_type=F32)
    n = a.shape[0]
    pre = a + pltpu.roll(b, n - 1, axis=0)
    out = jnp.dot(_gelu(pre).astype(BF16), w2_ref[0], preferred_element_type=F32)
    o_ref[0] = out.astype(o_ref.dtype)
    ot_ref[0] = out.T.astype(ot_ref.dtype)


def compress_kv(kvc, pe2, w1, w2):
    B, H2, S, _ = kvc.shape
    n_grp = S // CMP_STRIDE
    x = kvc.reshape(B * H2, n_grp, CMP_STRIDE * HEAD_DIM)
    sel = lambda i: ((i % H2) // N_KV_HEADS, 0, 0)
    return pl.pallas_call(
        _compress_kernel,
        grid=(B * H2,),
        in_specs=[pl.BlockSpec((1, n_grp, CMP_STRIDE * HEAD_DIM), lambda i: (i, 0, 0)),
                  pl.BlockSpec((1, 2, CMP_STRIDE * HEAD_DIM), sel),
                  pl.BlockSpec((1, CMP_LEN * HEAD_DIM, HEAD_DIM), sel),
                  pl.BlockSpec((1, HEAD_DIM, HEAD_DIM), sel)],
        out_specs=[pl.BlockSpec((1, n_grp, HEAD_DIM), lambda i: (i, 0, 0)),
                   pl.BlockSpec((1, HEAD_DIM, n_grp), lambda i: (i, 0, 0))],
        out_shape=[jax.ShapeDtypeStruct((B * H2, n_grp, HEAD_DIM), BF16),
                   jax.ShapeDtypeStruct((B * H2, HEAD_DIM, n_grp), BF16)],
        compiler_params=_params(("parallel",)),
    )(x, pe2, w1, w2)


def _nsa_kernel(q_ref, kc_ref, vct_ref, ksa_ref, vst_ref, kw_ref, vwt_ref, gt_ref, ovt_ref, wm_ref,
                o_ref, qaug_ref, s_a, s_b):
    nq = GQA * Q_BLK
    qb = pl.program_id(2)
    q0 = qb * Q_BLK
    q_t = q_ref[0].reshape(nq, HEAD_DIM).astype(F32).T.astype(BF16)
    t_lane = q0 + (lax.broadcasted_iota(jnp.int32, (1, nq), 1) & (Q_BLK - 1))

    n_c = kc_ref.shape[2]

    def compressed(n):
        s = jnp.dot(kc_ref[0, 0, :n, :], q_t, preferred_element_type=F32)
        c_end = lax.broadcasted_iota(jnp.int32, (n, 1), 0) * CMP_STRIDE + (CMP_LEN - 1)
        s = jnp.where(c_end <= t_lane, s, NEG)
        m = jnp.max(s, axis=0, keepdims=True)
        p = jnp.exp2(s - m)
        l = jnp.sum(p, axis=0, keepdims=True)
        p = p * jnp.where(m > 0.5 * NEG, 1.0 / l, 0.0)
        o = jnp.dot(vct_ref[0, 0, :, :n], p.astype(BF16), preferred_element_type=F32)
        psum = p[:, 0:Q_BLK]
        for h in range(1, GQA):
            psum = psum + p[:, h * Q_BLK:(h + 1) * Q_BLK]
        p_hi = psum.astype(BF16)
        p_lo = (psum - p_hi.astype(F32)).astype(BF16)
        ov = ovt_ref[:, :n]
        return o, (jnp.dot(ov, p_hi, preferred_element_type=F32)
                   + jnp.dot(ov, p_lo, preferred_element_type=F32))

    n_bkt = 4 if n_c % (4 * LANE) == 0 else 1
    bkt = n_c // n_bkt
    if n_bkt == 1:
        o_cmp, imp = compressed(n_c)
    else:
        o_cmp, imp = lax.switch((q0 + Q_BLK - 1) // (bkt * CMP_STRIDE),
                                [functools.partial(compressed, bkt * (i + 1)) for i in range(n_bkt)])

    n_sel = imp.shape[0]
    blk = lax.broadcasted_iota(jnp.int32, (n_sel, Q_BLK), 0)
    t_q = q0 + lax.broadcasted_iota(jnp.int32, (n_sel, Q_BLK), 1)
    dist = (t_q // SEL_BLK) - blk
    forced = (blk == 0) | ((dist >= 0) & (dist < SEL_LOCAL))
    score = jnp.where(forced, -jnp.inf, jnp.where(blk * SEL_BLK <= t_q, imp, NEG))
    blk_f = blk.astype(F32)
    for _ in range(SEL_TOPN - 1 - SEL_LOCAL):
        top = jnp.max(score, axis=0, keepdims=True)
        first = jnp.min(jnp.where(score == top, blk_f, float(n_sel)), axis=0, keepdims=True)
        score = jnp.where(blk_f == first, -jnp.inf, score)
    bias = jnp.where((score == -jnp.inf) & (blk * SEL_BLK < q0), 0.0, NEG)

    n_half = qaug_ref.shape[0]
    if n_sel < LANE:
        bias = jnp.concatenate([bias, jnp.zeros((LANE - n_sel, Q_BLK), F32)], axis=0)
    for hf in range(n_half):
        bq = bias[hf * LANE:(hf + 1) * LANE, :].astype(BF16)
        qaug_ref[hf] = jnp.concatenate([q_t, jnp.concatenate([bq] * GQA, axis=1)], axis=0)

    wk = WINDOW + Q_BLK
    kstart = pl.multiple_of(jnp.maximum(q0 - WINDOW, 0), LANE)
    sw = jnp.dot(kw_ref[0, 0, pl.ds(kstart, wk), :], q_t, preferred_element_type=F32)
    sw = sw + jnp.concatenate([wm_ref[0]] * GQA, axis=1)
    m_w = jnp.max(sw, axis=0, keepdims=True)
    p_w = jnp.exp2((sw - m_w).astype(BF16))
    vw = jnp.concatenate([vwt_ref[0, 0, kstart // LANE + j] for j in range(wk // LANE)], axis=1)
    vw = jnp.concatenate([vw, jnp.ones((ONES_ROWS, wk), BF16)], axis=0)
    acc_w = jnp.dot(vw, p_w, preferred_element_type=F32)
    o_win = acc_w[:HEAD_DIM] * (1.0 / acc_w[HEAD_DIM:HEAD_DIM + 1])

    kpos = q0 + lax.broadcasted_iota(jnp.int32, (Q_BLK, 1), 0)
    s_d = jnp.dot(ksa_ref[0, 0, pl.ds(pl.multiple_of(q0, Q_BLK), Q_BLK), :HEAD_DIM], q_t,
                  preferred_element_type=F32)
    s_d = jnp.where(kpos <= t_lane, s_d, NEG)
    m_d = jnp.max(s_d, axis=0, keepdims=True)
    p_d = jnp.exp2(s_d - m_d)
    v_d = jnp.concatenate([vst_ref[0, 0, q0 // LANE + j] for j in range(Q_BLK // LANE)], axis=1)
    init = (m_d, jnp.sum(p_d, axis=0, keepdims=True),
            jnp.dot(v_d, p_d.astype(BF16), preferred_element_type=F32))

    bpt = SEL_TILE // SEL_BLK
    vpt = SEL_TILE // LANE

    def scores(kt):
        k0 = pl.multiple_of(kt * SEL_TILE, SEL_TILE)
        return jnp.dot(ksa_ref[0, 0, pl.ds(k0, SEL_TILE), :], qaug_ref[(kt * bpt) // LANE],
                       preferred_element_type=F32)

    def softmax_pv(kt, st, carry):
        m_i, l_i, acc = carry
        m_new = jnp.maximum(m_i, jnp.max(st, axis=0, keepdims=True))
        alpha = jnp.exp2(m_i - m_new)
        pt = jnp.exp2(st - m_new)
        l_new = alpha * l_i + jnp.sum(pt, axis=0, keepdims=True)
        vt = jnp.concatenate([vst_ref[0, 0, kt * vpt + j] for j in range(vpt)], axis=1)
        acc = alpha * acc + jnp.dot(vt, pt.astype(BF16), preferred_element_type=F32)
        return m_new, l_new, acc

    def trip(j, carry):
        st = s_a[...]
        s_b[...] = scores(2 * j + 1)
        carry = softmax_pv(2 * j, st, carry)
        st = s_b[...]
        s_a[...] = scores(2 * j + 2)
        return softmax_pv(2 * j + 1, st, carry)

    n_tiles = (q0 + SEL_TILE - 1) // SEL_TILE
    n_trips = jnp.maximum((n_tiles + 1) // 2, 1)
    s_a[...] = scores(0)
    carry = lax.fori_loop(0, n_trips - 1, trip, init)
    last = 2 * n_trips - 1
    st = s_a[...]
    s_b[...] = scores(last)
    carry = softmax_pv(last - 1, st, carry)
    _, l_s, acc_s = softmax_pv(last, s_b[...], carry)
    o_sel = acc_s * (1.0 / l_s)

    for h in range(GQA):
        cols = slice(h * Q_BLK, (h + 1) * Q_BLK)
        o_t = (gt_ref[0, 0, 0, h:h + 1, :] * o_cmp[:, cols]
               + gt_ref[0, 0, 1, h:h + 1, :] * o_sel[:, cols]
               + gt_ref[0, 0, 2, h:h + 1, :] * o_win[:, cols])
        o_ref[0, :, h * HEAD_DIM:(h + 1) * HEAD_DIM] = o_t.T.astype(o_ref.dtype)


def _window_mask(S):
    wk = WINDOW + Q_BLK
    i = np.arange(wk)[:, None]
    qi = np.arange(Q_BLK)[None, :]
    out = []
    for v in range(WINDOW // Q_BLK + 1):
        t = Q_BLK * v + qi if v < WINDOW // Q_BLK else WINDOW + qi
        out.append(np.where((i <= t) & (i > t - WINDOW), 0.0, NEG))
    return jnp.asarray(np.stack(out), dtype=F32)


def nsa_attention_core(q, kc, vct, ks_aug, kw, vswt, gates_t, ov_t):
    B, _, S, _ = q.shape
    n_c = S // CMP_STRIDE
    n_sel = S // SEL_BLK
    nq = GQA * Q_BLK
    hk = N_KV_HEADS
    wk = WINDOW + Q_BLK
    nv = WINDOW // Q_BLK
    once = pl.Buffered(1)
    return pl.pallas_call(
        _nsa_kernel,
        grid=(B, hk, S // Q_BLK),
        in_specs=[
            pl.BlockSpec((1, GQA, Q_BLK, HEAD_DIM), lambda b, h, i: (b, h, i, 0)),
            pl.BlockSpec((1, 1, n_c, HEAD_DIM), lambda b, h, i: (b, h, 0, 0)),
            pl.BlockSpec((1, 1, HEAD_DIM, n_c), lambda b, h, i: (b, hk + h, 0, 0)),
            pl.BlockSpec((1, 1, S, 2 * HEAD_DIM), lambda b, h, i: (b, h, 0, 0), pipeline_mode=once),
            pl.BlockSpec((1, 1, S // LANE, HEAD_DIM, LANE), lambda b, h, i: (b, h, 0, 0, 0),
                         pipeline_mode=once),
            pl.BlockSpec((1, 1, S, HEAD_DIM), lambda b, h, i: (b, hk + h, 0, 0), pipeline_mode=once),
            pl.BlockSpec((1, 1, S // LANE, HEAD_DIM, LANE), lambda b, h, i: (b, hk + h, 0, 0, 0),
                         pipeline_mode=once),
            pl.BlockSpec((1, 1, 3, GQA, Q_BLK), lambda b, h, i: (b, h, 0, 0, i)),
            pl.BlockSpec((n_sel, n_c), lambda b, h, i: (0, 0)),
            pl.BlockSpec((1, wk, Q_BLK), lambda b, h, i: (jnp.minimum(i, nv), 0, 0)),
        ],
        out_specs=pl.BlockSpec((1, Q_BLK, GQA * HEAD_DIM), lambda b, h, i: (b, i, h)),
        out_shape=jax.ShapeDtypeStruct((B, S, N_HEADS * HEAD_DIM), BF16),
        scratch_shapes=[pltpu.VMEM((max(1, n_sel // LANE), 2 * HEAD_DIM, nq), BF16),
                        pltpu.VMEM((SEL_TILE, nq), F32), pltpu.VMEM((SEL_TILE, nq), F32)],
        compiler_params=_params(("parallel", "parallel", "arbitrary")),
    )(q, kc, vct, ks_aug, vswt, kw, vswt, gates_t, ov_t, _window_mask(S))


def _moe_kernel(nused_ref, bexp_ref, rtok_ref, h_hbm, g_ref, wg_ref, wu_ref, wd_ref, y_ref,
                x_even, x_odd, wg_b, wu_b, wd_b, sem):
    i = pl.program_id(0)
    n_used = nused_ref[0]
    blk = x_even.shape[0]
    bufs = (x_even, x_odd)

    def row_copy(b, r, slot):
        tok = rtok_ref[b * blk + r]
        return pltpu.make_async_copy(h_hbm.at[pl.ds(tok, 1)], bufs[slot].at[pl.ds(r, 1)],
                                     sem.at[slot])

    def wait_block(slot):
        pltpu.make_async_copy(h_hbm.at[pl.ds(0, blk)], bufs[slot], sem.at[slot]).wait()

    @pl.when(i == 0)
    def _():
        def body(r, c):
            row_copy(0, r, 0).start()
            return c
        lax.fori_loop(0, blk, body, 0, unroll=8)

    @pl.when((i < n_used) & ((i == 0) | (bexp_ref[i] != bexp_ref[jnp.maximum(i - 1, 0)])))
    def _():
        wg_b[...] = wg_ref[0, 0].astype(BF16)
        wu_b[...] = wu_ref[0, 0].astype(BF16)
        wd_b[...] = wd_ref[0, 0].astype(BF16)

    for slot in range(2):
        @pl.when((i < n_used) & (i % 2 == slot))
        def _(slot=slot):
            wait_block(slot)
            for r in range(blk):
                row_copy(i + 1, r, 1 - slot).start()
            x = _rms(bufs[slot][...], g_ref[...]).astype(BF16)
            gate = jnp.dot(x, wg_b[...], preferred_element_type=F32)
            up = jnp.dot(x, wu_b[...], preferred_element_type=F32)
            hid = (gate * jax.nn.sigmoid(gate) * up).astype(BF16)
            y_ref[...] = jnp.dot(hid, wd_b[...], preferred_element_type=F32)

        @pl.when((i == n_used) & (i % 2 == slot))
        def _(slot=slot):
            wait_block(slot)

    @pl.when(i >= n_used)
    def _():
        y_ref[...] = jnp.zeros_like(y_ref)


def moe_experts(h, g, n_used, block_exp, row_tok, wg, wu, wd, layer, blk=MOE_BLK):
    n_rows = row_tok.shape[0]
    n_steps = n_rows // blk
    grid_spec = pltpu.PrefetchScalarGridSpec(
        num_scalar_prefetch=3,
        grid=(n_steps,),
        in_specs=[
            pl.BlockSpec(memory_space=pl.ANY),
            pl.BlockSpec((1, D_MODEL), lambda i, nu, be, rt: (0, 0)),
            pl.BlockSpec((1, 1, D_MODEL, D_EXPERT), lambda i, nu, be, rt: (layer, be[i], 0, 0)),
            pl.BlockSpec((1, 1, D_MODEL, D_EXPERT), lambda i, nu, be, rt: (layer, be[i], 0, 0)),
            pl.BlockSpec((1, 1, D_EXPERT, D_MODEL), lambda i, nu, be, rt: (layer, be[i], 0, 0)),
        ],
        out_specs=pl.BlockSpec((blk, D_MODEL), lambda i, nu, be, rt: (i, 0)),
        scratch_shapes=[pltpu.VMEM((blk, D_MODEL), F32), pltpu.VMEM((blk, D_MODEL), F32),
                        pltpu.VMEM((D_MODEL, D_EXPERT), BF16), pltpu.VMEM((D_MODEL, D_EXPERT), BF16),
                        pltpu.VMEM((D_EXPERT, D_MODEL), BF16), pltpu.SemaphoreType.DMA((2,))],
    )
    return pl.pallas_call(
        _moe_kernel, grid_spec=grid_spec,
        out_shape=jax.ShapeDtypeStruct((n_rows, D_MODEL), F32),
        compiler_params=_params(("arbitrary",)),
    )(n_used, block_exp, row_tok, h, g.reshape(1, D_MODEL), wg, wu, wd)


def _combine_kernel(pos_ref, h_ref, w_ref, g_ref, y_hbm, o_ref, n_ref, ybuf, sem):
    i = pl.program_id(0)
    n = pl.num_programs(0)
    tm = h_ref.shape[0]

    def issue(b, slot):
        def body(j, c):
            for k in range(TOP_K):
                row = pos_ref[(b * tm + j) * TOP_K + k]
                pltpu.make_async_copy(y_hbm.at[pl.ds(row, 1)], ybuf.at[slot, pl.ds(k * tm + j, 1)],
                                      sem.at[slot]).start()
            return c
        lax.fori_loop(0, tm, body, 0, unroll=4)

    @pl.when(i == 0)
    def _():
        issue(0, 0)

    @pl.when(i + 1 < n)
    def _():
        issue(i + 1, (i + 1) % 2)

    slot = i % 2
    pltpu.make_async_copy(y_hbm.at[pl.ds(0, TOP_K * tm)], ybuf.at[slot], sem.at[slot]).wait()
    out = h_ref[...]
    for k in range(TOP_K):
        out = out + ybuf[slot, k * tm:(k + 1) * tm, :] * w_ref[:, k:k + 1]
    o_ref[...] = out
    n_ref[...] = _rms(out, g_ref[...]).astype(n_ref.dtype)


def moe_combine(h, y, pos, weight, g_next, norm_dtype, tm=256):
    T = h.shape[0]
    grid_spec = pltpu.PrefetchScalarGridSpec(
        num_scalar_prefetch=1,
        grid=(T // tm,),
        in_specs=[
            pl.BlockSpec((tm, D_MODEL), lambda i, p: (i, 0)),
            pl.BlockSpec((tm, TOP_K), lambda i, p: (i, 0)),
            pl.BlockSpec((1, D_MODEL), lambda i, p: (0, 0)),
            pl.BlockSpec(memory_space=pl.ANY),
        ],
        out_specs=[pl.BlockSpec((tm, D_MODEL), lambda i, p: (i, 0)),
                   pl.BlockSpec((tm, D_MODEL), lambda i, p: (i, 0))],
        scratch_shapes=[pltpu.VMEM((2, TOP_K * tm, D_MODEL), F32), pltpu.SemaphoreType.DMA((2,))],
    )
    return pl.pallas_call(
        _combine_kernel, grid_spec=grid_spec,
        out_shape=[jax.ShapeDtypeStruct((T, D_MODEL), F32),
                   jax.ShapeDtypeStruct((T, D_MODEL), norm_dtype)],
        compiler_params=_params(("arbitrary",)),
    )(pos.reshape(-1), h, weight, g_next.reshape(1, D_MODEL), y)


def _dispatch(expert, T, blk):
    onehot = (expert[:, None] == jnp.arange(N_EXPERTS, dtype=jnp.int32)[None, :]).astype(jnp.int32)
    csum = jnp.cumsum(onehot, axis=0)
    rank = jnp.sum(csum * onehot, axis=1) - 1
    counts = csum[-1]
    padded = (counts + blk - 1) // blk * blk
    pad_end = jnp.cumsum(padded)
    pad_start = pad_end - padded
    pos = (jnp.sum(pad_start[None, :] * onehot, axis=1) + rank).astype(jnp.int32)
    n_rows = T * TOP_K + (N_EXPERTS + 1) * blk
    token = jnp.repeat(jnp.arange(T, dtype=jnp.int32), TOP_K)
    row_tok = jnp.zeros((n_rows,), jnp.int32).at[pos].set(token, unique_indices=True)
    n_blocks = n_rows // blk
    starts = jnp.arange(n_blocks, dtype=jnp.int32) * blk
    block_exp = jnp.minimum(jnp.sum((pad_end[None, :] <= starts[:, None]).astype(jnp.int32), axis=1),
                            N_EXPERTS - 1).astype(jnp.int32)
    n_used = (pad_end[-1] // blk).astype(jnp.int32).reshape(1)
    return pos.reshape(T, TOP_K), row_tok, block_exp, n_used


def _moe_layer(h, norm_g, w_group, b_group, w_exp, b_exp, wg, wu, wd, layer, g_next, norm_dtype):
    T = h.shape[0]
    wr = jnp.zeros((D_MODEL, LANE), F32)
    wr = wr.at[:, :N_GROUPS].set(w_group).at[:, N_GROUPS:N_GROUPS + N_EXPERTS].set(w_exp)
    br = jnp.zeros((LANE,), F32).at[:N_GROUPS].set(b_group).at[N_GROUPS:N_GROUPS + N_EXPERTS].set(b_exp)
    e_t, w_t = router(h, norm_g, wr, br.reshape(LANE, 1))
    expert = e_t[:TOP_K].T.reshape(-1)
    weight = w_t[:TOP_K].T
    pos, row_tok, block_exp, n_used = _dispatch(expert, T, MOE_BLK)
    y = moe_experts(h, norm_g, n_used, block_exp, row_tok, wg, wu, wd, layer)
    return moe_combine(h, y, pos, weight, g_next, norm_dtype)


def _gmlp_layer(h, hn, w_in, ln_g, ln_b, w_s, b_s, w_out):
    z = matmul(hn, w_in.astype(BF16), out_dtype=F32, act="gelu")
    causal = jnp.tril(jnp.ones((GM_CHUNK, GM_CHUNK), w_s.dtype))
    bs_full = jnp.repeat(b_s.T, D_MODEL // GM_GROUPS, axis=1)
    y = gmlp_gate(z, ln_g, ln_b, (w_s * causal).astype(BF16), bs_full)
    return matmul(y, w_out.astype(BF16), out_dtype=F32, resid=h)


def _overlap_t(S):
    n_c = S // CMP_STRIDE
    n_sel = S // SEL_BLK
    ci = np.arange(n_c)[None, :] * CMP_STRIDE
    sj = np.arange(n_sel)[:, None] * SEL_BLK
    ov = (ci < sj + SEL_BLK) & (ci + CMP_LEN > sj) & (np.arange(n_c)[None, :] < n_c - 1)
    return jnp.asarray(ov.astype(np.float32), dtype=BF16)


def _nsa_layer(h, hn, B, S, w_in, ck_pe, ck_w1, ck_w2, cv_pe, cv_w1, cv_w2, w_out):
    qd = N_HEADS * HEAD_DIM
    kvd = N_KV_HEADS * HEAD_DIM
    w = w_in.astype(BF16)
    o = [qd + i * kvd for i in range(7)]
    w_q, w_kc, w_vc, w_ks, w_vs, w_kw, w_vw = (
        w[:, :o[0]], w[:, o[0]:o[1]], w[:, o[1]:o[2]], w[:, o[2]:o[3]], w[:, o[3]:o[4]],
        w[:, o[4]:o[5]], w[:, o[5]:o[6]])
    w_g = jnp.zeros((D_MODEL, LANE), BF16).at[:, :3 * N_HEADS].set(w[:, o[6]:])
    q = matmul(hn, w_q, out_dtype=BF16, scale=HEAD_DIM ** -0.5 * LOG2E, layout="heads", seq=S)
    kvc = matmul(hn, jnp.concatenate([w_kc, w_vc], 1), out_dtype=F32, layout="heads", seq=S)
    ksw = matmul(hn, jnp.concatenate([w_ks, w_kw], 1), out_dtype=BF16, layout="heads", seq=S)
    vswt = matmul(hn, jnp.concatenate([w_vs, w_vw], 1), out_dtype=BF16, layout="heads_t", seq=S)
    gates = matmul(hn, w_g, out_dtype=F32, act="sigmoid")[:, :3 * N_HEADS]
    gates_t = gates.reshape(B, S, N_KV_HEADS, GQA, 3).transpose(0, 2, 4, 3, 1)

    half = CMP_STRIDE
    pe2 = jnp.stack([jnp.stack([pe[:half].reshape(-1), pe[half:].reshape(-1)])
                     for pe in (ck_pe, cv_pe)])
    w1 = jnp.stack([ck_w1, cv_w1]).astype(BF16)
    w2 = jnp.stack([ck_w2, cv_w2]).astype(BF16)
    cmp, cmp_t = compress_kv(kvc, pe2, w1, w2)
    n_c = S // CMP_STRIDE
    cmp = cmp.reshape(B, 2 * N_KV_HEADS, n_c, HEAD_DIM)
    cmp_t = cmp_t.reshape(B, 2 * N_KV_HEADS, HEAD_DIM, n_c)
    blk_id = (np.arange(S) // SEL_BLK) % LANE
    onehot = jnp.asarray(blk_id[:, None] == np.arange(LANE)[None, :], dtype=BF16)
    ks_aug = jnp.concatenate(
        [ksw[:, :N_KV_HEADS], jnp.broadcast_to(onehot, (B, N_KV_HEADS, S, LANE))], axis=-1)
    o_att = nsa_attention_core(q, cmp, cmp_t, ks_aug, ksw, vswt, gates_t, _overlap_t(S))
    return matmul(o_att.reshape(B * S, qd), w_out.astype(BF16), out_dtype=F32, resid=h)


def kernel(x, norm_mix, norm_ffn, norm_final, a_w_in, a_ln_g, a_ln_b, a_w_s, a_b_s, a_w_out,
           b_w_in, b_ck_pe, b_ck_w1, b_ck_w2, b_cv_pe, b_cv_w1, b_cv_w2, b_w_out,
           r_w_group, r_b_group, r_w_exp, r_b_exp, e_w_gate, e_w_up, e_w_down):
    B, S, _ = x.shape
    h = x.reshape(B * S, D_MODEL)
    hn = rmsnorm(h, norm_mix[0])
    h = _gmlp_layer(h, hn, a_w_in[0], a_ln_g[0], a_ln_b[0], a_w_s[0], a_b_s[0], a_w_out[0])
    h, hn = _moe_layer(h, norm_ffn[0], r_w_group[0], r_b_group[0], r_w_exp[0], r_b_exp[0],
                       e_w_gate, e_w_up, e_w_down, 0, norm_mix[1], BF16)
    h = _nsa_layer(h, hn, B, S, b_w_in[0], b_ck_pe[0], b_ck_w1[0], b_ck_w2[0],
                   b_cv_pe[0], b_cv_w1[0], b_cv_w2[0], b_w_out[0])
    _, out = _moe_layer(h, norm_ffn[1], r_w_group[1], r_b_group[1], r_w_exp[1], r_b_exp[1],
                        e_w_gate, e_w_up, e_w_down, 1, norm_final, F32)
    return out.reshape(B, S, D_MODEL)
```

```python
import functools

import numpy as np
import jax
import jax.numpy as jnp
from jax import lax
from jax.experimental import pallas as pl
from jax.experimental.pallas import tpu as pltpu

F32 = jnp.float32
BF16 = jnp.bfloat16

D_MODEL = 2048
LANE = 128
GM_GROUPS = 8
GM_CHUNK = 128
HEAD_DIM = 128
N_HEADS = 16
N_KV_HEADS = 4
GQA = N_HEADS // N_KV_HEADS
CMP_LEN = 32
CMP_STRIDE = 16
SEL_BLK = 64
SEL_TOPN = 16
SEL_LOCAL = 2
WINDOW = 512
Q_BLK = 128
FORCE = 1.0e4
N_GROUPS = 4
EXP_PER_GROUP = 8
N_EXPERTS = N_GROUPS * EXP_PER_GROUP
TOP_K = 2
D_EXPERT = 512
EPS = 1e-6
NEG = -1e30
LOG2E = 1.4426950408889634
ONES_ROWS = 16
SEL_TILE = 512
MOE_BLK = 512
VMEM_LIMIT = 56 * 1024 * 1024

_NT = (((1,), (1,)), ((), ()))


def _params(sem):
    return pltpu.CompilerParams(dimension_semantics=sem, vmem_limit_bytes=VMEM_LIMIT)


def _gelu(x):
    return 0.5 * x * (1.0 + jnp.tanh(0.7978845608028654 * (x + 0.044715 * (x * x * x))))


def _rms(x, g):
    y = x * lax.rsqrt(jnp.mean(x * x, axis=-1, keepdims=True) + EPS)
    return y * g


def _pack_rows(x):
    half = x.shape[1] // 2
    bits = lambda v: lax.bitcast_convert_type(v.astype(BF16).astype(F32), jnp.uint32)
    return (bits(x[:, :half]) >> 16) | (bits(x[:, half:]) & jnp.uint32(0xFFFF0000))


def _unpack_rows(w):
    lo = lax.bitcast_convert_type(w << 16, F32)
    hi = lax.bitcast_convert_type(w & jnp.uint32(0xFFFF0000), F32)
    return lo, hi


def _rmsnorm_kernel(x_ref, g_ref, o_ref):
    o_ref[...] = _rms(x_ref[...], g_ref[...]).astype(o_ref.dtype)


def rmsnorm(x, g, out_dtype=BF16, tm=512):
    T = x.shape[0]
    return pl.pallas_call(
        _rmsnorm_kernel,
        grid=(T // tm,),
        in_specs=[pl.BlockSpec((tm, D_MODEL), lambda i: (i, 0)),
                  pl.BlockSpec((1, D_MODEL), lambda i: (0, 0))],
        out_specs=pl.BlockSpec((tm, D_MODEL), lambda i: (i, 0)),
        out_shape=jax.ShapeDtypeStruct((T, D_MODEL), out_dtype),
        compiler_params=_params(("parallel",)),
    )(x, g.reshape(1, D_MODEL))


def _router_kernel(x_ref, g_ref, wh_ref, wl_ref, br_ref, e_ref, w_ref, xn_ref):
    y = _rms(x_ref[...], g_ref[...])
    xn_ref[...] = _pack_rows(y)
    y_hi = y.astype(BF16)
    y_lo = (y - y_hi.astype(F32)).astype(BF16)
    lg = (jnp.dot(y_hi, wh_ref[...], preferred_element_type=F32)
          + jnp.dot(y_hi, wl_ref[...], preferred_element_type=F32)
          + jnp.dot(y_lo, wh_ref[...], preferred_element_type=F32))
    lt = lg.T + br_ref[...]
    row = lambda i: lt[i:i + 1, :]

    def softmax(rows):
        m = functools.reduce(jnp.maximum, rows)
        e = [jnp.exp(r - m) for r in rows]
        s = functools.reduce(lambda a, b: a + b, e)
        return [v / s for v in e]

    def top1(vals, skip=None):
        best = jnp.full_like(vals[0], -1.0)
        idx = jnp.zeros(vals[0].shape, jnp.int32)
        for k, v in enumerate(vals):
            if skip is not None:
                v = jnp.where(skip == k, -1.0, v)
            upd = v > best
            best = jnp.where(upd, v, best)
            idx = jnp.where(upd, k, idx)
        return best, idx

    g_w, g_idx = top1(softmax([row(j) for j in range(N_GROUPS)]))
    e_logits = []
    for k in range(EXP_PER_GROUP):
        v = row(N_GROUPS + k)
        for gg in range(1, N_GROUPS):
            v = jnp.where(g_idx == gg, row(N_GROUPS + gg * EXP_PER_GROUP + k), v)
        e_logits.append(v)
    e_prob = softmax(e_logits)
    w1, i1 = top1(e_prob)
    w2, i2 = top1(e_prob, skip=i1)
    tot = w1 + w2
    zi = jnp.zeros((6, lt.shape[1]), jnp.int32)
    e_ref[...] = jnp.concatenate([g_idx * EXP_PER_GROUP + i1, g_idx * EXP_PER_GROUP + i2, zi], axis=0)
    w_ref[...] = jnp.concatenate([g_w * (w1 / tot), g_w * (w2 / tot), zi.astype(F32)], axis=0)


def router(x, g, wr, br, tm=512):
    T = x.shape[0]
    wr_hi = wr.astype(BF16)
    wr_lo = (wr - wr_hi.astype(F32)).astype(BF16)
    return pl.pallas_call(
        _router_kernel,
        grid=(T // tm,),
        in_specs=[pl.BlockSpec((tm, D_MODEL), lambda i: (i, 0)),
                  pl.BlockSpec((1, D_MODEL), lambda i: (0, 0)),
                  pl.BlockSpec((D_MODEL, LANE), lambda i: (0, 0)),
                  pl.BlockSpec((D_MODEL, LANE), lambda i: (0, 0)),
                  pl.BlockSpec((LANE, 1), lambda i: (0, 0))],
        out_specs=[pl.BlockSpec((8, tm), lambda i: (0, i)),
                   pl.BlockSpec((8, tm), lambda i: (0, i)),
                   pl.BlockSpec((tm, D_MODEL // 2), lambda i: (i, 0))],
        out_shape=[jax.ShapeDtypeStruct((8, T), jnp.int32),
                   jax.ShapeDtypeStruct((8, T), F32),
                   jax.ShapeDtypeStruct((T, D_MODEL // 2), jnp.uint32)],
        compiler_params=_params(("parallel",)),
    )(x, g.reshape(1, D_MODEL), wr_hi, wr_lo, br)


def _mm_kernel(*refs, act, scale, has_resid, layout):
    a_ref, w_ref = refs[0], refs[1]
    o_ref = refs[-1]
    acc = jnp.dot(a_ref[...], w_ref[...], preferred_element_type=F32)
    if scale is not None:
        acc = acc * scale
    if act == "gelu":
        acc = _gelu(acc)
    elif act == "sigmoid":
        acc = jax.nn.sigmoid(acc)
    if has_resid:
        acc = acc + refs[2][...]
    tm, tn = acc.shape
    if layout == "plain":
        o_ref[...] = acc.astype(o_ref.dtype)
    elif layout == "heads":
        for h in range(tn // HEAD_DIM):
            o_ref[0, h] = acc[:, h * HEAD_DIM:(h + 1) * HEAD_DIM].astype(o_ref.dtype)
    else:
        for h in range(tn // HEAD_DIM):
            for c in range(tm // LANE):
                blk = acc[c * LANE:(c + 1) * LANE, h * HEAD_DIM:(h + 1) * HEAD_DIM]
                o_ref[0, h, c] = blk.T.astype(o_ref.dtype)


def matmul(a, w, *, out_dtype, act=None, scale=None, resid=None, layout="plain",
           seq=None, tm=512, tn=1024):
    M, K = a.shape
    N = w.shape[1]
    tn = min(tn, N)
    grid = (N // tn, M // tm)
    in_specs = [pl.BlockSpec((tm, K), lambda j, i: (i, 0)),
                pl.BlockSpec((K, tn), lambda j, i: (0, j))]
    args = [a, w]
    if resid is not None:
        in_specs.append(pl.BlockSpec((tm, tn), lambda j, i: (i, j)))
        args.append(resid)
    if layout == "plain":
        out_shape = jax.ShapeDtypeStruct((M, N), out_dtype)
        out_spec = pl.BlockSpec((tm, tn), lambda j, i: (i, j))
    else:
        nb = seq // tm
        nh = tn // HEAD_DIM
        if layout == "heads":
            out_shape = jax.ShapeDtypeStruct((M // seq, N // HEAD_DIM, seq, HEAD_DIM), out_dtype)
            out_spec = pl.BlockSpec((1, nh, tm, HEAD_DIM), lambda j, i: (i // nb, j, i % nb, 0))
        else:
            out_shape = jax.ShapeDtypeStruct(
                (M // seq, N // HEAD_DIM, seq // LANE, HEAD_DIM, LANE), out_dtype)
            out_spec = pl.BlockSpec((1, nh, tm // LANE, HEAD_DIM, LANE),
                                    lambda j, i: (i // nb, j, i % nb, 0, 0))
    kern = functools.partial(_mm_kernel, act=act, scale=scale, has_resid=resid is not None,
                             layout=layout)
    return pl.pallas_call(
        kern, grid=grid, in_specs=in_specs, out_specs=out_spec, out_shape=out_shape,
        compiler_params=_params(("parallel", "parallel")),
    )(*args)


def _gate_kernel(z_ref, lng_ref, lnb_ref, ws_ref, bs_ref, y_ref):
    tm = z_ref.shape[0]
    u = z_ref[:, :D_MODEL]
    v = z_ref[:, D_MODEL:]
    mu = jnp.mean(v, axis=-1, keepdims=True)
    vc = v - mu
    vn = vc * lax.rsqrt(jnp.mean(vc * vc, axis=-1, keepdims=True) + EPS)
    vn = (vn * lng_ref[...] + lnb_ref[...]).astype(BF16)
    gd = D_MODEL // GM_GROUPS
    for c in range(tm // GM_CHUNK):
        rows = slice(c * GM_CHUNK, (c + 1) * GM_CHUNK)
        for g in range(GM_GROUPS):
            cols = slice(g * gd, (g + 1) * gd)
            sv = jnp.dot(ws_ref[g], vn[rows, cols], preferred_element_type=F32)
            y_ref[rows, cols] = (u[rows, cols] * (sv + bs_ref[:, cols])).astype(y_ref.dtype)


def gmlp_gate(z, ln_g, ln_b, ws_masked, bs_full, tm=256):
    T = z.shape[0]
    return pl.pallas_call(
        _gate_kernel,
        grid=(T // tm,),
        in_specs=[pl.BlockSpec((tm, 2 * D_MODEL), lambda i: (i, 0)),
                  pl.BlockSpec((1, D_MODEL), lambda i: (0, 0)),
                  pl.BlockSpec((1, D_MODEL), lambda i: (0, 0)),
                  pl.BlockSpec((GM_GROUPS, GM_CHUNK, GM_CHUNK), lambda i: (0, 0, 0)),
                  pl.BlockSpec((GM_CHUNK, D_MODEL), lambda i: (0, 0))],
        out_specs=pl.BlockSpec((tm, D_MODEL), lambda i: (i, 0)),
        out_shape=jax.ShapeDtypeStruct((T, D_MODEL), BF16),
        compiler_params=_params(("parallel",)),
    )(z, ln_g.reshape(1, D_MODEL), ln_b.reshape(1, D_MODEL), ws_masked, bs_full)


def _compress_kernel(x_ref, pe_ref, w1_ref, w2_ref, o_ref, ot_ref):
    x = x_ref[0]
    half = CMP_STRIDE * HEAD_DIM
    xa = (x + pe_ref[0, 0:1, :]).astype(BF16)
    xb = (x + pe_ref[0, 1:2, :]).astype(BF16)
    a = jnp.dot(xa, w1_ref[0, :half, :], preferred_element_type=F32)
    b = jnp.dot(xb, w1_ref[0, half:, :], preferred_element_type=F32)
    n = a.shape[0]
    pre = a + pltpu.roll(b, n - 1, axis=0)
    out = jnp.dot(_gelu(pre).astype(BF16), w2_ref[0], preferred_element_type=F32)
    o_ref[0] = out.astype(o_ref.dtype)
    ot_ref[0] = out.T.astype(ot_ref.dtype)


def compress_kv(kvc, pe2, w1, w2):
    B, H2, S, _ = kvc.shape
    n_grp = S // CMP_STRIDE
    x = kvc.reshape(B * H2, n_grp, CMP_STRIDE * HEAD_DIM)
    sel = lambda i: ((i % H2) // N_KV_HEADS, 0, 0)
    return pl.pallas_call(
        _compress_kernel,
        grid=(B * H2,),
        in_specs=[pl.BlockSpec((1, n_grp, CMP_STRIDE * HEAD_DIM), lambda i: (i, 0, 0)),
                  pl.BlockSpec((1, 2, CMP_STRIDE * HEAD_DIM), sel),
                  pl.BlockSpec((1, CMP_LEN * HEAD_DIM, HEAD_DIM), sel),
                  pl.BlockSpec((1, HEAD_DIM, HEAD_DIM), sel)],
        out_specs=[pl.BlockSpec((1, n_grp, HEAD_DIM), lambda i: (i, 0, 0)),
                   pl.BlockSpec((1, HEAD_DIM, n_grp), lambda i: (i, 0, 0))],
        out_shape=[jax.ShapeDtypeStruct((B * H2, n_grp, HEAD_DIM), BF16),
                   jax.ShapeDtypeStruct((B * H2, HEAD_DIM, n_grp), BF16)],
        compiler_params=_params(("parallel",)),
    )(x, pe2, w1, w2)


def _nsa_kernel(q_ref, kc_ref, vct_ref, ksa_ref, vst_ref, kw_ref, vwt_ref, gt_ref, ovt_ref, wm_ref,
                o_ref, qaug_ref, s_a, s_b):
    nq = GQA * Q_BLK
    qb = pl.program_id(2)
    q0 = qb * Q_BLK
    q_t = q_ref[0].reshape(nq, HEAD_DIM).astype(F32).T.astype(BF16)
    t_lane = q0 + (lax.broadcasted_iota(jnp.int32, (1, nq), 1) & (Q_BLK - 1))

    n_c = kc_ref.shape[2]

    def compressed(n):
        s = jnp.dot(kc_ref[0, 0, :n, :], q_t, preferred_element_type=F32)
        c_end = lax.broadcasted_iota(jnp.int32, (n, 1), 0) * CMP_STRIDE + (CMP_LEN - 1)
        s = jnp.where(c_end <= t_lane, s, NEG)
        m = jnp.max(s, axis=0, keepdims=True)
        p = jnp.exp2(s - m)
        l = jnp.sum(p, axis=0, keepdims=True)
        p = p * jnp.where(m > 0.5 * NEG, 1.0 / l, 0.0)
        o = jnp.dot(vct_ref[0, 0, :, :n], p.astype(BF16), preferred_element_type=F32)
        psum = p[:, 0:Q_BLK]
        for h in range(1, GQA):
            psum = psum + p[:, h * Q_BLK:(h + 1) * Q_BLK]
        p_hi = psum.astype(BF16)
        p_lo = (psum - p_hi.astype(F32)).astype(BF16)
        ov = ovt_ref[:, :n]
        return o, (jnp.dot(ov, p_hi, preferred_element_type=F32)
                   + jnp.dot(ov, p_lo, preferred_element_type=F32))

    n_bkt = 4 if n_c % (4 * LANE) == 0 else 1
    bkt = n_c // n_bkt
    if n_bkt == 1:
        o_cmp, imp = compressed(n_c)
    else:
        o_cmp, imp = lax.switch((q0 + Q_BLK - 1) // (bkt * CMP_STRIDE),
                                [functools.partial(compressed, bkt * (i + 1)) for i in range(n_bkt)])

    n_sel = imp.shape[0]
    blk = lax.broadcasted_iota(jnp.int32, (n_sel, Q_BLK), 0)
    t_q = q0 + lax.broadcasted_iota(jnp.int32, (n_sel, Q_BLK), 1)
    dist = (t_q // SEL_BLK) - blk
    forced = (blk == 0) | ((dist >= 0) & (dist < SEL_LOCAL))
    score = jnp.where(forced, -jnp.inf, jnp.where(blk * SEL_BLK <= t_q, imp, NEG))
    blk_f = blk.astype(F32)
    for _ in range(SEL_TOPN - 1 - SEL_LOCAL):
        top = jnp.max(score, axis=0, keepdims=True)
        first = jnp.min(jnp.where(score == top, blk_f, float(n_sel)), axis=0, keepdims=True)
        score = jnp.where(blk_f == first, -jnp.inf, score)
    bias = jnp.where((score == -jnp.inf) & (blk * SEL_BLK < q0), 0.0, NEG)

    n_half = qaug_ref.shape[0]
    if n_sel < LANE:
        bias = jnp.concatenate([bias, jnp.zeros((LANE - n_sel, Q_BLK), F32)], axis=0)
    for hf in range(n_half):
        bq = bias[hf * LANE:(hf + 1) * LANE, :].astype(BF16)
        qaug_ref[hf] = jnp.concatenate([q_t, jnp.concatenate([bq] * GQA, axis=1)], axis=0)

    wk = WINDOW + Q_BLK
    kstart = pl.multiple_of(jnp.maximum(q0 - WINDOW, 0), LANE)
    sw = jnp.dot(kw_ref[0, 0, pl.ds(kstart, wk), :], q_t, preferred_element_type=F32)
    sw = sw + jnp.concatenate([wm_ref[0]] * GQA, axis=1)
    m_w = jnp.max(sw, axis=0, keepdims=True)
    p_w = jnp.exp2((sw - m_w).astype(BF16))
    vw = jnp.concatenate([vwt_ref[0, 0, kstart // LANE + j] for j in range(wk // LANE)], axis=1)
    vw = jnp.concatenate([vw, jnp.ones((ONES_ROWS, wk), BF16)], axis=0)
    acc_w = jnp.dot(vw, p_w, preferred_element_type=F32)
    o_win = acc_w[:HEAD_DIM] * (1.0 / acc_w[HEAD_DIM:HEAD_DIM + 1])

    kpos = q0 + lax.broadcasted_iota(jnp.int32, (Q_BLK, 1), 0)
    s_d = jnp.dot(ksa_ref[0, 0, pl.ds(pl.multiple_of(q0, Q_BLK), Q_BLK), :HEAD_DIM], q_t,
                  preferred_element_type=F32)
    s_d = jnp.where(kpos <= t_lane, s_d, NEG)
    m_d = jnp.max(s_d, axis=0, keepdims=True)
    p_d = jnp.exp2(s_d - m_d)
    v_d = jnp.concatenate([vst_ref[0, 0, q0 // LANE + j] for j in range(Q_BLK // LANE)], axis=1)
    init = (m_d, jnp.sum(p_d, axis=0, keepdims=True),
            jnp.dot(v_d, p_d.astype(BF16), preferred_element_type=F32))

    bpt = SEL_TILE // SEL_BLK
    vpt = SEL_TILE // LANE

    def scores(kt):
        k0 = pl.multiple_of(kt * SEL_TILE, SEL_TILE)
        return jnp.dot(ksa_ref[0, 0, pl.ds(k0, SEL_TILE), :], qaug_ref[(kt * bpt) // LANE],
                       preferred_element_type=F32)

    def softmax_pv(kt, st, carry):
        m_i, l_i, acc = carry
        m_new = jnp.maximum(m_i, jnp.max(st, axis=0, keepdims=True))
        alpha = jnp.exp2(m_i - m_new)
        pt = jnp.exp2(st - m_new)
        l_new = alpha * l_i + jnp.sum(pt, axis=0, keepdims=True)
        vt = jnp.concatenate([vst_ref[0, 0, kt * vpt + j] for j in range(vpt)], axis=1)
        acc = alpha * acc + jnp.dot(vt, pt.astype(BF16), preferred_element_type=F32)
        return m_new, l_new, acc

    def trip(j, carry):
        st = s_a[...]
        s_b[...] = scores(2 * j + 1)
        carry = softmax_pv(2 * j, st, carry)
        st = s_b[...]
        s_a[...] = scores(2 * j + 2)
        return softmax_pv(2 * j + 1, st, carry)

    n_tiles = (q0 + SEL_TILE - 1) // SEL_TILE
    n_trips = jnp.maximum((n_tiles + 1) // 2, 1)
    s_a[...] = scores(0)
    carry = lax.fori_loop(0, n_trips - 1, trip, init)
    last = 2 * n_trips - 1
    st = s_a[...]
    s_b[...] = scores(last)
    carry = softmax_pv(last - 1, st, carry)
    _, l_s, acc_s = softmax_pv(last, s_b[...], carry)
    o_sel = acc_s * (1.0 / l_s)

    for h in range(GQA):
        cols = slice(h * Q_BLK, (h + 1) * Q_BLK)
        o_t = (gt_ref[0, 0, 0, h:h + 1, :] * o_cmp[:, cols]
               + gt_ref[0, 0, 1, h:h + 1, :] * o_sel[:, cols]
               + gt_ref[0, 0, 2, h:h + 1, :] * o_win[:, cols])
        o_ref[0, :, h * HEAD_DIM:(h + 1) * HEAD_DIM] = o_t.T.astype(o_ref.dtype)


def _window_mask(S):
    wk = WINDOW + Q_BLK
    i = np.arange(wk)[:, None]
    qi = np.arange(Q_BLK)[None, :]
    out = []
    for v in range(WINDOW // Q_BLK + 1):
        t = Q_BLK * v + qi if v < WINDOW // Q_BLK else WINDOW + qi
        out.append(np.where((i <= t) & (i > t - WINDOW), 0.0, NEG))
    return jnp.asarray(np.stack(out), dtype=F32)


def nsa_attention_core(q, kc, vct, ks_aug, kw, vswt, gates_t, ov_t):
    B, _, S, _ = q.shape
    n_c = S // CMP_STRIDE
    n_sel = S // SEL_BLK
    nq = GQA * Q_BLK
    hk = N_KV_HEADS
    wk = WINDOW + Q_BLK
    nv = WINDOW // Q_BLK
    once = pl.Buffered(1)
    return pl.pallas_call(
        _nsa_kernel,
        grid=(B, hk, S // Q_BLK),
        in_specs=[
            pl.BlockSpec((1, GQA, Q_BLK, HEAD_DIM), lambda b, h, i: (b, h, i, 0)),
            pl.BlockSpec((1, 1, n_c, HEAD_DIM), lambda b, h, i: (b, h, 0, 0)),
            pl.BlockSpec((1, 1, HEAD_DIM, n_c), lambda b, h, i: (b, hk + h, 0, 0)),
            pl.BlockSpec((1, 1, S, 2 * HEAD_DIM), lambda b, h, i: (b, h, 0, 0), pipeline_mode=once),
            pl.BlockSpec((1, 1, S // LANE, HEAD_DIM, LANE), lambda b, h, i: (b, h, 0, 0, 0),
                         pipeline_mode=once),
            pl.BlockSpec((1, 1, S, HEAD_DIM), lambda b, h, i: (b, hk + h, 0, 0), pipeline_mode=once),
            pl.BlockSpec((1, 1, S // LANE, HEAD_DIM, LANE), lambda b, h, i: (b, hk + h, 0, 0, 0),
                         pipeline_mode=once),
            pl.BlockSpec((1, 1, 3, GQA, Q_BLK), lambda b, h, i: (b, h, 0, 0, i)),
            pl.BlockSpec((n_sel, n_c), lambda b, h, i: (0, 0)),
            pl.BlockSpec((1, wk, Q_BLK), lambda b, h, i: (jnp.minimum(i, nv), 0, 0)),
        ],
        out_specs=pl.BlockSpec((1, Q_BLK, GQA * HEAD_DIM), lambda b, h, i: (b, i, h)),
        out_shape=jax.ShapeDtypeStruct((B, S, N_HEADS * HEAD_DIM), BF16),
        scratch_shapes=[pltpu.VMEM((max(1, n_sel // LANE), 2 * HEAD_DIM, nq), BF16),
                        pltpu.VMEM((SEL_TILE, nq), F32), pltpu.VMEM((SEL_TILE, nq), F32)],
        compiler_params=_params(("parallel", "parallel", "arbitrary")),
    )(q, kc, vct, ks_aug, vswt, kw, vswt, gates_t, ov_t, _window_mask(S))


def _moe_kernel(nused_ref, bexp_ref, rtok_ref, h_hbm, wg_ref, wu_ref, wd_ref, y_ref,
                x_even, x_odd, wg_b, wu_b, wd_b, sem):
    i = pl.program_id(0)
    n_used = nused_ref[0]
    blk = x_even.shape[0]
    bufs = (x_even, x_odd)

    def row_copy(b, r, slot):
        tok = rtok_ref[b * blk + r]
        return pltpu.make_async_copy(h_hbm.at[pl.ds(tok, 1)], bufs[slot].at[pl.ds(r, 1)],
                                     sem.at[slot])

    def wait_block(slot):
        pltpu.make_async_copy(h_hbm.at[pl.ds(0, blk)], bufs[slot], sem.at[slot]).wait()

    @pl.when(i == 0)
    def _():
        def body(r, c):
            row_copy(0, r, 0).start()
            return c
        lax.fori_loop(0, blk, body, 0, unroll=8)

    @pl.when((i < n_used) & ((i == 0) | (bexp_ref[i] != bexp_ref[jnp.maximum(i - 1, 0)])))
    def _():
        wg_b[...] = wg_ref[0, 0].astype(BF16)
        wu_b[...] = wu_ref[0, 0].astype(BF16)
        wd_b[...] = wd_ref[0, 0].astype(BF16)

    for slot in range(2):
        @pl.when((i < n_used) & (i % 2 == slot))
        def _(slot=slot):
            wait_block(slot)
            for r in range(blk):
                row_copy(i + 1, r, 1 - slot).start()
            x = jnp.concatenate(_unpack_rows(bufs[slot][...]), axis=1).astype(BF16)
            gate = jnp.dot(x, wg_b[...], preferred_element_type=F32)
            up = jnp.dot(x, wu_b[...], preferred_element_type=F32)
            hid = (gate * jax.nn.sigmoid(gate) * up).astype(BF16)
            y_ref[...] = _pack_rows(jnp.dot(hid, wd_b[...], preferred_element_type=F32))

        @pl.when((i == n_used) & (i % 2 == slot))
        def _(slot=slot):
            wait_block(slot)

    @pl.when(i >= n_used)
    def _():
        y_ref[...] = jnp.zeros_like(y_ref)


def moe_experts(xn, n_used, block_exp, row_tok, wg, wu, wd, layer, blk=MOE_BLK):
    n_rows = row_tok.shape[0]
    n_steps = n_rows // blk
    grid_spec = pltpu.PrefetchScalarGridSpec(
        num_scalar_prefetch=3,
        grid=(n_steps,),
        in_specs=[
            pl.BlockSpec(memory_space=pl.ANY),
            pl.BlockSpec((1, 1, D_MODEL, D_EXPERT), lambda i, nu, be, rt: (layer, be[i], 0, 0)),
            pl.BlockSpec((1, 1, D_MODEL, D_EXPERT), lambda i, nu, be, rt: (layer, be[i], 0, 0)),
            pl.BlockSpec((1, 1, D_EXPERT, D_MODEL), lambda i, nu, be, rt: (layer, be[i], 0, 0)),
        ],
        out_specs=pl.BlockSpec((blk, D_MODEL // 2), lambda i, nu, be, rt: (i, 0)),
        scratch_shapes=[pltpu.VMEM((blk, D_MODEL // 2), jnp.uint32),
                        pltpu.VMEM((blk, D_MODEL // 2), jnp.uint32),
                        pltpu.VMEM((D_MODEL, D_EXPERT), BF16), pltpu.VMEM((D_MODEL, D_EXPERT), BF16),
                        pltpu.VMEM((D_EXPERT, D_MODEL), BF16), pltpu.SemaphoreType.DMA((2,))],
    )
    return pl.pallas_call(
        _moe_kernel, grid_spec=grid_spec,
        out_shape=jax.ShapeDtypeStruct((n_rows, D_MODEL // 2), jnp.uint32),
        compiler_params=_params(("arbitrary",)),
    )(n_used, block_exp, row_tok, xn, wg, wu, wd)


def _combine_kernel(pos_ref, h_ref, w_ref, g_ref, y_hbm, o_ref, n_ref, ybuf, sem):
    i = pl.program_id(0)
    n = pl.num_programs(0)
    tm = h_ref.shape[0]

    def issue(b, slot):
        def body(j, c):
            for k in range(TOP_K):
                row = pos_ref[(b * tm + j) * TOP_K + k]
                pltpu.make_async_copy(y_hbm.at[pl.ds(row, 1)], ybuf.at[slot, pl.ds(k * tm + j, 1)],
                                      sem.at[slot]).start()
            return c
        lax.fori_loop(0, tm, body, 0, unroll=4)

    @pl.when(i == 0)
    def _():
        issue(0, 0)

    @pl.when(i + 1 < n)
    def _():
        issue(i + 1, (i + 1) % 2)

    slot = i % 2
    pltpu.make_async_copy(y_hbm.at[pl.ds(0, TOP_K * tm)], ybuf.at[slot], sem.at[slot]).wait()
    half = D_MODEL // 2
    lo, hi = h_ref[:, :half], h_ref[:, half:]
    for k in range(TOP_K):
        y_lo, y_hi = _unpack_rows(ybuf[slot, k * tm:(k + 1) * tm, :])
        lo = lo + y_lo * w_ref[:, k:k + 1]
        hi = hi + y_hi * w_ref[:, k:k + 1]
    out = jnp.concatenate([lo, hi], axis=1)
    o_ref[...] = out
    n_ref[...] = _rms(out, g_ref[...]).astype(n_ref.dtype)


def moe_combine(h, y, pos, weight, g_next, norm_dtype, tm=256):
    T = h.shape[0]
    grid_spec = pltpu.PrefetchScalarGridSpec(
        num_scalar_prefetch=1,
        grid=(T // tm,),
        in_specs=[
            pl.BlockSpec((tm, D_MODEL), lambda i, p: (i, 0)),
            pl.BlockSpec((tm, TOP_K), lambda i, p: (i, 0)),
            pl.BlockSpec((1, D_MODEL), lambda i, p: (0, 0)),
            pl.BlockSpec(memory_space=pl.ANY),
        ],
        out_specs=[pl.BlockSpec((tm, D_MODEL), lambda i, p: (i, 0)),
                   pl.BlockSpec((tm, D_MODEL), lambda i, p: (i, 0))],
        scratch_shapes=[pltpu.VMEM((2, TOP_K * tm, D_MODEL // 2), jnp.uint32),
                        pltpu.SemaphoreType.DMA((2,))],
    )
    return pl.pallas_call(
        _combine_kernel, grid_spec=grid_spec,
        out_shape=[jax.ShapeDtypeStruct((T, D_MODEL), F32),
                   jax.ShapeDtypeStruct((T, D_MODEL), norm_dtype)],
        compiler_params=_params(("arbitrary",)),
    )(pos.reshape(-1), h, weight, g_next.reshape(1, D_MODEL), y)


def _dispatch(expert, T, blk):
    onehot = (expert[:, None] == jnp.arange(N_EXPERTS, dtype=jnp.int32)[None, :]).astype(jnp.int32)
    csum = jnp.cumsum(onehot, axis=0)
    rank = jnp.sum(csum * onehot, axis=1) - 1
    counts = csum[-1]
    padded = (counts + blk - 1) // blk * blk
    pad_end = jnp.cumsum(padded)
    pad_start = pad_end - padded
    pos = (jnp.sum(pad_start[None, :] * onehot, axis=1) + rank).astype(jnp.int32)
    n_rows = T * TOP_K + (N_EXPERTS + 1) * blk
    token = jnp.repeat(jnp.arange(T, dtype=jnp.int32), TOP_K)
    row_tok = jnp.zeros((n_rows,), jnp.int32).at[pos].set(token, unique_indices=True)
    n_blocks = n_rows // blk
    starts = jnp.arange(n_blocks, dtype=jnp.int32) * blk
    block_exp = jnp.minimum(jnp.sum((pad_end[None, :] <= starts[:, None]).astype(jnp.int32), axis=1),
                            N_EXPERTS - 1).astype(jnp.int32)
    n_used = (pad_end[-1] // blk).astype(jnp.int32).reshape(1)
    return pos.reshape(T, TOP_K), row_tok, block_exp, n_used


def _moe_layer(h, norm_g, w_group, b_group, w_exp, b_exp, wg, wu, wd, layer, g_next, norm_dtype):
    T = h.shape[0]
    wr = jnp.zeros((D_MODEL, LANE), F32)
    wr = wr.at[:, :N_GROUPS].set(w_group).at[:, N_GROUPS:N_GROUPS + N_EXPERTS].set(w_exp)
    br = jnp.zeros((LANE,), F32).at[:N_GROUPS].set(b_group).at[N_GROUPS:N_GROUPS + N_EXPERTS].set(b_exp)
    e_t, w_t, xn = router(h, norm_g, wr, br.reshape(LANE, 1))
    expert = e_t[:TOP_K].T.reshape(-1)
    weight = w_t[:TOP_K].T
    pos, row_tok, block_exp, n_used = _dispatch(expert, T, MOE_BLK)
    y = moe_experts(xn, n_used, block_exp, row_tok, wg, wu, wd, layer)
    return moe_combine(h, y, pos, weight, g_next, norm_dtype)


def _gmlp_layer(h, hn, w_in, ln_g, ln_b, w_s, b_s, w_out):
    z = matmul(hn, w_in.astype(BF16), out_dtype=F32, act="gelu")
    causal = jnp.tril(jnp.ones((GM_CHUNK, GM_CHUNK), w_s.dtype))
    bs_full = jnp.repeat(b_s.T, D_MODEL // GM_GROUPS, axis=1)
    y = gmlp_gate(z, ln_g, ln_b, (w_s * causal).astype(BF16), bs_full)
    return matmul(y, w_out.astype(BF16), out_dtype=F32, resid=h)


def _overlap_t(S):
    n_c = S // CMP_STRIDE
    n_sel = S // SEL_BLK
    ci = np.arange(n_c)[None, :] * CMP_STRIDE
    sj = np.arange(n_sel)[:, None] * SEL_BLK
    ov = (ci < sj + SEL_BLK) & (ci + CMP_LEN > sj) & (np.arange(n_c)[None, :] < n_c - 1)
    return jnp.asarray(ov.astype(np.float32), dtype=BF16)


def _nsa_layer(h, hn, B, S, w_in, ck_pe, ck_w1, ck_w2, cv_pe, cv_w1, cv_w2, w_out):
    qd = N_HEADS * HEAD_DIM
    kvd = N_KV_HEADS * HEAD_DIM
    w = w_in.astype(BF16)
    o = [qd + i * kvd for i in range(7)]
    w_q, w_kc, w_vc, w_ks, w_vs, w_kw, w_vw = (
        w[:, :o[0]], w[:, o[0]:o[1]], w[:, o[1]:o[2]], w[:, o[2]:o[3]], w[:, o[3]:o[4]],
        w[:, o[4]:o[5]], w[:, o[5]:o[6]])
    w_g = jnp.zeros((D_MODEL, LANE), BF16).at[:, :3 * N_HEADS].set(w[:, o[6]:])
    q = matmul(hn, w_q, out_dtype=BF16, scale=HEAD_DIM ** -0.5 * LOG2E, layout="heads", seq=S)
    kvc = matmul(hn, jnp.concatenate([w_kc, w_vc], 1), out_dtype=F32, layout="heads", seq=S)
    ksw = matmul(hn, jnp.concatenate([w_ks, w_kw], 1), out_dtype=BF16, layout="heads", seq=S)
    vswt = matmul(hn, jnp.concatenate([w_vs, w_vw], 1), out_dtype=BF16, layout="heads_t", seq=S)
    gates = matmul(hn, w_g, out_dtype=F32, act="sigmoid")[:, :3 * N_HEADS]
    gates_t = gates.reshape(B, S, N_KV_HEADS, GQA, 3).transpose(0, 2, 4, 3, 1)

    half = CMP_STRIDE
    pe2 = jnp.stack([jnp.stack([pe[:half].reshape(-1), pe[half:].reshape(-1)])
                     for pe in (ck_pe, cv_pe)])
    w1 = jnp.stack([ck_w1, cv_w1]).astype(BF16)
    w2 = jnp.stack([ck_w2, cv_w2]).astype(BF16)
    cmp, cmp_t = compress_kv(kvc, pe2, w1, w2)
    n_c = S // CMP_STRIDE
    cmp = cmp.reshape(B, 2 * N_KV_HEADS, n_c, HEAD_DIM)
    cmp_t = cmp_t.reshape(B, 2 * N_KV_HEADS, HEAD_DIM, n_c)
    blk_id = (np.arange(S) // SEL_BLK) % LANE
    onehot = jnp.asarray(blk_id[:, None] == np.arange(LANE)[None, :], dtype=BF16)
    ks_aug = jnp.concatenate(
        [ksw[:, :N_KV_HEADS], jnp.broadcast_to(onehot, (B, N_KV_HEADS, S, LANE))], axis=-1)
    o_att = nsa_attention_core(q, cmp, cmp_t, ks_aug, ksw, vswt, gates_t, _overlap_t(S))
    return matmul(o_att.reshape(B * S, qd), w_out.astype(BF16), out_dtype=F32, resid=h)


def kernel(x, norm_mix, norm_ffn, norm_final, a_w_in, a_ln_g, a_ln_b, a_w_s, a_b_s, a_w_out,
           b_w_in, b_ck_pe, b_ck_w1, b_ck_w2, b_cv_pe, b_cv_w1, b_cv_w2, b_w_out,
           r_w_group, r_b_group, r_w_exp, r_b_exp, e_w_gate, e_w_up, e_w_down):
    B, S, _ = x.shape
    h = x.reshape(B * S, D_MODEL)
    hn = rmsnorm(h, norm_mix[0])
    h = _gmlp_layer(h, hn, a_w_in[0], a_ln_g[0], a_ln_b[0], a_w_s[0], a_b_s[0], a_w_out[0])
    h, hn = _moe_layer(h, norm_ffn[0], r_w_group[0], r_b_group[0], r_w_exp[0], r_b_exp[0],
                       e_w_gate, e_w_up, e_w_down, 0, norm_mix[1], BF16)
    h = _nsa_layer(h, hn, B, S, b_w_in[0], b_ck_pe[0], b_ck_w1[0], b_ck_w2[0],
                   b_cv_pe[0], b_cv_w1[0], b_cv_w2[0], b_w_out[0])
    _, out = _moe_layer(h, norm_ffn[1], r_w_group[1], r_b_group[1], r_w_exp[1], r_b_exp[1],
                        e_w_gate, e_w_up, e_w_down, 1, norm_final, F32)
    return out.reshape(B, S, D_MODEL)
```

```python
import functools

import numpy as np
import jax
import jax.numpy as jnp
from jax import lax
from jax.experimental import pallas as pl
from jax.experimental.pallas import tpu as pltpu

F32 = jnp.float32
BF16 = jnp.bfloat16

D_MODEL = 2048
LANE = 128
GM_GROUPS = 8
GM_CHUNK = 128
HEAD_DIM = 128
N_HEADS = 16
N_KV_HEADS = 4
GQA = N_HEADS // N_KV_HEADS
CMP_LEN = 32
CMP_STRIDE = 16
SEL_BLK = 64
SEL_TOPN = 16
SEL_LOCAL = 2
WINDOW = 512
Q_BLK = 128
FORCE = 1.0e4
N_GROUPS = 4
EXP_PER_GROUP = 8
N_EXPERTS = N_GROUPS * EXP_PER_GROUP
TOP_K = 2
D_EXPERT = 512
EPS = 1e-6
NEG = -1e30
LOG2E = 1.4426950408889634
ONES_ROWS = 16
SEL_TILE = 512
MOE_BLK = 512
VMEM_LIMIT = 56 * 1024 * 1024

_NT = (((1,), (1,)), ((), ()))


def _params(sem):
    return pltpu.CompilerParams(dimension_semantics=sem, vmem_limit_bytes=VMEM_LIMIT)


def _gelu(x):
    return 0.5 * x * (1.0 + jnp.tanh(0.7978845608028654 * (x + 0.044715 * (x * x * x))))


def _rms(x, g):
    y = x * lax.rsqrt(jnp.mean(x * x, axis=-1, keepdims=True) + EPS)
    return y * g


def _pack_rows(x):
    half = x.shape[1] // 2
    bits = lambda v: lax.bitcast_convert_type(v.astype(BF16).astype(F32), jnp.uint32)
    return (bits(x[:, :half]) >> 16) | (bits(x[:, half:]) & jnp.uint32(0xFFFF0000))


def _unpack_rows(w):
    lo = lax.bitcast_convert_type(w << 16, F32)
    hi = lax.bitcast_convert_type(w & jnp.uint32(0xFFFF0000), F32)
    return lo, hi


def _rmsnorm_kernel(x_ref, g_ref, o_ref):
    o_ref[...] = _rms(x_ref[...], g_ref[...]).astype(o_ref.dtype)


def rmsnorm(x, g, out_dtype=BF16, tm=512):
    T = x.shape[0]
    return pl.pallas_call(
        _rmsnorm_kernel,
        grid=(T // tm,),
        in_specs=[pl.BlockSpec((tm, D_MODEL), lambda i: (i, 0)),
                  pl.BlockSpec((1, D_MODEL), lambda i: (0, 0))],
        out_specs=pl.BlockSpec((tm, D_MODEL), lambda i: (i, 0)),
        out_shape=jax.ShapeDtypeStruct((T, D_MODEL), out_dtype),
        compiler_params=_params(("parallel",)),
    )(x, g.reshape(1, D_MODEL))


def _router_kernel(x_ref, g_ref, wh_ref, wl_ref, br_ref, e_ref, w_ref, xn_ref):
    y = _rms(x_ref[...], g_ref[...])
    xn_ref[...] = _pack_rows(y)
    y_hi = y.astype(BF16)
    y_lo = (y - y_hi.astype(F32)).astype(BF16)
    lg = (jnp.dot(y_hi, wh_ref[...], preferred_element_type=F32)
          + jnp.dot(y_hi, wl_ref[...], preferred_element_type=F32)
          + jnp.dot(y_lo, wh_ref[...], preferred_element_type=F32))
    lt = lg.T + br_ref[...]
    row = lambda i: lt[i:i + 1, :]

    def softmax(rows):
        m = functools.reduce(jnp.maximum, rows)
        e = [jnp.exp(r - m) for r in rows]
        s = functools.reduce(lambda a, b: a + b, e)
        return [v / s for v in e]

    def top1(vals, skip=None):
        best = jnp.full_like(vals[0], -1.0)
        idx = jnp.zeros(vals[0].shape, jnp.int32)
        for k, v in enumerate(vals):
            if skip is not None:
                v = jnp.where(skip == k, -1.0, v)
            upd = v > best
            best = jnp.where(upd, v, best)
            idx = jnp.where(upd, k, idx)
        return best, idx

    g_w, g_idx = top1(softmax([row(j) for j in range(N_GROUPS)]))
    e_logits = []
    for k in range(EXP_PER_GROUP):
        v = row(N_GROUPS + k)
        for gg in range(1, N_GROUPS):
            v = jnp.where(g_idx == gg, row(N_GROUPS + gg * EXP_PER_GROUP + k), v)
        e_logits.append(v)
    e_prob = softmax(e_logits)
    w1, i1 = top1(e_prob)
    w2, i2 = top1(e_prob, skip=i1)
    tot = w1 + w2
    zi = jnp.zeros((6, lt.shape[1]), jnp.int32)
    e_ref[...] = jnp.concatenate([g_idx * EXP_PER_GROUP + i1, g_idx * EXP_PER_GROUP + i2, zi], axis=0)
    w_ref[...] = jnp.concatenate([g_w * (w1 / tot), g_w * (w2 / tot), zi.astype(F32)], axis=0)


def router(x, g, wr, br, tm=512):
    T = x.shape[0]
    wr_hi = wr.astype(BF16)
    wr_lo = (wr - wr_hi.astype(F32)).astype(BF16)
    return pl.pallas_call(
        _router_kernel,
        grid=(T // tm,),
        in_specs=[pl.BlockSpec((tm, D_MODEL), lambda i: (i, 0)),
                  pl.BlockSpec((1, D_MODEL), lambda i: (0, 0)),
                  pl.BlockSpec((D_MODEL, LANE), lambda i: (0, 0)),
                  pl.BlockSpec((D_MODEL, LANE), lambda i: (0, 0)),
                  pl.BlockSpec((LANE, 1), lambda i: (0, 0))],
        out_specs=[pl.BlockSpec((8, tm), lambda i: (0, i)),
                   pl.BlockSpec((8, tm), lambda i: (0, i)),
                   pl.BlockSpec((tm, D_MODEL // 2), lambda i: (i, 0))],
        out_shape=[jax.ShapeDtypeStruct((8, T), jnp.int32),
                   jax.ShapeDtypeStruct((8, T), F32),
                   jax.ShapeDtypeStruct((T, D_MODEL // 2), jnp.uint32)],
        compiler_params=_params(("parallel",)),
    )(x, g.reshape(1, D_MODEL), wr_hi, wr_lo, br)


def _mm_kernel(*refs, act, scale, has_resid, layout):
    a_ref, w_ref = refs[0], refs[1]
    o_ref = refs[-1]
    acc = jnp.dot(a_ref[...], w_ref[...], preferred_element_type=F32)
    if scale is not None:
        acc = acc * scale
    if act == "gelu":
        acc = _gelu(acc)
    elif act == "sigmoid":
        acc = jax.nn.sigmoid(acc)
    if has_resid:
        acc = acc + refs[2][...]
    tm, tn = acc.shape
    if layout == "plain":
        o_ref[...] = acc.astype(o_ref.dtype)
    elif layout == "heads":
        for h in range(tn // HEAD_DIM):
            o_ref[0, h] = acc[:, h * HEAD_DIM:(h + 1) * HEAD_DIM].astype(o_ref.dtype)
    else:
        for h in range(tn // HEAD_DIM):
            for c in range(tm // LANE):
                blk = acc[c * LANE:(c + 1) * LANE, h * HEAD_DIM:(h + 1) * HEAD_DIM]
                o_ref[0, h, c] = blk.T.astype(o_ref.dtype)


def matmul(a, w, *, out_dtype, act=None, scale=None, resid=None, layout="plain",
           seq=None, tm=512, tn=1024):
    M, K = a.shape
    N = w.shape[1]
    tn = min(tn, N)
    grid = (N // tn, M // tm)
    in_specs = [pl.BlockSpec((tm, K), lambda j, i: (i, 0)),
                pl.BlockSpec((K, tn), lambda j, i: (0, j))]
    args = [a, w]
    if resid is not None:
        in_specs.append(pl.BlockSpec((tm, tn), lambda j, i: (i, j)))
        args.append(resid)
    if layout == "plain":
        out_shape = jax.ShapeDtypeStruct((M, N), out_dtype)
        out_spec = pl.BlockSpec((tm, tn), lambda j, i: (i, j))
    else:
        nb = seq // tm
        nh = tn // HEAD_DIM
        if layout == "heads":
            out_shape = jax.ShapeDtypeStruct((M // seq, N // HEAD_DIM, seq, HEAD_DIM), out_dtype)
            out_spec = pl.BlockSpec((1, nh, tm, HEAD_DIM), lambda j, i: (i // nb, j, i % nb, 0))
        else:
            out_shape = jax.ShapeDtypeStruct(
                (M // seq, N // HEAD_DIM, seq // LANE, HEAD_DIM, LANE), out_dtype)
            out_spec = pl.BlockSpec((1, nh, tm // LANE, HEAD_DIM, LANE),
                                    lambda j, i: (i // nb, j, i % nb, 0, 0))
    kern = functools.partial(_mm_kernel, act=act, scale=scale, has_resid=resid is not None,
                             layout=layout)
    return pl.pallas_call(
        kern, grid=grid, in_specs=in_specs, out_specs=out_spec, out_shape=out_shape,
        compiler_params=_params(("parallel", "parallel")),
    )(*args)


def _gate_kernel(a_ref, w_ref, lng_ref, lnb_ref, ws_ref, bs_ref, y_ref):
    tm = a_ref.shape[0]
    z = _gelu(jnp.dot(a_ref[...], w_ref[...], preferred_element_type=F32))
    u = z[:, :D_MODEL]
    v = z[:, D_MODEL:]
    mu = jnp.mean(v, axis=-1, keepdims=True)
    vc = v - mu
    vn = vc * lax.rsqrt(jnp.mean(vc * vc, axis=-1, keepdims=True) + EPS)
    vn = (vn * lng_ref[...] + lnb_ref[...]).astype(BF16)
    gd = D_MODEL // GM_GROUPS
    for c in range(tm // GM_CHUNK):
        rows = slice(c * GM_CHUNK, (c + 1) * GM_CHUNK)
        for g in range(GM_GROUPS):
            cols = slice(g * gd, (g + 1) * gd)
            sv = jnp.dot(ws_ref[g], vn[rows, cols], preferred_element_type=F32)
            y_ref[rows, cols] = (u[rows, cols] * (sv + bs_ref[:, cols])).astype(y_ref.dtype)


def gmlp_gate(a, w_in, ln_g, ln_b, ws_masked, bs_full, tm=512):
    T = a.shape[0]
    return pl.pallas_call(
        _gate_kernel,
        grid=(T // tm,),
        in_specs=[pl.BlockSpec((tm, D_MODEL), lambda i: (i, 0)),
                  pl.BlockSpec((D_MODEL, 2 * D_MODEL), lambda i: (0, 0), pipeline_mode=pl.Buffered(1)),
                  pl.BlockSpec((1, D_MODEL), lambda i: (0, 0)),
                  pl.BlockSpec((1, D_MODEL), lambda i: (0, 0)),
                  pl.BlockSpec((GM_GROUPS, GM_CHUNK, GM_CHUNK), lambda i: (0, 0, 0)),
                  pl.BlockSpec((GM_CHUNK, D_MODEL), lambda i: (0, 0))],
        out_specs=pl.BlockSpec((tm, D_MODEL), lambda i: (i, 0)),
        out_shape=jax.ShapeDtypeStruct((T, D_MODEL), BF16),
        compiler_params=_params(("parallel",)),
    )(a, w_in, ln_g.reshape(1, D_MODEL), ln_b.reshape(1, D_MODEL), ws_masked, bs_full)


def _compress_kernel(x_ref, pe_ref, w1_ref, w2_ref, o_ref, ot_ref):
    x = x_ref[0]
    half = CMP_STRIDE * HEAD_DIM
    xa = (x + pe_ref[0, 0:1, :]).astype(BF16)
    xb = (x + pe_ref[0, 1:2, :]).astype(BF16)
    a = jnp.dot(xa, w1_ref[0, :half, :], preferred_element_type=F32)
    b = jnp.dot(xb, w1_ref[0, half:, :], preferred_element_type=F32)
    n = a.shape[0]
    pre = a + pltpu.roll(b, n - 1, axis=0)
    out = jnp.dot(_gelu(pre).astype(BF16), w2_ref[0], preferred_element_type=F32)
    o_ref[0] = out.astype(o_ref.dtype)
    ot_ref[0] = out.T.astype(ot_ref.dtype)


def compress_kv(kvc, pe2, w1, w2):
    B, H2, S, _ = kvc.shape
    n_grp = S // CMP_STRIDE
    x = kvc.reshape(B * H2, n_grp, CMP_STRIDE * HEAD_DIM)
    sel = lambda i: ((i % H2) // N_KV_HEADS, 0, 0)
    return pl.pallas_call(
        _compress_kernel,
        grid=(B * H2,),
        in_specs=[pl.BlockSpec((1, n_grp, CMP_STRIDE * HEAD_DIM), lambda i: (i, 0, 0)),
                  pl.BlockSpec((1, 2, CMP_STRIDE * HEAD_DIM), sel),
                  pl.BlockSpec((1, CMP_LEN * HEAD_DIM, HEAD_DIM), sel),
                  pl.BlockSpec((1, HEAD_DIM, HEAD_DIM), sel)],
        out_specs=[pl.BlockSpec((1, n_grp, HEAD_DIM), lambda i: (i, 0, 0)),
                   pl.BlockSpec((1, HEAD_DIM, n_grp), lambda i: (i, 0, 0))],
        out_shape=[jax.ShapeDtypeStruct((B * H2, n_grp, HEAD_DIM), BF16),
                   jax.ShapeDtypeStruct((B * H2, HEAD_DIM, n_grp), BF16)],
        compiler_params=_params(("parallel",)),
    )(x, pe2, w1, w2)


def _nsa_kernel(q_ref, kc_ref, vct_ref, ksa_ref, vst_ref, kw_ref, vwt_ref, gt_ref, ovt_ref, wm_ref,
                o_ref, qaug_ref, s_a, s_b):
    nq = GQA * Q_BLK
    qb = pl.program_id(2)
    q0 = qb * Q_BLK
    q_t = q_ref[0].reshape(nq, HEAD_DIM).astype(F32).T.astype(BF16)
    t_lane = q0 + (lax.broadcasted_iota(jnp.int32, (1, nq), 1) & (Q_BLK - 1))

    n_c = kc_ref.shape[2]

    def compressed(n):
        s = jnp.dot(kc_ref[0, 0, :n, :], q_t, preferred_element_type=F32)
        c_end = lax.broadcasted_iota(jnp.int32, (n, 1), 0) * CMP_STRIDE + (CMP_LEN - 1)
        s = jnp.where(c_end <= t_lane, s, NEG)
        m = jnp.max(s, axis=0, keepdims=True)
        p = jnp.exp2(s - m)
        l = jnp.sum(p, axis=0, keepdims=True)
        p = p * jnp.where(m > 0.5 * NEG, 1.0 / l, 0.0)
        o = jnp.dot(vct_ref[0, 0, :, :n], p.astype(BF16), preferred_element_type=F32)
        psum = p[:, 0:Q_BLK]
        for h in range(1, GQA):
            psum = psum + p[:, h * Q_BLK:(h + 1) * Q_BLK]
        p_hi = psum.astype(BF16)
        p_lo = (psum - p_hi.astype(F32)).astype(BF16)
        ov = ovt_ref[:, :n]
        return o, (jnp.dot(ov, p_hi, preferred_element_type=F32)
                   + jnp.dot(ov, p_lo, preferred_element_type=F32))

    n_bkt = 4 if n_c % (4 * LANE) == 0 else 1
    bkt = n_c // n_bkt
    if n_bkt == 1:
        o_cmp, imp = compressed(n_c)
    else:
        o_cmp, imp = lax.switch((q0 + Q_BLK - 1) // (bkt * CMP_STRIDE),
                                [functools.partial(compressed, bkt * (i + 1)) for i in range(n_bkt)])

    n_sel = imp.shape[0]
    blk = lax.broadcasted_iota(jnp.int32, (n_sel, Q_BLK), 0)
    t_q = q0 + lax.broadcasted_iota(jnp.int32, (n_sel, Q_BLK), 1)
    dist = (t_q // SEL_BLK) - blk
    forced = (blk == 0) | ((dist >= 0) & (dist < SEL_LOCAL))
    score = jnp.where(forced, -jnp.inf, jnp.where(blk * SEL_BLK <= t_q, imp, NEG))
    blk_f = blk.astype(F32)
    for _ in range(SEL_TOPN - 1 - SEL_LOCAL):
        top = jnp.max(score, axis=0, keepdims=True)
        first = jnp.min(jnp.where(score == top, blk_f, float(n_sel)), axis=0, keepdims=True)
        score = jnp.where(blk_f == first, -jnp.inf, score)
    bias = jnp.where((score == -jnp.inf) & (blk * SEL_BLK < q0), 0.0, NEG)

    n_half = qaug_ref.shape[0]
    if n_sel < LANE:
        bias = jnp.concatenate([bias, jnp.zeros((LANE - n_sel, Q_BLK), F32)], axis=0)
    for hf in range(n_half):
        bq = bias[hf * LANE:(hf + 1) * LANE, :].astype(BF16)
        qaug_ref[hf] = jnp.concatenate([q_t, jnp.concatenate([bq] * GQA, axis=1)], axis=0)

    wk = WINDOW + Q_BLK
    kstart = pl.multiple_of(jnp.maximum(q0 - WINDOW, 0), LANE)
    sw = jnp.dot(kw_ref[0, 0, pl.ds(kstart, wk), :], q_t, preferred_element_type=F32)
    sw = sw + jnp.concatenate([wm_ref[0]] * GQA, axis=1)
    m_w = jnp.max(sw, axis=0, keepdims=True)
    p_w = jnp.exp2((sw - m_w).astype(BF16))
    vw = jnp.concatenate([vwt_ref[0, 0, kstart // LANE + j] for j in range(wk // LANE)], axis=1)
    vw = jnp.concatenate([vw, jnp.ones((ONES_ROWS, wk), BF16)], axis=0)
    acc_w = jnp.dot(vw, p_w, preferred_element_type=F32)
    o_win = acc_w[:HEAD_DIM] * (1.0 / acc_w[HEAD_DIM:HEAD_DIM + 1])

    kpos = q0 + lax.broadcasted_iota(jnp.int32, (Q_BLK, 1), 0)
    s_d = jnp.dot(ksa_ref[0, 0, pl.ds(pl.multiple_of(q0, Q_BLK), Q_BLK), :HEAD_DIM], q_t,
                  preferred_element_type=F32)
    s_d = jnp.where(kpos <= t_lane, s_d, NEG)
    m_d = jnp.max(s_d, axis=0, keepdims=True)
    p_d = jnp.exp2(s_d - m_d)
    v_d = jnp.concatenate([vst_ref[0, 0, q0 // LANE + j] for j in range(Q_BLK // LANE)], axis=1)
    init = (m_d, jnp.sum(p_d, axis=0, keepdims=True),
            jnp.dot(v_d, p_d.astype(BF16), preferred_element_type=F32))

    bpt = SEL_TILE // SEL_BLK
    vpt = SEL_TILE // LANE

    def scores(kt):
        k0 = pl.multiple_of(kt * SEL_TILE, SEL_TILE)
        return jnp.dot(ksa_ref[0, 0, pl.ds(k0, SEL_TILE), :], qaug_ref[(kt * bpt) // LANE],
                       preferred_element_type=F32)

    def softmax_pv(kt, st, carry):
        m_i, l_i, acc = carry
        m_new = jnp.maximum(m_i, jnp.max(st, axis=0, keepdims=True))
        alpha = jnp.exp2(m_i - m_new)
        pt = jnp.exp2(st - m_new)
        l_new = alpha * l_i + jnp.sum(pt, axis=0, keepdims=True)
        vt = jnp.concatenate([vst_ref[0, 0, kt * vpt + j] for j in range(vpt)], axis=1)
        acc = alpha * acc + jnp.dot(vt, pt.astype(BF16), preferred_element_type=F32)
        return m_new, l_new, acc

    def trip(j, carry):
        st = s_a[...]
        s_b[...] = scores(2 * j + 1)
        carry = softmax_pv(2 * j, st, carry)
        st = s_b[...]
        s_a[...] = scores(2 * j + 2)
        return softmax_pv(2 * j + 1, st, carry)

    n_tiles = (q0 + SEL_TILE - 1) // SEL_TILE
    n_trips = jnp.maximum((n_tiles + 1) // 2, 1)
    s_a[...] = scores(0)
    carry = lax.fori_loop(0, n_trips - 1, trip, init)
    last = 2 * n_trips - 1
    st = s_a[...]
    s_b[...] = scores(last)
    carry = softmax_pv(last - 1, st, carry)
    _, l_s, acc_s = softmax_pv(last, s_b[...], carry)
    o_sel = acc_s * (1.0 / l_s)

    for h in range(GQA):
        cols = slice(h * Q_BLK, (h + 1) * Q_BLK)
        o_t = (gt_ref[0, 0, 0, h:h + 1, :] * o_cmp[:, cols]
               + gt_ref[0, 0, 1, h:h + 1, :] * o_sel[:, cols]
               + gt_ref[0, 0, 2, h:h + 1, :] * o_win[:, cols])
        o_ref[0, :, h * HEAD_DIM:(h + 1) * HEAD_DIM] = o_t.T.astype(o_ref.dtype)


def _window_mask(S):
    wk = WINDOW + Q_BLK
    i = np.arange(wk)[:, None]
    qi = np.arange(Q_BLK)[None, :]
    out = []
    for v in range(WINDOW // Q_BLK + 1):
        t = Q_BLK * v + qi if v < WINDOW // Q_BLK else WINDOW + qi
        out.append(np.where((i <= t) & (i > t - WINDOW), 0.0, NEG))
    return jnp.asarray(np.stack(out), dtype=F32)


def nsa_attention_core(q, kc, vct, ks_aug, kw, vswt, gates_t, ov_t):
    B, _, S, _ = q.shape
    n_c = S // CMP_STRIDE
    n_sel = S // SEL_BLK
    nq = GQA * Q_BLK
    hk = N_KV_HEADS
    wk = WINDOW + Q_BLK
    nv = WINDOW // Q_BLK
    once = pl.Buffered(1)
    return pl.pallas_call(
        _nsa_kernel,
        grid=(B, hk, S // Q_BLK),
        in_specs=[
            pl.BlockSpec((1, GQA, Q_BLK, HEAD_DIM), lambda b, h, i: (b, h, i, 0)),
            pl.BlockSpec((1, 1, n_c, HEAD_DIM), lambda b, h, i: (b, h, 0, 0)),
            pl.BlockSpec((1, 1, HEAD_DIM, n_c), lambda b, h, i: (b, hk + h, 0, 0)),
            pl.BlockSpec((1, 1, S, 2 * HEAD_DIM), lambda b, h, i: (b, h, 0, 0), pipeline_mode=once),
            pl.BlockSpec((1, 1, S // LANE, HEAD_DIM, LANE), lambda b, h, i: (b, h, 0, 0, 0),
                         pipeline_mode=once),
            pl.BlockSpec((1, 1, S, HEAD_DIM), lambda b, h, i: (b, hk + h, 0, 0), pipeline_mode=once),
            pl.BlockSpec((1, 1, S // LANE, HEAD_DIM, LANE), lambda b, h, i: (b, hk + h, 0, 0, 0),
                         pipeline_mode=once),
            pl.BlockSpec((1, 1, 3, GQA, Q_BLK), lambda b, h, i: (b, h, 0, 0, i)),
            pl.BlockSpec((n_sel, n_c), lambda b, h, i: (0, 0)),
            pl.BlockSpec((1, wk, Q_BLK), lambda b, h, i: (jnp.minimum(i, nv), 0, 0)),
        ],
        out_specs=pl.BlockSpec((1, Q_BLK, GQA * HEAD_DIM), lambda b, h, i: (b, i, h)),
        out_shape=jax.ShapeDtypeStruct((B, S, N_HEADS * HEAD_DIM), BF16),
        scratch_shapes=[pltpu.VMEM((max(1, n_sel // LANE), 2 * HEAD_DIM, nq), BF16),
                        pltpu.VMEM((SEL_TILE, nq), F32), pltpu.VMEM((SEL_TILE, nq), F32)],
        compiler_params=_params(("parallel", "parallel", "arbitrary")),
    )(q, kc, vct, ks_aug, vswt, kw, vswt, gates_t, ov_t, _window_mask(S))


def _moe_kernel(nused_ref, bexp_ref, rtok_ref, h_hbm, wg_ref, wu_ref, wd_ref, y_ref,
                x_even, x_odd, wg_b, wu_b, wd_b, sem):
    i = pl.program_id(0)
    n_used = nused_ref[0]
    blk = x_even.shape[0]
    bufs = (x_even, x_odd)

    def row_copy(b, r, slot):
        tok = rtok_ref[b * blk + r]
        return pltpu.make_async_copy(h_hbm.at[pl.ds(tok, 1)], bufs[slot].at[pl.ds(r, 1)],
                                     sem.at[slot])

    def wait_block(slot):
        pltpu.make_async_copy(h_hbm.at[pl.ds(0, blk)], bufs[slot], sem.at[slot]).wait()

    @pl.when(i == 0)
    def _():
        def body(r, c):
            row_copy(0, r, 0).start()
            return c
        lax.fori_loop(0, blk, body, 0, unroll=8)

    @pl.when((i < n_used) & ((i == 0) | (bexp_ref[i] != bexp_ref[jnp.maximum(i - 1, 0)])))
    def _():
        wg_b[...] = wg_ref[0, 0].astype(BF16)
        wu_b[...] = wu_ref[0, 0].astype(BF16)
        wd_b[...] = wd_ref[0, 0].astype(BF16)

    for slot in range(2):
        @pl.when((i < n_used) & (i % 2 == slot))
        def _(slot=slot):
            wait_block(slot)
            for r in range(blk):
                row_copy(i + 1, r, 1 - slot).start()
            x = jnp.concatenate(_unpack_rows(bufs[slot][...]), axis=1).astype(BF16)
            gate = jnp.dot(x, wg_b[...], preferred_element_type=F32)
            up = jnp.dot(x, wu_b[...], preferred_element_type=F32)
            hid = (gate * jax.nn.sigmoid(gate) * up).astype(BF16)
            y_ref[...] = _pack_rows(jnp.dot(hid, wd_b[...], preferred_element_type=F32))

        @pl.when((i == n_used) & (i % 2 == slot))
        def _(slot=slot):
            wait_block(slot)

    @pl.when(i >= n_used)
    def _():
        y_ref[...] = jnp.zeros_like(y_ref)


def moe_experts(xn, n_used, block_exp, row_tok, wg, wu, wd, layer, blk=MOE_BLK):
    n_rows = row_tok.shape[0]
    n_steps = n_rows // blk
    grid_spec = pltpu.PrefetchScalarGridSpec(
        num_scalar_prefetch=3,
        grid=(n_steps,),
        in_specs=[
            pl.BlockSpec(memory_space=pl.ANY),
            pl.BlockSpec((1, 1, D_MODEL, D_EXPERT), lambda i, nu, be, rt: (layer, be[i], 0, 0)),
            pl.BlockSpec((1, 1, D_MODEL, D_EXPERT), lambda i, nu, be, rt: (layer, be[i], 0, 0)),
            pl.BlockSpec((1, 1, D_EXPERT, D_MODEL), lambda i, nu, be, rt: (layer, be[i], 0, 0)),
        ],
        out_specs=pl.BlockSpec((blk, D_MODEL // 2), lambda i, nu, be, rt: (i, 0)),
        scratch_shapes=[pltpu.VMEM((blk, D_MODEL // 2), jnp.uint32),
                        pltpu.VMEM((blk, D_MODEL // 2), jnp.uint32),
                        pltpu.VMEM((D_MODEL, D_EXPERT), BF16), pltpu.VMEM((D_MODEL, D_EXPERT), BF16),
                        pltpu.VMEM((D_EXPERT, D_MODEL), BF16), pltpu.SemaphoreType.DMA((2,))],
    )
    return pl.pallas_call(
        _moe_kernel, grid_spec=grid_spec,
        out_shape=jax.ShapeDtypeStruct((n_rows, D_MODEL // 2), jnp.uint32),
        compiler_params=_params(("arbitrary",)),
    )(n_used, block_exp, row_tok, xn, wg, wu, wd)


def _combine_kernel(pos_ref, h_ref, w_ref, g_ref, y_hbm, o_ref, n_ref, y_even, y_odd, sem):
    i = pl.program_id(0)
    n = pl.num_programs(0)
    tm = h_ref.shape[0]
    bufs = (y_even, y_odd)

    def row_copy(b, j, k, slot):
        row = pos_ref[(b * tm + j) * TOP_K + k]
        return pltpu.make_async_copy(y_hbm.at[pl.ds(row, 1)], bufs[slot].at[pl.ds(k * tm + j, 1)],
                                     sem.at[slot])

    @pl.when(i == 0)
    def _():
        def body(j, c):
            for k in range(TOP_K):
                row_copy(0, j, k, 0).start()
            return c
        lax.fori_loop(0, tm, body, 0, unroll=4)

    def step(slot, prefetch):
        pltpu.make_async_copy(y_hbm.at[pl.ds(0, TOP_K * tm)], bufs[slot], sem.at[slot]).wait()
        if prefetch:
            for j in range(tm):
                for k in range(TOP_K):
                    row_copy(i + 1, j, k, 1 - slot).start()
        half = D_MODEL // 2
        lo, hi = h_ref[:, :half], h_ref[:, half:]
        for k in range(TOP_K):
            y_lo, y_hi = _unpack_rows(bufs[slot][k * tm:(k + 1) * tm, :])
            lo = lo + y_lo * w_ref[:, k:k + 1]
            hi = hi + y_hi * w_ref[:, k:k + 1]
        out = jnp.concatenate([lo, hi], axis=1)
        o_ref[...] = out
        n_ref[...] = _rms(out, g_ref[...]).astype(n_ref.dtype)

    for slot in range(2):
        for prefetch in (True, False):
            cond = (i % 2 == slot) & ((i + 1 < n) if prefetch else (i + 1 == n))
            pl.when(cond)(functools.partial(step, slot, prefetch))


def moe_combine(h, y, pos, weight, g_next, norm_dtype, tm=256):
    T = h.shape[0]
    grid_spec = pltpu.PrefetchScalarGridSpec(
        num_scalar_prefetch=1,
        grid=(T // tm,),
        in_specs=[
            pl.BlockSpec((tm, D_MODEL), lambda i, p: (i, 0)),
            pl.BlockSpec((tm, TOP_K), lambda i, p: (i, 0)),
            pl.BlockSpec((1, D_MODEL), lambda i, p: (0, 0)),
            pl.BlockSpec(memory_space=pl.ANY),
        ],
        out_specs=[pl.BlockSpec((tm, D_MODEL), lambda i, p: (i, 0)),
                   pl.BlockSpec((tm, D_MODEL), lambda i, p: (i, 0))],
        scratch_shapes=[pltpu.VMEM((TOP_K * tm, D_MODEL // 2), jnp.uint32),
                        pltpu.VMEM((TOP_K * tm, D_MODEL // 2), jnp.uint32),
                        pltpu.SemaphoreType.DMA((2,))],
    )
    return pl.pallas_call(
        _combine_kernel, grid_spec=grid_spec,
        out_shape=[jax.ShapeDtypeStruct((T, D_MODEL), F32),
                   jax.ShapeDtypeStruct((T, D_MODEL), norm_dtype)],
        compiler_params=_params(("arbitrary",)),
    )(pos.reshape(-1), h, weight, g_next.reshape(1, D_MODEL), y)


def _dispatch(expert, T, blk):
    onehot = (expert[:, None] == jnp.arange(N_EXPERTS, dtype=jnp.int32)[None, :]).astype(jnp.int32)
    csum = jnp.cumsum(onehot, axis=0)
    rank = jnp.sum(csum * onehot, axis=1) - 1
    counts = csum[-1]
    padded = (counts + blk - 1) // blk * blk
    pad_end = jnp.cumsum(padded)
    pad_start = pad_end - padded
    pos = (jnp.sum(pad_start[None, :] * onehot, axis=1) + rank).astype(jnp.int32)
    n_rows = T * TOP_K + (N_EXPERTS + 1) * blk
    token = jnp.repeat(jnp.arange(T, dtype=jnp.int32), TOP_K)
    row_tok = jnp.zeros((n_rows,), jnp.int32).at[pos].set(token, unique_indices=True)
    n_blocks = n_rows // blk
    starts = jnp.arange(n_blocks, dtype=jnp.int32) * blk
    block_exp = jnp.minimum(jnp.sum((pad_end[None, :] <= starts[:, None]).astype(jnp.int32), axis=1),
                            N_EXPERTS - 1).astype(jnp.int32)
    n_used = (pad_end[-1] // blk).astype(jnp.int32).reshape(1)
    return pos.reshape(T, TOP_K), row_tok, block_exp, n_used


def _moe_layer(h, norm_g, w_group, b_group, w_exp, b_exp, wg, wu, wd, layer, g_next, norm_dtype):
    T = h.shape[0]
    wr = jnp.zeros((D_MODEL, LANE), F32)
    wr = wr.at[:, :N_GROUPS].set(w_group).at[:, N_GROUPS:N_GROUPS + N_EXPERTS].set(w_exp)
    br = jnp.zeros((LANE,), F32).at[:N_GROUPS].set(b_group).at[N_GROUPS:N_GROUPS + N_EXPERTS].set(b_exp)
    e_t, w_t, xn = router(h, norm_g, wr, br.reshape(LANE, 1))
    expert = e_t[:TOP_K].T.reshape(-1)
    weight = w_t[:TOP_K].T
    pos, row_tok, block_exp, n_used = _dispatch(expert, T, MOE_BLK)
    y = moe_experts(xn, n_used, block_exp, row_tok, wg, wu, wd, layer)
    return moe_combine(h, y, pos, weight, g_next, norm_dtype)


def _gmlp_layer(h, hn, w_in, ln_g, ln_b, w_s, b_s, w_out):
    causal = jnp.tril(jnp.ones((GM_CHUNK, GM_CHUNK), w_s.dtype))
    bs_full = jnp.repeat(b_s.T, D_MODEL // GM_GROUPS, axis=1)
    y = gmlp_gate(hn, w_in.astype(BF16), ln_g, ln_b, (w_s * causal).astype(BF16), bs_full)
    return matmul(y, w_out.astype(BF16), out_dtype=F32, resid=h)


def _overlap_t(S):
    n_c = S // CMP_STRIDE
    n_sel = S // SEL_BLK
    ci = np.arange(n_c)[None, :] * CMP_STRIDE
    sj = np.arange(n_sel)[:, None] * SEL_BLK
    ov = (ci < sj + SEL_BLK) & (ci + CMP_LEN > sj) & (np.arange(n_c)[None, :] < n_c - 1)
    return jnp.asarray(ov.astype(np.float32), dtype=BF16)


def _nsa_layer(h, hn, B, S, w_in, ck_pe, ck_w1, ck_w2, cv_pe, cv_w1, cv_w2, w_out):
    qd = N_HEADS * HEAD_DIM
    kvd = N_KV_HEADS * HEAD_DIM
    w = w_in.astype(BF16)
    o = [qd + i * kvd for i in range(7)]
    w_q, w_kc, w_vc, w_ks, w_vs, w_kw, w_vw = (
        w[:, :o[0]], w[:, o[0]:o[1]], w[:, o[1]:o[2]], w[:, o[2]:o[3]], w[:, o[3]:o[4]],
        w[:, o[4]:o[5]], w[:, o[5]:o[6]])
    w_g = jnp.zeros((D_MODEL, LANE), BF16).at[:, :3 * N_HEADS].set(w[:, o[6]:])
    q = matmul(hn, w_q, out_dtype=BF16, scale=HEAD_DIM ** -0.5 * LOG2E, layout="heads", seq=S)
    kvc = matmul(hn, jnp.concatenate([w_kc, w_vc], 1), out_dtype=F32, layout="heads", seq=S)
    ksw = matmul(hn, jnp.concatenate([w_ks, w_kw], 1), out_dtype=BF16, layout="heads", seq=S)
    vswt = matmul(hn, jnp.concatenate([w_vs, w_vw], 1), out_dtype=BF16, layout="heads_t", seq=S)
    gates = matmul(hn, w_g, out_dtype=F32, act="sigmoid")[:, :3 * N_HEADS]
    gates_t = gates.reshape(B, S, N_KV_HEADS, GQA, 3).transpose(0, 2, 4, 3, 1)

    half = CMP_STRIDE
    pe2 = jnp.stack([jnp.stack([pe[:half].reshape(-1), pe[half:].reshape(-1)])
                     for pe in (ck_pe, cv_pe)])
    w1 = jnp.stack([ck_w1, cv_w1]).astype(BF16)
    w2 = jnp.stack([ck_w2, cv_w2]).astype(BF16)
    cmp, cmp_t = compress_kv(kvc, pe2, w1, w2)
    n_c = S // CMP_STRIDE
    cmp = cmp.reshape(B, 2 * N_KV_HEADS, n_c, HEAD_DIM)
    cmp_t = cmp_t.reshape(B, 2 * N_KV_HEADS, HEAD_DIM, n_c)
    blk_id = (np.arange(S) // SEL_BLK) % LANE
    onehot = jnp.asarray(blk_id[:, None] == np.arange(LANE)[None, :], dtype=BF16)
    ks_aug = jnp.concatenate(
        [ksw[:, :N_KV_HEADS], jnp.broadcast_to(onehot, (B, N_KV_HEADS, S, LANE))], axis=-1)
    o_att = nsa_attention_core(q, cmp, cmp_t, ks_aug, ksw, vswt, gates_t, _overlap_t(S))
    return matmul(o_att.reshape(B * S, qd), w_out.astype(BF16), out_dtype=F32, resid=h)


def kernel(x, norm_mix, norm_ffn, norm_final, a_w_in, a_ln_g, a_ln_b, a_w_s, a_b_s, a_w_out,
           b_w_in, b_ck_pe, b_ck_w1, b_ck_w2, b_cv_pe, b_cv_w1, b_cv_w2, b_w_out,
           r_w_group, r_b_group, r_w_exp, r_b_exp, e_w_gate, e_w_up, e_w_down):
    B, S, _ = x.shape
    h = x.reshape(B * S, D_MODEL)
    hn = rmsnorm(h, norm_mix[0])
    h = _gmlp_layer(h, hn, a_w_in[0], a_ln_g[0], a_ln_b[0], a_w_s[0], a_b_s[0], a_w_out[0])
    h, hn = _moe_layer(h, norm_ffn[0], r_w_group[0], r_b_group[0], r_w_exp[0], r_b_exp[0],
                       e_w_gate, e_w_up, e_w_down, 0, norm_mix[1], BF16)
    h = _nsa_layer(h, hn, B, S, b_w_in[0], b_ck_pe[0], b_ck_w1[0], b_ck_w2[0],
                   b_cv_pe[0], b_cv_w1[0], b_cv_w2[0], b_w_out[0])
    _, out = _moe_layer(h, norm_ffn[1], r_w_group[1], r_b_group[1], r_w_exp[1], r_b_exp[1],
                        e_w_gate, e_w_up, e_w_down, 1, norm_final, F32)
    return out.reshape(B, S, D_MODEL)
```

```python
import functools

import numpy as np
import jax
import jax.numpy as jnp
from jax import lax
from jax.experimental import pallas as pl
from jax.experimental.pallas import tpu as pltpu

F32 = jnp.float32
BF16 = jnp.bfloat16

D_MODEL = 2048
LANE = 128
GM_GROUPS = 8
GM_CHUNK = 128
HEAD_DIM = 128
N_HEADS = 16
N_KV_HEADS = 4
GQA = N_HEADS // N_KV_HEADS
CMP_LEN = 32
CMP_STRIDE = 16
SEL_BLK = 64
SEL_TOPN = 16
SEL_LOCAL = 2
WINDOW = 512
Q_BLK = 256
FORCE = 1.0e4
N_GROUPS = 4
EXP_PER_GROUP = 8
N_EXPERTS = N_GROUPS * EXP_PER_GROUP
TOP_K = 2
D_EXPERT = 512
EPS = 1e-6
NEG = -1e30
LOG2E = 1.4426950408889634
ONES_ROWS = 16
SEL_TILE = 512
MOE_BLK = 512
VMEM_LIMIT = 56 * 1024 * 1024

_NT = (((1,), (1,)), ((), ()))


def _params(sem):
    return pltpu.CompilerParams(dimension_semantics=sem, vmem_limit_bytes=VMEM_LIMIT)


def _gelu(x):
    return 0.5 * x * (1.0 + jnp.tanh(0.7978845608028654 * (x + 0.044715 * (x * x * x))))


def _rms(x, g):
    y = x * lax.rsqrt(jnp.mean(x * x, axis=-1, keepdims=True) + EPS)
    return y * g


def _pack_rows(x):
    half = x.shape[1] // 2
    bits = lambda v: lax.bitcast_convert_type(v.astype(BF16).astype(F32), jnp.uint32)
    return (bits(x[:, :half]) >> 16) | (bits(x[:, half:]) & jnp.uint32(0xFFFF0000))


def _unpack_rows(w):
    lo = lax.bitcast_convert_type(w << 16, F32)
    hi = lax.bitcast_convert_type(w & jnp.uint32(0xFFFF0000), F32)
    return lo, hi


def _rmsnorm_kernel(x_ref, g_ref, o_ref):
    o_ref[...] = _rms(x_ref[...], g_ref[...]).astype(o_ref.dtype)


def rmsnorm(x, g, out_dtype=BF16, tm=512):
    T = x.shape[0]
    return pl.pallas_call(
        _rmsnorm_kernel,
        grid=(T // tm,),
        in_specs=[pl.BlockSpec((tm, D_MODEL), lambda i: (i, 0)),
                  pl.BlockSpec((1, D_MODEL), lambda i: (0, 0))],
        out_specs=pl.BlockSpec((tm, D_MODEL), lambda i: (i, 0)),
        out_shape=jax.ShapeDtypeStruct((T, D_MODEL), out_dtype),
        compiler_params=_params(("parallel",)),
    )(x, g.reshape(1, D_MODEL))


def _router_kernel(x_ref, g_ref, wh_ref, wl_ref, br_ref, e_ref, w_ref, xn_ref):
    y = _rms(x_ref[...], g_ref[...])
    xn_ref[...] = _pack_rows(y)
    y_hi = y.astype(BF16)
    y_lo = (y - y_hi.astype(F32)).astype(BF16)
    lg = (jnp.dot(y_hi, wh_ref[...], preferred_element_type=F32)
          + jnp.dot(y_hi, wl_ref[...], preferred_element_type=F32)
          + jnp.dot(y_lo, wh_ref[...], preferred_element_type=F32))
    lt = lg.T + br_ref[...]
    row = lambda i: lt[i:i + 1, :]

    def softmax(rows):
        m = functools.reduce(jnp.maximum, rows)
        e = [jnp.exp(r - m) for r in rows]
        s = functools.reduce(lambda a, b: a + b, e)
        return [v / s for v in e]

    def top1(vals, skip=None):
        best = jnp.full_like(vals[0], -1.0)
        idx = jnp.zeros(vals[0].shape, jnp.int32)
        for k, v in enumerate(vals):
            if skip is not None:
                v = jnp.where(skip == k, -1.0, v)
            upd = v > best
            best = jnp.where(upd, v, best)
            idx = jnp.where(upd, k, idx)
        return best, idx

    g_w, g_idx = top1(softmax([row(j) for j in range(N_GROUPS)]))
    e_logits = []
    for k in range(EXP_PER_GROUP):
        v = row(N_GROUPS + k)
        for gg in range(1, N_GROUPS):
            v = jnp.where(g_idx == gg, row(N_GROUPS + gg * EXP_PER_GROUP + k), v)
        e_logits.append(v)
    e_prob = softmax(e_logits)
    w1, i1 = top1(e_prob)
    w2, i2 = top1(e_prob, skip=i1)
    tot = w1 + w2
    zi = jnp.zeros((6, lt.shape[1]), jnp.int32)
    e_ref[...] = jnp.concatenate([g_idx * EXP_PER_GROUP + i1, g_idx * EXP_PER_GROUP + i2, zi], axis=0)
    w_ref[...] = jnp.concatenate([g_w * (w1 / tot), g_w * (w2 / tot), zi.astype(F32)], axis=0)


def router(x, g, wr, br, tm=512):
    T = x.shape[0]
    wr_hi = wr.astype(BF16)
    wr_lo = (wr - wr_hi.astype(F32)).astype(BF16)
    return pl.pallas_call(
        _router_kernel,
        grid=(T // tm,),
        in_specs=[pl.BlockSpec((tm, D_MODEL), lambda i: (i, 0)),
                  pl.BlockSpec((1, D_MODEL), lambda i: (0, 0)),
                  pl.BlockSpec((D_MODEL, LANE), lambda i: (0, 0)),
                  pl.BlockSpec((D_MODEL, LANE), lambda i: (0, 0)),
                  pl.BlockSpec((LANE, 1), lambda i: (0, 0))],
        out_specs=[pl.BlockSpec((8, tm), lambda i: (0, i)),
                   pl.BlockSpec((8, tm), lambda i: (0, i)),
                   pl.BlockSpec((tm, D_MODEL // 2), lambda i: (i, 0))],
        out_shape=[jax.ShapeDtypeStruct((8, T), jnp.int32),
                   jax.ShapeDtypeStruct((8, T), F32),
                   jax.ShapeDtypeStruct((T, D_MODEL // 2), jnp.uint32)],
        compiler_params=_params(("parallel",)),
    )(x, g.reshape(1, D_MODEL), wr_hi, wr_lo, br)


def _mm_kernel(*refs, act, scale, has_resid, layout):
    a_ref, w_ref = refs[0], refs[1]
    o_ref = refs[-1]
    acc = jnp.dot(a_ref[...], w_ref[...], preferred_element_type=F32)
    if scale is not None:
        acc = acc * scale
    if act == "gelu":
        acc = _gelu(acc)
    elif act == "sigmoid":
        acc = jax.nn.sigmoid(acc)
    if has_resid:
        acc = acc + refs[2][...]
    tm, tn = acc.shape
    if layout == "plain":
        o_ref[...] = acc.astype(o_ref.dtype)
    elif layout == "heads":
        for h in range(tn // HEAD_DIM):
            o_ref[0, h] = acc[:, h * HEAD_DIM:(h + 1) * HEAD_DIM].astype(o_ref.dtype)
    else:
        for h in range(tn // HEAD_DIM):
            for c in range(tm // LANE):
                blk = acc[c * LANE:(c + 1) * LANE, h * HEAD_DIM:(h + 1) * HEAD_DIM]
                o_ref[0, h, c] = blk.T.astype(o_ref.dtype)


def matmul(a, w, *, out_dtype, act=None, scale=None, resid=None, layout="plain",
           seq=None, tm=512, tn=1024):
    M, K = a.shape
    N = w.shape[1]
    tn = min(tn, N)
    grid = (N // tn, M // tm)
    in_specs = [pl.BlockSpec((tm, K), lambda j, i: (i, 0)),
                pl.BlockSpec((K, tn), lambda j, i: (0, j))]
    args = [a, w]
    if resid is not None:
        in_specs.append(pl.BlockSpec((tm, tn), lambda j, i: (i, j)))
        args.append(resid)
    if layout == "plain":
        out_shape = jax.ShapeDtypeStruct((M, N), out_dtype)
        out_spec = pl.BlockSpec((tm, tn), lambda j, i: (i, j))
    else:
        nb = seq // tm
        nh = tn // HEAD_DIM
        if layout == "heads":
            out_shape = jax.ShapeDtypeStruct((M // seq, N // HEAD_DIM, seq, HEAD_DIM), out_dtype)
            out_spec = pl.BlockSpec((1, nh, tm, HEAD_DIM), lambda j, i: (i // nb, j, i % nb, 0))
        else:
            out_shape = jax.ShapeDtypeStruct(
                (M // seq, N // HEAD_DIM, seq // LANE, HEAD_DIM, LANE), out_dtype)
            out_spec = pl.BlockSpec((1, nh, tm // LANE, HEAD_DIM, LANE),
                                    lambda j, i: (i // nb, j, i % nb, 0, 0))
    kern = functools.partial(_mm_kernel, act=act, scale=scale, has_resid=resid is not None,
                             layout=layout)
    return pl.pallas_call(
        kern, grid=grid, in_specs=in_specs, out_specs=out_spec, out_shape=out_shape,
        compiler_params=_params(("parallel", "parallel")),
    )(*args)


def _gate_kernel(a_ref, w_ref, lng_ref, lnb_ref, ws_ref, bs_ref, y_ref):
    tm = a_ref.shape[0]
    z = _gelu(jnp.dot(a_ref[...], w_ref[...], preferred_element_type=F32))
    u = z[:, :D_MODEL]
    v = z[:, D_MODEL:]
    mu = jnp.mean(v, axis=-1, keepdims=True)
    vc = v - mu
    vn = vc * lax.rsqrt(jnp.mean(vc * vc, axis=-1, keepdims=True) + EPS)
    vn = (vn * lng_ref[...] + lnb_ref[...]).astype(BF16)
    gd = D_MODEL // GM_GROUPS
    for c in range(tm // GM_CHUNK):
        rows = slice(c * GM_CHUNK, (c + 1) * GM_CHUNK)
        for g in range(GM_GROUPS):
            cols = slice(g * gd, (g + 1) * gd)
            sv = jnp.dot(ws_ref[g], vn[rows, cols], preferred_element_type=F32)
            y_ref[rows, cols] = (u[rows, cols] * (sv + bs_ref[:, cols])).astype(y_ref.dtype)


def gmlp_gate(a, w_in, ln_g, ln_b, ws_masked, bs_full, tm=512):
    T = a.shape[0]
    return pl.pallas_call(
        _gate_kernel,
        grid=(T // tm,),
        in_specs=[pl.BlockSpec((tm, D_MODEL), lambda i: (i, 0)),
                  pl.BlockSpec((D_MODEL, 2 * D_MODEL), lambda i: (0, 0), pipeline_mode=pl.Buffered(1)),
                  pl.BlockSpec((1, D_MODEL), lambda i: (0, 0)),
                  pl.BlockSpec((1, D_MODEL), lambda i: (0, 0)),
                  pl.BlockSpec((GM_GROUPS, GM_CHUNK, GM_CHUNK), lambda i: (0, 0, 0)),
                  pl.BlockSpec((GM_CHUNK, D_MODEL), lambda i: (0, 0))],
        out_specs=pl.BlockSpec((tm, D_MODEL), lambda i: (i, 0)),
        out_shape=jax.ShapeDtypeStruct((T, D_MODEL), BF16),
        compiler_params=_params(("parallel",)),
    )(a, w_in, ln_g.reshape(1, D_MODEL), ln_b.reshape(1, D_MODEL), ws_masked, bs_full)


def _compress_kernel(x_ref, pe_ref, w1_ref, w2_ref, o_ref, ot_ref):
    x = x_ref[0]
    half = CMP_STRIDE * HEAD_DIM
    xa = (x + pe_ref[0, 0:1, :]).astype(BF16)
    xb = (x + pe_ref[0, 1:2, :]).astype(BF16)
    a = jnp.dot(xa, w1_ref[0, :half, :], preferred_element_type=F32)
    b = jnp.dot(xb, w1_ref[0, half:, :], preferred_element_type=F32)
    n = a.shape[0]
    pre = a + pltpu.roll(b, n - 1, axis=0)
    out = jnp.dot(_gelu(pre).astype(BF16), w2_ref[0], preferred_element_type=F32)
    o_ref[0] = out.astype(o_ref.dtype)
    ot_ref[0] = out.T.astype(ot_ref.dtype)


def compress_kv(kvc, pe2, w1, w2):
    B, H2, S, _ = kvc.shape
    n_grp = S // CMP_STRIDE
    x = kvc.reshape(B * H2, n_grp, CMP_STRIDE * HEAD_DIM)
    sel = lambda i: ((i % H2) // N_KV_HEADS, 0, 0)
    return pl.pallas_call(
        _compress_kernel,
        grid=(B * H2,),
        in_specs=[pl.BlockSpec((1, n_grp, CMP_STRIDE * HEAD_DIM), lambda i: (i, 0, 0)),
                  pl.BlockSpec((1, 2, CMP_STRIDE * HEAD_DIM), sel),
                  pl.BlockSpec((1, CMP_LEN * HEAD_DIM, HEAD_DIM), sel),
                  pl.BlockSpec((1, HEAD_DIM, HEAD_DIM), sel)],
        out_specs=[pl.BlockSpec((1, n_grp, HEAD_DIM), lambda i: (i, 0, 0)),
                   pl.BlockSpec((1, HEAD_DIM, n_grp), lambda i: (i, 0, 0))],
        out_shape=[jax.ShapeDtypeStruct((B * H2, n_grp, HEAD_DIM), BF16),
                   jax.ShapeDtypeStruct((B * H2, HEAD_DIM, n_grp), BF16)],
        compiler_params=_params(("parallel",)),
    )(x, pe2, w1, w2)


def _nsa_kernel(q_ref, kc_ref, vct_ref, ksa_ref, vst_ref, kw_ref, vwt_ref, gt_ref, ovt_ref, wm_ref,
                o_ref, qaug_ref, bsel_ref, s_a, s_b):
    nq = GQA * Q_BLK
    qb = pl.program_id(2)
    q0 = qb * Q_BLK
    q_t = q_ref[0].reshape(nq, HEAD_DIM).astype(F32).T.astype(BF16)
    t_lane = q0 + (lax.broadcasted_iota(jnp.int32, (1, nq), 1) & (Q_BLK - 1))

    n_c = kc_ref.shape[2]

    def compressed(n):
        s = jnp.dot(kc_ref[0, 0, :n, :], q_t, preferred_element_type=F32)
        c_end = lax.broadcasted_iota(jnp.int32, (n, 1), 0) * CMP_STRIDE + (CMP_LEN - 1)
        s = jnp.where(c_end <= t_lane, s, NEG)
        m = jnp.max(s, axis=0, keepdims=True)
        p = jnp.exp2(s - m)
        l = jnp.sum(p, axis=0, keepdims=True)
        p = p * jnp.where(m > 0.5 * NEG, 1.0 / l, 0.0)
        o = jnp.dot(vct_ref[0, 0, :, :n], p.astype(BF16), preferred_element_type=F32)
        psum = p[:, 0:Q_BLK]
        for h in range(1, GQA):
            psum = psum + p[:, h * Q_BLK:(h + 1) * Q_BLK]
        p_hi = psum.astype(BF16)
        p_lo = (psum - p_hi.astype(F32)).astype(BF16)
        ov = ovt_ref[:, :n]
        return o, (jnp.dot(ov, p_hi, preferred_element_type=F32)
                   + jnp.dot(ov, p_lo, preferred_element_type=F32))

    n_bkt = 4 if n_c % (4 * LANE) == 0 else 1
    bkt = n_c // n_bkt
    if n_bkt == 1:
        o_cmp, imp = compressed(n_c)
    else:
        o_cmp, imp = lax.switch((q0 + Q_BLK - 1) // (bkt * CMP_STRIDE),
                                [functools.partial(compressed, bkt * (i + 1)) for i in range(n_bkt)])

    n_sel = imp.shape[0]
    blk = lax.broadcasted_iota(jnp.int32, (n_sel, Q_BLK), 0)
    t_q = q0 + lax.broadcasted_iota(jnp.int32, (n_sel, Q_BLK), 1)
    dist = (t_q // SEL_BLK) - blk
    forced = (blk == 0) | ((dist >= 0) & (dist < SEL_LOCAL))
    score = jnp.where(forced, -jnp.inf, jnp.where(blk * SEL_BLK <= t_q, imp, NEG))
    blk_f = blk.astype(F32)
    for _ in range(SEL_TOPN - 1 - SEL_LOCAL):
        top = jnp.max(score, axis=0, keepdims=True)
        first = jnp.min(jnp.where(score == top, blk_f, float(n_sel)), axis=0, keepdims=True)
        score = jnp.where(blk_f == first, -jnp.inf, score)
    picked = score == -jnp.inf
    bsel_ref[...] = jnp.where(picked, 0.0, NEG)
    bias = jnp.where(picked & (blk * SEL_BLK < q0), 0.0, NEG)

    n_half = qaug_ref.shape[0]
    if n_sel < LANE:
        bias = jnp.concatenate([bias, jnp.zeros((LANE - n_sel, Q_BLK), F32)], axis=0)
    for hf in range(n_half):
        bq = bias[hf * LANE:(hf + 1) * LANE, :].astype(BF16)
        qaug_ref[hf] = jnp.concatenate([q_t, jnp.concatenate([bq] * GQA, axis=1)], axis=0)

    wk = WINDOW + Q_BLK
    kstart = pl.multiple_of(jnp.maximum(q0 - WINDOW, 0), LANE)
    sw = jnp.dot(kw_ref[0, 0, pl.ds(kstart, wk), :], q_t, preferred_element_type=F32)
    sw = sw + jnp.concatenate([wm_ref[0]] * GQA, axis=1)
    m_w = jnp.max(sw, axis=0, keepdims=True)
    p_w = jnp.exp2((sw - m_w).astype(BF16))
    vw = jnp.concatenate([vwt_ref[0, 0, kstart // LANE + j] for j in range(wk // LANE)], axis=1)
    vw = jnp.concatenate([vw, jnp.ones((ONES_ROWS, wk), BF16)], axis=0)
    acc_w = jnp.dot(vw, p_w, preferred_element_type=F32)
    o_win = acc_w[:HEAD_DIM] * (1.0 / acc_w[HEAD_DIM:HEAD_DIM + 1])

    kpos = q0 + lax.broadcasted_iota(jnp.int32, (Q_BLK, 1), 0)
    s_d = jnp.dot(ksa_ref[0, 0, pl.ds(pl.multiple_of(q0, Q_BLK), Q_BLK), :HEAD_DIM], q_t,
                  preferred_element_type=F32)
    parts = []
    for j in range(Q_BLK // SEL_BLK):
        b_row = bsel_ref[pl.ds(q0 // SEL_BLK + j, 1), :]
        parts.append(s_d[j * SEL_BLK:(j + 1) * SEL_BLK, :] + jnp.concatenate([b_row] * GQA, axis=1))
    s_d = jnp.where(kpos <= t_lane, jnp.concatenate(parts, axis=0), NEG)
    m_d = jnp.max(s_d, axis=0, keepdims=True)
    p_d = jnp.exp2(s_d - m_d)
    v_d = jnp.concatenate([vst_ref[0, 0, q0 // LANE + j] for j in range(Q_BLK // LANE)], axis=1)
    init = (m_d, jnp.sum(p_d, axis=0, keepdims=True),
            jnp.dot(v_d, p_d.astype(BF16), preferred_element_type=F32))

    bpt = SEL_TILE // SEL_BLK
    vpt = SEL_TILE // LANE

    def scores(kt):
        k0 = pl.multiple_of(kt * SEL_TILE, SEL_TILE)
        return jnp.dot(ksa_ref[0, 0, pl.ds(k0, SEL_TILE), :], qaug_ref[(kt * bpt) // LANE],
                       preferred_element_type=F32)

    def softmax_pv(kt, st, carry):
        m_i, l_i, acc = carry
        m_new = jnp.maximum(m_i, jnp.max(st, axis=0, keepdims=True))
        alpha = jnp.exp2(m_i - m_new)
        pt = jnp.exp2(st - m_new)
        l_new = alpha * l_i + jnp.sum(pt, axis=0, keepdims=True)
        vt = jnp.concatenate([vst_ref[0, 0, kt * vpt + j] for j in range(vpt)], axis=1)
        acc = alpha * acc + jnp.dot(vt, pt.astype(BF16), preferred_element_type=F32)
        return m_new, l_new, acc

    def trip(j, carry):
        st = s_a[...]
        s_b[...] = scores(2 * j + 1)
        carry = softmax_pv(2 * j, st, carry)
        st = s_b[...]
        s_a[...] = scores(2 * j + 2)
        return softmax_pv(2 * j + 1, st, carry)

    n_tiles = (q0 + SEL_TILE - 1) // SEL_TILE
    n_trips = jnp.maximum((n_tiles + 1) // 2, 1)
    s_a[...] = scores(0)
    carry = lax.fori_loop(0, n_trips - 1, trip, init)
    last = 2 * n_trips - 1
    st = s_a[...]
    s_b[...] = scores(last)
    carry = softmax_pv(last - 1, st, carry)
    _, l_s, acc_s = softmax_pv(last, s_b[...], carry)
    o_sel = acc_s * (1.0 / l_s)

    for h in range(GQA):
        cols = slice(h * Q_BLK, (h + 1) * Q_BLK)
        o_t = (gt_ref[0, 0, 0, h:h + 1, :] * o_cmp[:, cols]
               + gt_ref[0, 0, 1, h:h + 1, :] * o_sel[:, cols]
               + gt_ref[0, 0, 2, h:h + 1, :] * o_win[:, cols])
        o_ref[0, :, h * HEAD_DIM:(h + 1) * HEAD_DIM] = o_t.T.astype(o_ref.dtype)


def _window_mask(S):
    wk = WINDOW + Q_BLK
    i = np.arange(wk)[:, None]
    qi = np.arange(Q_BLK)[None, :]
    out = []
    for v in range(WINDOW // Q_BLK + 1):
        t = Q_BLK * v + qi if v < WINDOW // Q_BLK else WINDOW + qi
        out.append(np.where((i <= t) & (i > t - WINDOW), 0.0, NEG))
    return jnp.asarray(np.stack(out), dtype=F32)


def nsa_attention_core(q, kc, vct, ks_aug, kw, vswt, gates_t, ov_t):
    B, _, S, _ = q.shape
    n_c = S // CMP_STRIDE
    n_sel = S // SEL_BLK
    nq = GQA * Q_BLK
    hk = N_KV_HEADS
    wk = WINDOW + Q_BLK
    nv = WINDOW // Q_BLK
    once = pl.Buffered(1)
    return pl.pallas_call(
        _nsa_kernel,
        grid=(B, hk, S // Q_BLK),
        in_specs=[
            pl.BlockSpec((1, GQA, Q_BLK, HEAD_DIM), lambda b, h, i: (b, h, i, 0)),
            pl.BlockSpec((1, 1, n_c, HEAD_DIM), lambda b, h, i: (b, h, 0, 0)),
            pl.BlockSpec((1, 1, HEAD_DIM, n_c), lambda b, h, i: (b, hk + h, 0, 0)),
            pl.BlockSpec((1, 1, S, 2 * HEAD_DIM), lambda b, h, i: (b, h, 0, 0), pipeline_mode=once),
            pl.BlockSpec((1, 1, S // LANE, HEAD_DIM, LANE), lambda b, h, i: (b, h, 0, 0, 0),
                         pipeline_mode=once),
            pl.BlockSpec((1, 1, S, HEAD_DIM), lambda b, h, i: (b, hk + h, 0, 0), pipeline_mode=once),
            pl.BlockSpec((1, 1, S // LANE, HEAD_DIM, LANE), lambda b, h, i: (b, hk + h, 0, 0, 0),
                         pipeline_mode=once),
            pl.BlockSpec((1, 1, 3, GQA, Q_BLK), lambda b, h, i: (b, h, 0, 0, i)),
            pl.BlockSpec((n_sel, n_c), lambda b, h, i: (0, 0)),
            pl.BlockSpec((1, wk, Q_BLK), lambda b, h, i: (jnp.minimum(i, nv), 0, 0)),
        ],
        out_specs=pl.BlockSpec((1, Q_BLK, GQA * HEAD_DIM), lambda b, h, i: (b, i, h)),
        out_shape=jax.ShapeDtypeStruct((B, S, N_HEADS * HEAD_DIM), BF16),
        scratch_shapes=[pltpu.VMEM((max(1, n_sel // LANE), 2 * HEAD_DIM, nq), BF16),
                        pltpu.VMEM((n_sel, Q_BLK), F32),
                        pltpu.VMEM((SEL_TILE, nq), F32), pltpu.VMEM((SEL_TILE, nq), F32)],
        compiler_params=_params(("parallel", "parallel", "arbitrary")),
    )(q, kc, vct, ks_aug, vswt, kw, vswt, gates_t, ov_t, _window_mask(S))


def _moe_kernel(nused_ref, bexp_ref, rtok_ref, h_hbm, wg_ref, wu_ref, wd_ref, y_ref,
                x_even, x_odd, wg_b, wu_b, wd_b, sem):
    i = pl.program_id(0)
    n_used = nused_ref[0]
    blk = x_even.shape[0]
    bufs = (x_even, x_odd)

    def row_copy(b, r, slot):
        tok = rtok_ref[b * blk + r]
        return pltpu.make_async_copy(h_hbm.at[pl.ds(tok, 1)], bufs[slot].at[pl.ds(r, 1)],
                                     sem.at[slot])

    def wait_block(slot):
        pltpu.make_async_copy(h_hbm.at[pl.ds(0, blk)], bufs[slot], sem.at[slot]).wait()

    @pl.when(i == 0)
    def _():
        def body(r, c):
            row_copy(0, r, 0).start()
            return c
        lax.fori_loop(0, blk, body, 0, unroll=8)

    @pl.when((i < n_used) & ((i == 0) | (bexp_ref[i] != bexp_ref[jnp.maximum(i - 1, 0)])))
    def _():
        wg_b[...] = wg_ref[0, 0].astype(BF16)
        wu_b[...] = wu_ref[0, 0].astype(BF16)
        wd_b[...] = wd_ref[0, 0].astype(BF16)

    for slot in range(2):
        @pl.when((i < n_used) & (i % 2 == slot))
        def _(slot=slot):
            wait_block(slot)
            for r in range(blk):
                row_copy(i + 1, r, 1 - slot).start()
            x = jnp.concatenate(_unpack_rows(bufs[slot][...]), axis=1).astype(BF16)
            gate = jnp.dot(x, wg_b[...], preferred_element_type=F32)
            up = jnp.dot(x, wu_b[...], preferred_element_type=F32)
            hid = (gate * jax.nn.sigmoid(gate) * up).astype(BF16)
            y_ref[...] = _pack_rows(jnp.dot(hid, wd_b[...], preferred_element_type=F32))

        @pl.when((i == n_used) & (i % 2 == slot))
        def _(slot=slot):
            wait_block(slot)

    @pl.when(i >= n_used)
    def _():
        y_ref[...] = jnp.zeros_like(y_ref)


def moe_experts(xn, n_used, block_exp, row_tok, wg, wu, wd, layer, blk=MOE_BLK):
    n_rows = row_tok.shape[0]
    n_steps = n_rows // blk
    grid_spec = pltpu.PrefetchScalarGridSpec(
        num_scalar_prefetch=3,
        grid=(n_steps,),
        in_specs=[
            pl.BlockSpec(memory_space=pl.ANY),
            pl.BlockSpec((1, 1, D_MODEL, D_EXPERT), lambda i, nu, be, rt: (layer, be[i], 0, 0)),
            pl.BlockSpec((1, 1, D_MODEL, D_EXPERT), lambda i, nu, be, rt: (layer, be[i], 0, 0)),
            pl.BlockSpec((1, 1, D_EXPERT, D_MODEL), lambda i, nu, be, rt: (layer, be[i], 0, 0)),
        ],
        out_specs=pl.BlockSpec((blk, D_MODEL // 2), lambda i, nu, be, rt: (i, 0)),
        scratch_shapes=[pltpu.VMEM((blk, D_MODEL // 2), jnp.uint32),
                        pltpu.VMEM((blk, D_MODEL // 2), jnp.uint32),
                        pltpu.VMEM((D_MODEL, D_EXPERT), BF16), pltpu.VMEM((D_MODEL, D_EXPERT), BF16),
                        pltpu.VMEM((D_EXPERT, D_MODEL), BF16), pltpu.SemaphoreType.DMA((2,))],
    )
    return pl.pallas_call(
        _moe_kernel, grid_spec=grid_spec,
        out_shape=jax.ShapeDtypeStruct((n_rows, D_MODEL // 2), jnp.uint32),
        compiler_params=_params(("arbitrary",)),
    )(n_used, block_exp, row_tok, xn, wg, wu, wd)


def _combine_kernel(pos_ref, h_ref, w_ref, g_ref, y_hbm, o_ref, n_ref, y_even, y_odd, sem):
    i = pl.program_id(0)
    n = pl.num_programs(0)
    tm = h_ref.shape[0]
    bufs = (y_even, y_odd)

    def row_copy(b, j, k, slot):
        row = pos_ref[(b * tm + j) * TOP_K + k]
        return pltpu.make_async_copy(y_hbm.at[pl.ds(row, 1)], bufs[slot].at[pl.ds(k * tm + j, 1)],
                                     sem.at[slot])

    @pl.when(i == 0)
    def _():
        def body(j, c):
            for k in range(TOP_K):
                row_copy(0, j, k, 0).start()
            return c
        lax.fori_loop(0, tm, body, 0, unroll=4)

    def step(slot, prefetch):
        pltpu.make_async_copy(y_hbm.at[pl.ds(0, TOP_K * tm)], bufs[slot], sem.at[slot]).wait()
        if prefetch:
            for j in range(tm):
                for k in range(TOP_K):
                    row_copy(i + 1, j, k, 1 - slot).start()
        half = D_MODEL // 2
        lo, hi = h_ref[:, :half], h_ref[:, half:]
        for k in range(TOP_K):
            y_lo, y_hi = _unpack_rows(bufs[slot][k * tm:(k + 1) * tm, :])
            lo = lo + y_lo * w_ref[:, k:k + 1]
            hi = hi + y_hi * w_ref[:, k:k + 1]
        out = jnp.concatenate([lo, hi], axis=1)
        o_ref[...] = out
        n_ref[...] = _rms(out, g_ref[...]).astype(n_ref.dtype)

    for slot in range(2):
        for prefetch in (True, False):
            cond = (i % 2 == slot) & ((i + 1 < n) if prefetch else (i + 1 == n))
            pl.when(cond)(functools.partial(step, slot, prefetch))


def moe_combine(h, y, pos, weight, g_next, norm_dtype, tm=256):
    T = h.shape[0]
    grid_spec = pltpu.PrefetchScalarGridSpec(
        num_scalar_prefetch=1,
        grid=(T // tm,),
        in_specs=[
            pl.BlockSpec((tm, D_MODEL), lambda i, p: (i, 0)),
            pl.BlockSpec((tm, TOP_K), lambda i, p: (i, 0)),
            pl.BlockSpec((1, D_MODEL), lambda i, p: (0, 0)),
            pl.BlockSpec(memory_space=pl.ANY),
        ],
        out_specs=[pl.BlockSpec((tm, D_MODEL), lambda i, p: (i, 0)),
                   pl.BlockSpec((tm, D_MODEL), lambda i, p: (i, 0))],
        scratch_shapes=[pltpu.VMEM((TOP_K * tm, D_MODEL // 2), jnp.uint32),
                        pltpu.VMEM((TOP_K * tm, D_MODEL // 2), jnp.uint32),
                        pltpu.SemaphoreType.DMA((2,))],
    )
    return pl.pallas_call(
        _combine_kernel, grid_spec=grid_spec,
        out_shape=[jax.ShapeDtypeStruct((T, D_MODEL), F32),
                   jax.ShapeDtypeStruct((T, D_MODEL), norm_dtype)],
        compiler_params=_params(("arbitrary",)),
    )(pos.reshape(-1), h, weight, g_next.reshape(1, D_MODEL), y)


def _dispatch(expert, T, blk):
    onehot = (expert[:, None] == jnp.arange(N_EXPERTS, dtype=jnp.int32)[None, :]).astype(jnp.int32)
    csum = jnp.cumsum(onehot, axis=0)
    rank = jnp.sum(csum * onehot, axis=1) - 1
    counts = csum[-1]
    padded = (counts + blk - 1) // blk * blk
    pad_end = jnp.cumsum(padded)
    pad_start = pad_end - padded
    pos = (jnp.sum(pad_start[None, :] * onehot, axis=1) + rank).astype(jnp.int32)
    n_rows = T * TOP_K + (N_EXPERTS + 1) * blk
    token = jnp.repeat(jnp.arange(T, dtype=jnp.int32), TOP_K)
    row_tok = jnp.zeros((n_rows,), jnp.int32).at[pos].set(token, unique_indices=True)
    n_blocks = n_rows // blk
    starts = jnp.arange(n_blocks, dtype=jnp.int32) * blk
    block_exp = jnp.minimum(jnp.sum((pad_end[None, :] <= starts[:, None]).astype(jnp.int32), axis=1),
                            N_EXPERTS - 1).astype(jnp.int32)
    n_used = (pad_end[-1] // blk).astype(jnp.int32).reshape(1)
    return pos.reshape(T, TOP_K), row_tok, block_exp, n_used


def _moe_layer(h, norm_g, w_group, b_group, w_exp, b_exp, wg, wu, wd, layer, g_next, norm_dtype):
    T = h.shape[0]
    wr = jnp.zeros((D_MODEL, LANE), F32)
    wr = wr.at[:, :N_GROUPS].set(w_group).at[:, N_GROUPS:N_GROUPS + N_EXPERTS].set(w_exp)
    br = jnp.zeros((LANE,), F32).at[:N_GROUPS].set(b_group).at[N_GROUPS:N_GROUPS + N_EXPERTS].set(b_exp)
    e_t, w_t, xn = router(h, norm_g, wr, br.reshape(LANE, 1))
    expert = e_t[:TOP_K].T.reshape(-1)
    weight = w_t[:TOP_K].T
    pos, row_tok, block_exp, n_used = _dispatch(expert, T, MOE_BLK)
    y = moe_experts(xn, n_used, block_exp, row_tok, wg, wu, wd, layer)
    return moe_combine(h, y, pos, weight, g_next, norm_dtype)


def _gmlp_layer(h, hn, w_in, ln_g, ln_b, w_s, b_s, w_out):
    causal = jnp.tril(jnp.ones((GM_CHUNK, GM_CHUNK), w_s.dtype))
    bs_full = jnp.repeat(b_s.T, D_MODEL // GM_GROUPS, axis=1)
    y = gmlp_gate(hn, w_in.astype(BF16), ln_g, ln_b, (w_s * causal).astype(BF16), bs_full)
    return matmul(y, w_out.astype(BF16), out_dtype=F32, resid=h)


def _overlap_t(S):
    n_c = S // CMP_STRIDE
    n_sel = S // SEL_BLK
    ci = np.arange(n_c)[None, :] * CMP_STRIDE
    sj = np.arange(n_sel)[:, None] * SEL_BLK
    ov = (ci < sj + SEL_BLK) & (ci + CMP_LEN > sj) & (np.arange(n_c)[None, :] < n_c - 1)
    return jnp.asarray(ov.astype(np.float32), dtype=BF16)


def _nsa_layer(h, hn, B, S, w_in, ck_pe, ck_w1, ck_w2, cv_pe, cv_w1, cv_w2, w_out):
    qd = N_HEADS * HEAD_DIM
    kvd = N_KV_HEADS * HEAD_DIM
    w = w_in.astype(BF16)
    o = [qd + i * kvd for i in range(7)]
    w_q, w_kc, w_vc, w_ks, w_vs, w_kw, w_vw = (
        w[:, :o[0]], w[:, o[0]:o[1]], w[:, o[1]:o[2]], w[:, o[2]:o[3]], w[:, o[3]:o[4]],
        w[:, o[4]:o[5]], w[:, o[5]:o[6]])
    w_g = jnp.zeros((D_MODEL, LANE), BF16).at[:, :3 * N_HEADS].set(w[:, o[6]:])
    q = matmul(hn, w_q, out_dtype=BF16, scale=HEAD_DIM ** -0.5 * LOG2E, layout="heads", seq=S)
    kvc = matmul(hn, jnp.concatenate([w_kc, w_vc], 1), out_dtype=F32, layout="heads", seq=S)
    ksw = matmul(hn, jnp.concatenate([w_ks, w_kw], 1), out_dtype=BF16, layout="heads", seq=S)
    vswt = matmul(hn, jnp.concatenate([w_vs, w_vw], 1), out_dtype=BF16, layout="heads_t", seq=S)
    gates = matmul(hn, w_g, out_dtype=F32, act="sigmoid")[:, :3 * N_HEADS]
    gates_t = gates.reshape(B, S, N_KV_HEADS, GQA, 3).transpose(0, 2, 4, 3, 1)

    half = CMP_STRIDE
    pe2 = jnp.stack([jnp.stack([pe[:half].reshape(-1), pe[half:].reshape(-1)])
                     for pe in (ck_pe, cv_pe)])
    w1 = jnp.stack([ck_w1, cv_w1]).astype(BF16)
    w2 = jnp.stack([ck_w2, cv_w2]).astype(BF16)
    cmp, cmp_t = compress_kv(kvc, pe2, w1, w2)
    n_c = S // CMP_STRIDE
    cmp = cmp.reshape(B, 2 * N_KV_HEADS, n_c, HEAD_DIM)
    cmp_t = cmp_t.reshape(B, 2 * N_KV_HEADS, HEAD_DIM, n_c)
    blk_id = (np.arange(S) // SEL_BLK) % LANE
    onehot = jnp.asarray(blk_id[:, None] == np.arange(LANE)[None, :], dtype=BF16)
    ks_aug = jnp.concatenate(
        [ksw[:, :N_KV_HEADS], jnp.broadcast_to(onehot, (B, N_KV_HEADS, S, LANE))], axis=-1)
    o_att = nsa_attention_core(q, cmp, cmp_t, ks_aug, ksw, vswt, gates_t, _overlap_t(S))
    return matmul(o_att.reshape(B * S, qd), w_out.astype(BF16), out_dtype=F32, resid=h)


def kernel(x, norm_mix, norm_ffn, norm_final, a_w_in, a_ln_g, a_ln_b, a_w_s, a_b_s, a_w_out,
           b_w_in, b_ck_pe, b_ck_w1, b_ck_w2, b_cv_pe, b_cv_w1, b_cv_w2, b_w_out,
           r_w_group, r_b_group, r_w_exp, r_b_exp, e_w_gate, e_w_up, e_w_down):
    B, S, _ = x.shape
    h = x.reshape(B * S, D_MODEL)
    hn = rmsnorm(h, norm_mix[0])
    h = _gmlp_layer(h, hn, a_w_in[0], a_ln_g[0], a_ln_b[0], a_w_s[0], a_b_s[0], a_w_out[0])
    h, hn = _moe_layer(h, norm_ffn[0], r_w_group[0], r_b_group[0], r_w_exp[0], r_b_exp[0],
                       e_w_gate, e_w_up, e_w_down, 0, norm_mix[1], BF16)
    h = _nsa_layer(h, hn, B, S, b_w_in[0], b_ck_pe[0], b_ck_w1[0], b_ck_w2[0],
                   b_cv_pe[0], b_cv_w1[0], b_cv_w2[0], b_w_out[0])
    _, out = _moe_layer(h, norm_ffn[1], r_w_group[1], r_b_group[1], r_w_exp[1], r_b_exp[1],
                        e_w_gate, e_w_up, e_w_down, 1, norm_final, F32)
    return out.reshape(B, S, D_MODEL)
```

```python
import functools

import numpy as np
import jax
import jax.numpy as jnp
from jax import lax
from jax.experimental import pallas as pl
from jax.experimental.pallas import tpu as pltpu

F32 = jnp.float32
BF16 = jnp.bfloat16

D_MODEL = 2048
LANE = 128
GM_GROUPS = 8
GM_CHUNK = 128
HEAD_DIM = 128
N_HEADS = 16
N_KV_HEADS = 4
GQA = N_HEADS // N_KV_HEADS
CMP_LEN = 32
CMP_STRIDE = 16
SEL_BLK = 64
SEL_TOPN = 16
SEL_LOCAL = 2
WINDOW = 512
Q_BLK = 256
FORCE = 1.0e4
N_GROUPS = 4
EXP_PER_GROUP = 8
N_EXPERTS = N_GROUPS * EXP_PER_GROUP
TOP_K = 2
D_EXPERT = 512
EPS = 1e-6
NEG = -1e30
LOG2E = 1.4426950408889634
ONES_ROWS = 16
ROW_TILES = D_MODEL // 2 // LANE
SEL_TILE = 512
MOE_BLK = 512
VMEM_LIMIT = 56 * 1024 * 1024

_NT = (((1,), (1,)), ((), ()))


def _params(sem):
    return pltpu.CompilerParams(dimension_semantics=sem, vmem_limit_bytes=VMEM_LIMIT)


def _gelu(x):
    return 0.5 * x * (1.0 + jnp.tanh(0.7978845608028654 * (x + 0.044715 * (x * x * x))))


def _rms(x, g):
    y = x * lax.rsqrt(jnp.mean(x * x, axis=-1, keepdims=True) + EPS)
    return y * g


def _pack_rows(x):
    half = x.shape[1] // 2
    bits = lambda v: lax.bitcast_convert_type(v.astype(BF16).astype(F32), jnp.uint32)
    return (bits(x[:, :half]) >> 16) | (bits(x[:, half:]) & jnp.uint32(0xFFFF0000))


def _unpack_rows(w):
    lo = lax.bitcast_convert_type(w << 16, F32)
    hi = lax.bitcast_convert_type(w & jnp.uint32(0xFFFF0000), F32)
    return lo, hi


def _store_row_tiles(ref, packed):
    for c in range(packed.shape[1] // LANE):
        ref[:, c, :] = packed[:, c * LANE:(c + 1) * LANE]


def _load_row_tiles(ref, rows=slice(None)):
    return jnp.concatenate([ref[rows, c, :] for c in range(ref.shape[1])], axis=1)


def _rmsnorm_kernel(x_ref, g_ref, o_ref):
    o_ref[...] = _rms(x_ref[...], g_ref[...]).astype(o_ref.dtype)


def rmsnorm(x, g, out_dtype=BF16, tm=512):
    T = x.shape[0]
    return pl.pallas_call(
        _rmsnorm_kernel,
        grid=(T // tm,),
        in_specs=[pl.BlockSpec((tm, D_MODEL), lambda i: (i, 0)),
                  pl.BlockSpec((1, D_MODEL), lambda i: (0, 0))],
        out_specs=pl.BlockSpec((tm, D_MODEL), lambda i: (i, 0)),
        out_shape=jax.ShapeDtypeStruct((T, D_MODEL), out_dtype),
        compiler_params=_params(("parallel",)),
    )(x, g.reshape(1, D_MODEL))


def _router_kernel(x_ref, g_ref, wh_ref, wl_ref, br_ref, e_ref, w_ref, xn_ref):
    y = _rms(x_ref[...], g_ref[...])
    _store_row_tiles(xn_ref, _pack_rows(y))
    y_hi = y.astype(BF16)
    y_lo = (y - y_hi.astype(F32)).astype(BF16)
    lg = (jnp.dot(y_hi, wh_ref[...], preferred_element_type=F32)
          + jnp.dot(y_hi, wl_ref[...], preferred_element_type=F32)
          + jnp.dot(y_lo, wh_ref[...], preferred_element_type=F32))
    lt = lg.T + br_ref[...]
    row = lambda i: lt[i:i + 1, :]

    def softmax(rows):
        m = functools.reduce(jnp.maximum, rows)
        e = [jnp.exp(r - m) for r in rows]
        s = functools.reduce(lambda a, b: a + b, e)
        return [v / s for v in e]

    def top1(vals, skip=None):
        best = jnp.full_like(vals[0], -1.0)
        idx = jnp.zeros(vals[0].shape, jnp.int32)
        for k, v in enumerate(vals):
            if skip is not None:
                v = jnp.where(skip == k, -1.0, v)
            upd = v > best
            best = jnp.where(upd, v, best)
            idx = jnp.where(upd, k, idx)
        return best, idx

    g_w, g_idx = top1(softmax([row(j) for j in range(N_GROUPS)]))
    e_logits = []
    for k in range(EXP_PER_GROUP):
        v = row(N_GROUPS + k)
        for gg in range(1, N_GROUPS):
            v = jnp.where(g_idx == gg, row(N_GROUPS + gg * EXP_PER_GROUP + k), v)
        e_logits.append(v)
    e_prob = softmax(e_logits)
    w1, i1 = top1(e_prob)
    w2, i2 = top1(e_prob, skip=i1)
    tot = w1 + w2
    zi = jnp.zeros((6, lt.shape[1]), jnp.int32)
    e_ref[...] = jnp.concatenate([g_idx * EXP_PER_GROUP + i1, g_idx * EXP_PER_GROUP + i2, zi], axis=0)
    w_ref[...] = jnp.concatenate([g_w * (w1 / tot), g_w * (w2 / tot), zi.astype(F32)], axis=0)


def router(x, g, wr, br, tm=512):
    T = x.shape[0]
    wr_hi = wr.astype(BF16)
    wr_lo = (wr - wr_hi.astype(F32)).astype(BF16)
    return pl.pallas_call(
        _router_kernel,
        grid=(T // tm,),
        in_specs=[pl.BlockSpec((tm, D_MODEL), lambda i: (i, 0)),
                  pl.BlockSpec((1, D_MODEL), lambda i: (0, 0)),
                  pl.BlockSpec((D_MODEL, LANE), lambda i: (0, 0)),
                  pl.BlockSpec((D_MODEL, LANE), lambda i: (0, 0)),
                  pl.BlockSpec((LANE, 1), lambda i: (0, 0))],
        out_specs=[pl.BlockSpec((8, tm), lambda i: (0, i)),
                   pl.BlockSpec((8, tm), lambda i: (0, i)),
                   pl.BlockSpec((tm, ROW_TILES, LANE), lambda i: (i, 0, 0))],
        out_shape=[jax.ShapeDtypeStruct((8, T), jnp.int32),
                   jax.ShapeDtypeStruct((8, T), F32),
                   jax.ShapeDtypeStruct((T, ROW_TILES, LANE), jnp.uint32)],
        compiler_params=_params(("parallel",)),
    )(x, g.reshape(1, D_MODEL), wr_hi, wr_lo, br)


def _mm_kernel(*refs, act, scale, has_resid, layout):
    a_ref, w_ref = refs[0], refs[1]
    o_ref = refs[-1]
    acc = jnp.dot(a_ref[...], w_ref[...], preferred_element_type=F32)
    if scale is not None:
        acc = acc * scale
    if act == "gelu":
        acc = _gelu(acc)
    elif act == "sigmoid":
        acc = jax.nn.sigmoid(acc)
    if has_resid:
        acc = acc + refs[2][...]
    tm, tn = acc.shape
    if layout == "plain":
        o_ref[...] = acc.astype(o_ref.dtype)
    elif layout == "heads":
        for h in range(tn // HEAD_DIM):
            o_ref[0, h] = acc[:, h * HEAD_DIM:(h + 1) * HEAD_DIM].astype(o_ref.dtype)
    else:
        for h in range(tn // HEAD_DIM):
            for c in range(tm // LANE):
                blk = acc[c * LANE:(c + 1) * LANE, h * HEAD_DIM:(h + 1) * HEAD_DIM]
                o_ref[0, h, c] = blk.T.astype(o_ref.dtype)


def matmul(a, w, *, out_dtype, act=None, scale=None, resid=None, layout="plain",
           seq=None, tm=512, tn=1024):
    M, K = a.shape
    N = w.shape[1]
    tn = min(tn, N)
    grid = (N // tn, M // tm)
    in_specs = [pl.BlockSpec((tm, K), lambda j, i: (i, 0)),
                pl.BlockSpec((K, tn), lambda j, i: (0, j))]
    args = [a, w]
    if resid is not None:
        in_specs.append(pl.BlockSpec((tm, tn), lambda j, i: (i, j)))
        args.append(resid)
    if layout == "plain":
        out_shape = jax.ShapeDtypeStruct((M, N), out_dtype)
        out_spec = pl.BlockSpec((tm, tn), lambda j, i: (i, j))
    else:
        nb = seq // tm
        nh = tn // HEAD_DIM
        if layout == "heads":
            out_shape = jax.ShapeDtypeStruct((M // seq, N // HEAD_DIM, seq, HEAD_DIM), out_dtype)
            out_spec = pl.BlockSpec((1, nh, tm, HEAD_DIM), lambda j, i: (i // nb, j, i % nb, 0))
        else:
            out_shape = jax.ShapeDtypeStruct(
                (M // seq, N // HEAD_DIM, seq // LANE, HEAD_DIM, LANE), out_dtype)
            out_spec = pl.BlockSpec((1, nh, tm // LANE, HEAD_DIM, LANE),
                                    lambda j, i: (i // nb, j, i % nb, 0, 0))
    kern = functools.partial(_mm_kernel, act=act, scale=scale, has_resid=resid is not None,
                             layout=layout)
    return pl.pallas_call(
        kern, grid=grid, in_specs=in_specs, out_specs=out_spec, out_shape=out_shape,
        compiler_params=_params(("parallel", "parallel")),
    )(*args)


def _gate_kernel(x_ref, g_ref, w_ref, lng_ref, lnb_ref, ws_ref, bs_ref, y_ref):
    tm = x_ref.shape[0]
    a = _rms(x_ref[...], g_ref[...]).astype(BF16)
    z = _gelu(jnp.dot(a, w_ref[...], preferred_element_type=F32))
    u = z[:, :D_MODEL]
    v = z[:, D_MODEL:]
    mu = jnp.mean(v, axis=-1, keepdims=True)
    vc = v - mu
    vn = vc * lax.rsqrt(jnp.mean(vc * vc, axis=-1, keepdims=True) + EPS)
    vn = (vn * lng_ref[...] + lnb_ref[...]).astype(BF16)
    gd = D_MODEL // GM_GROUPS
    for c in range(tm // GM_CHUNK):
        rows = slice(c * GM_CHUNK, (c + 1) * GM_CHUNK)
        for g in range(GM_GROUPS):
            cols = slice(g * gd, (g + 1) * gd)
            sv = jnp.dot(ws_ref[g], vn[rows, cols], preferred_element_type=F32)
            y_ref[rows, cols] = (u[rows, cols] * (sv + bs_ref[:, cols])).astype(y_ref.dtype)


def gmlp_gate(x, g, w_in, ln_g, ln_b, ws_masked, bs_full, tm=512):
    T = x.shape[0]
    return pl.pallas_call(
        _gate_kernel,
        grid=(T // tm,),
        in_specs=[pl.BlockSpec((tm, D_MODEL), lambda i: (i, 0)),
                  pl.BlockSpec((1, D_MODEL), lambda i: (0, 0)),
                  pl.BlockSpec((D_MODEL, 2 * D_MODEL), lambda i: (0, 0), pipeline_mode=pl.Buffered(1)),
                  pl.BlockSpec((1, D_MODEL), lambda i: (0, 0)),
                  pl.BlockSpec((1, D_MODEL), lambda i: (0, 0)),
                  pl.BlockSpec((GM_GROUPS, GM_CHUNK, GM_CHUNK), lambda i: (0, 0, 0)),
                  pl.BlockSpec((GM_CHUNK, D_MODEL), lambda i: (0, 0))],
        out_specs=pl.BlockSpec((tm, D_MODEL), lambda i: (i, 0)),
        out_shape=jax.ShapeDtypeStruct((T, D_MODEL), BF16),
        compiler_params=_params(("parallel",)),
    )(x, g.reshape(1, D_MODEL), w_in, ln_g.reshape(1, D_MODEL), ln_b.reshape(1, D_MODEL), ws_masked,
      bs_full)


def _compress_kernel(x_ref, pe_ref, w1_ref, w2_ref, o_ref, ot_ref):
    x = x_ref[0]
    half = CMP_STRIDE * HEAD_DIM
    xa = (x + pe_ref[0, 0:1, :]).astype(BF16)
    xb = (x + pe_ref[0, 1:2, :]).astype(BF16)
    a = jnp.dot(xa, w1_ref[0, :half, :], preferred_element_type=F32)
    b = jnp.dot(xb, w1_ref[0, half:, :], preferred_element_type=F32)
    n = a.shape[0]
    pre = a + pltpu.roll(b, n - 1, axis=0)
    out = jnp.dot(_gelu(pre).astype(BF16), w2_ref[0], preferred_element_type=F32)
    o_ref[0] = out.astype(o_ref.dtype)
    ot_ref[0] = out.T.astype(ot_ref.dtype)


def compress_kv(kvc, pe2, w1, w2):
    B, H2, S, _ = kvc.shape
    n_grp = S // CMP_STRIDE
    x = kvc.reshape(B * H2, n_grp, CMP_STRIDE * HEAD_DIM)
    sel = lambda i: ((i % H2) // N_KV_HEADS, 0, 0)
    return pl.pallas_call(
        _compress_kernel,
        grid=(B * H2,),
        in_specs=[pl.BlockSpec((1, n_grp, CMP_STRIDE * HEAD_DIM), lambda i: (i, 0, 0)),
                  pl.BlockSpec((1, 2, CMP_STRIDE * HEAD_DIM), sel),
                  pl.BlockSpec((1, CMP_LEN * HEAD_DIM, HEAD_DIM), sel),
                  pl.BlockSpec((1, HEAD_DIM, HEAD_DIM), sel)],
        out_specs=[pl.BlockSpec((1, n_grp, HEAD_DIM), lambda i: (i, 0, 0)),
                   pl.BlockSpec((1, HEAD_DIM, n_grp), lambda i: (i, 0, 0))],
        out_shape=[jax.ShapeDtypeStruct((B * H2, n_grp, HEAD_DIM), BF16),
                   jax.ShapeDtypeStruct((B * H2, HEAD_DIM, n_grp), BF16)],
        compiler_params=_params(("parallel",)),
    )(x, pe2, w1, w2)


def _nsa_kernel(q_ref, kc_ref, vct_ref, ksa_ref, vst_ref, kw_ref, vwt_ref, gt_ref, ovt_ref, wm_ref,
                o_ref, qaug_ref, bsel_ref, s_a, s_b):
    nq = GQA * Q_BLK
    qb = pl.program_id(2)
    q0 = qb * Q_BLK
    q_t = q_ref[0].reshape(nq, HEAD_DIM).astype(F32).T.astype(BF16)
    t_lane = q0 + (lax.broadcasted_iota(jnp.int32, (1, nq), 1) & (Q_BLK - 1))

    n_c = kc_ref.shape[2]

    def compressed(n):
        s = jnp.dot(kc_ref[0, 0, :n, :], q_t, preferred_element_type=F32)
        c_end = lax.broadcasted_iota(jnp.int32, (n, 1), 0) * CMP_STRIDE + (CMP_LEN - 1)
        s = jnp.where(c_end <= t_lane, s, NEG)
        m = jnp.max(s, axis=0, keepdims=True)
        p = jnp.exp2(s - m)
        l = jnp.sum(p, axis=0, keepdims=True)
        p = p * jnp.where(m > 0.5 * NEG, 1.0 / l, 0.0)
        o = jnp.dot(vct_ref[0, 0, :, :n], p.astype(BF16), preferred_element_type=F32)
        psum = p[:, 0:Q_BLK]
        for h in range(1, GQA):
            psum = psum + p[:, h * Q_BLK:(h + 1) * Q_BLK]
        p_hi = psum.astype(BF16)
        p_lo = (psum - p_hi.astype(F32)).astype(BF16)
        ov = ovt_ref[:, :n]
        return o, (jnp.dot(ov, p_hi, preferred_element_type=F32)
                   + jnp.dot(ov, p_lo, preferred_element_type=F32))

    n_bkt = 4 if n_c % (4 * LANE) == 0 else 1
    bkt = n_c // n_bkt
    if n_bkt == 1:
        o_cmp, imp = compressed(n_c)
    else:
        o_cmp, imp = lax.switch((q0 + Q_BLK - 1) // (bkt * CMP_STRIDE),
                                [functools.partial(compressed, bkt * (i + 1)) for i in range(n_bkt)])

    n_sel = imp.shape[0]
    blk = lax.broadcasted_iota(jnp.int32, (n_sel, Q_BLK), 0)
    t_q = q0 + lax.broadcasted_iota(jnp.int32, (n_sel, Q_BLK), 1)
    dist = (t_q // SEL_BLK) - blk
    forced = (blk == 0) | ((dist >= 0) & (dist < SEL_LOCAL))
    score = jnp.where(forced, -jnp.inf, jnp.where(blk * SEL_BLK <= t_q, imp, NEG))
    blk_f = blk.astype(F32)
    for _ in range(SEL_TOPN - 1 - SEL_LOCAL):
        top = jnp.max(score, axis=0, keepdims=True)
        first = jnp.min(jnp.where(score == top, blk_f, float(n_sel)), axis=0, keepdims=True)
        score = jnp.where(blk_f == first, -jnp.inf, score)
    picked = score == -jnp.inf
    bsel_ref[...] = jnp.where(picked, 0.0, NEG)
    bias = jnp.where(picked & (blk * SEL_BLK < q0), 0.0, NEG)

    n_half = qaug_ref.shape[0]
    if n_sel < LANE:
        bias = jnp.concatenate([bias, jnp.zeros((LANE - n_sel, Q_BLK), F32)], axis=0)
    for hf in range(n_half):
        bq = bias[hf * LANE:(hf + 1) * LANE, :].astype(BF16)
        qaug_ref[hf] = jnp.concatenate([q_t, jnp.concatenate([bq] * GQA, axis=1)], axis=0)

    wk = WINDOW + Q_BLK
    kstart = pl.multiple_of(jnp.maximum(q0 - WINDOW, 0), LANE)
    sw = jnp.dot(kw_ref[0, 0, pl.ds(kstart, wk), :], q_t, preferred_element_type=F32)
    sw = sw + jnp.concatenate([wm_ref[0]] * GQA, axis=1)
    m_w = jnp.max(sw, axis=0, keepdims=True)
    p_w = jnp.exp2((sw - m_w).astype(BF16))
    vw = jnp.concatenate([vwt_ref[0, 0, kstart // LANE + j] for j in range(wk // LANE)], axis=1)
    vw = jnp.concatenate([vw, jnp.ones((ONES_ROWS, wk), BF16)], axis=0)
    acc_w = jnp.dot(vw, p_w, preferred_element_type=F32)
    o_win = acc_w[:HEAD_DIM] * (1.0 / acc_w[HEAD_DIM:HEAD_DIM + 1])

    kpos = q0 + lax.broadcasted_iota(jnp.int32, (Q_BLK, 1), 0)
    s_d = jnp.dot(ksa_ref[0, 0, pl.ds(pl.multiple_of(q0, Q_BLK), Q_BLK), :HEAD_DIM], q_t,
                  preferred_element_type=F32)
    parts = []
    for j in range(Q_BLK // SEL_BLK):
        b_row = bsel_ref[pl.ds(q0 // SEL_BLK + j, 1), :]
        parts.append(s_d[j * SEL_BLK:(j + 1) * SEL_BLK, :] + jnp.concatenate([b_row] * GQA, axis=1))
    s_d = jnp.where(kpos <= t_lane, jnp.concatenate(parts, axis=0), NEG)
    m_d = jnp.max(s_d, axis=0, keepdims=True)
    p_d = jnp.exp2(s_d - m_d)
    v_d = jnp.concatenate([vst_ref[0, 0, q0 // LANE + j] for j in range(Q_BLK // LANE)], axis=1)
    init = (m_d, jnp.sum(p_d, axis=0, keepdims=True),
            jnp.dot(v_d, p_d.astype(BF16), preferred_element_type=F32))

    bpt = SEL_TILE // SEL_BLK
    vpt = SEL_TILE // LANE

    def scores(kt):
        k0 = pl.multiple_of(kt * SEL_TILE, SEL_TILE)
        return jnp.dot(ksa_ref[0, 0, pl.ds(k0, SEL_TILE), :], qaug_ref[(kt * bpt) // LANE],
                       preferred_element_type=F32)

    def softmax_pv(kt, st, carry):
        m_i, l_i, acc = carry
        m_new = jnp.maximum(m_i, jnp.max(st, axis=0, keepdims=True))
        alpha = jnp.exp2(m_i - m_new)
        pt = jnp.exp2(st - m_new)
        l_new = alpha * l_i + jnp.sum(pt, axis=0, keepdims=True)
        vt = jnp.concatenate([vst_ref[0, 0, kt * vpt + j] for j in range(vpt)], axis=1)
        acc = alpha * acc + jnp.dot(vt, pt.astype(BF16), preferred_element_type=F32)
        return m_new, l_new, acc

    def trip(j, carry):
        st = s_a[...]
        s_b[...] = scores(2 * j + 1)
        carry = softmax_pv(2 * j, st, carry)
        st = s_b[...]
        s_a[...] = scores(2 * j + 2)
        return softmax_pv(2 * j + 1, st, carry)

    n_tiles = (q0 + SEL_TILE - 1) // SEL_TILE
    n_trips = jnp.maximum((n_tiles + 1) // 2, 1)
    s_a[...] = scores(0)
    carry = lax.fori_loop(0, n_trips - 1, trip, init)
    last = 2 * n_trips - 1
    st = s_a[...]
    s_b[...] = scores(last)
    carry = softmax_pv(last - 1, st, carry)
    _, l_s, acc_s = softmax_pv(last, s_b[...], carry)
    o_sel = acc_s * (1.0 / l_s)

    for h in range(GQA):
        cols = slice(h * Q_BLK, (h + 1) * Q_BLK)
        o_t = (gt_ref[0, 0, 0, h:h + 1, :] * o_cmp[:, cols]
               + gt_ref[0, 0, 1, h:h + 1, :] * o_sel[:, cols]
               + gt_ref[0, 0, 2, h:h + 1, :] * o_win[:, cols])
        o_ref[0, :, h * HEAD_DIM:(h + 1) * HEAD_DIM] = o_t.T.astype(o_ref.dtype)


def _window_mask(S):
    wk = WINDOW + Q_BLK
    i = np.arange(wk)[:, None]
    qi = np.arange(Q_BLK)[None, :]
    out = []
    for v in range(WINDOW // Q_BLK + 1):
        t = Q_BLK * v + qi if v < WINDOW // Q_BLK else WINDOW + qi
        out.append(np.where((i <= t) & (i > t - WINDOW), 0.0, NEG))
    return jnp.asarray(np.stack(out), dtype=F32)


def nsa_attention_core(q, kc, vct, ks_aug, kw, vswt, gates_t, ov_t):
    B, _, S, _ = q.shape
    n_c = S // CMP_STRIDE
    n_sel = S // SEL_BLK
    nq = GQA * Q_BLK
    hk = N_KV_HEADS
    wk = WINDOW + Q_BLK
    nv = WINDOW // Q_BLK
    once = pl.Buffered(1)
    return pl.pallas_call(
        _nsa_kernel,
        grid=(B, hk, S // Q_BLK),
        in_specs=[
            pl.BlockSpec((1, GQA, Q_BLK, HEAD_DIM), lambda b, h, i: (b, h, i, 0)),
            pl.BlockSpec((1, 1, n_c, HEAD_DIM), lambda b, h, i: (b, h, 0, 0)),
            pl.BlockSpec((1, 1, HEAD_DIM, n_c), lambda b, h, i: (b, hk + h, 0, 0)),
            pl.BlockSpec((1, 1, S, 2 * HEAD_DIM), lambda b, h, i: (b, h, 0, 0), pipeline_mode=once),
            pl.BlockSpec((1, 1, S // LANE, HEAD_DIM, LANE), lambda b, h, i: (b, h, 0, 0, 0),
                         pipeline_mode=once),
            pl.BlockSpec((1, 1, S, HEAD_DIM), lambda b, h, i: (b, hk + h, 0, 0), pipeline_mode=once),
            pl.BlockSpec((1, 1, S // LANE, HEAD_DIM, LANE), lambda b, h, i: (b, hk + h, 0, 0, 0),
                         pipeline_mode=once),
            pl.BlockSpec((1, 1, 3, GQA, Q_BLK), lambda b, h, i: (b, h, 0, 0, i)),
            pl.BlockSpec((n_sel, n_c), lambda b, h, i: (0, 0)),
            pl.BlockSpec((1, wk, Q_BLK), lambda b, h, i: (jnp.minimum(i, nv), 0, 0)),
        ],
        out_specs=pl.BlockSpec((1, Q_BLK, GQA * HEAD_DIM), lambda b, h, i: (b, i, h)),
        out_shape=jax.ShapeDtypeStruct((B, S, N_HEADS * HEAD_DIM), BF16),
        scratch_shapes=[pltpu.VMEM((max(1, n_sel // LANE), 2 * HEAD_DIM, nq), BF16),
                        pltpu.VMEM((n_sel, Q_BLK), F32),
                        pltpu.VMEM((SEL_TILE, nq), F32), pltpu.VMEM((SEL_TILE, nq), F32)],
        compiler_params=_params(("parallel", "parallel", "arbitrary")),
    )(q, kc, vct, ks_aug, vswt, kw, vswt, gates_t, ov_t, _window_mask(S))


def _moe_kernel(nused_ref, bexp_ref, rtok_ref, h_hbm, wg_ref, wu_ref, wd_ref, y_ref,
                x_even, x_odd, wg_b, wu_b, wd_b, sem):
    i = pl.program_id(0)
    n_used = nused_ref[0]
    blk = x_even.shape[0]
    bufs = (x_even, x_odd)

    def row_copy(b, r, slot):
        tok = rtok_ref[b * blk + r]
        return pltpu.make_async_copy(h_hbm.at[pl.ds(tok, 1)], bufs[slot].at[pl.ds(r, 1)],
                                     sem.at[slot])

    def wait_block(slot):
        pltpu.make_async_copy(h_hbm.at[pl.ds(0, blk)], bufs[slot], sem.at[slot]).wait()

    @pl.when(i == 0)
    def _():
        def body(r, c):
            row_copy(0, r, 0).start()
            return c
        lax.fori_loop(0, blk, body, 0, unroll=8)

    @pl.when((i < n_used) & ((i == 0) | (bexp_ref[i] != bexp_ref[jnp.maximum(i - 1, 0)])))
    def _():
        wg_b[...] = wg_ref[0, 0].astype(BF16)
        wu_b[...] = wu_ref[0, 0].astype(BF16)
        wd_b[...] = wd_ref[0, 0].astype(BF16)

    for slot in range(2):
        @pl.when((i < n_used) & (i % 2 == slot))
        def _(slot=slot):
            wait_block(slot)
            for r in range(blk):
                row_copy(i + 1, r, 1 - slot).start()
            x = jnp.concatenate(_unpack_rows(_load_row_tiles(bufs[slot])), axis=1).astype(BF16)
            gate = jnp.dot(x, wg_b[...], preferred_element_type=F32)
            up = jnp.dot(x, wu_b[...], preferred_element_type=F32)
            hid = (gate * jax.nn.sigmoid(gate) * up).astype(BF16)
            _store_row_tiles(y_ref, _pack_rows(jnp.dot(hid, wd_b[...], preferred_element_type=F32)))

        @pl.when((i == n_used) & (i % 2 == slot))
        def _(slot=slot):
            wait_block(slot)

    @pl.when(i >= n_used)
    def _():
        y_ref[...] = jnp.zeros_like(y_ref)


def moe_experts(xn, n_used, block_exp, row_tok, wg, wu, wd, layer, blk=MOE_BLK):
    n_rows = row_tok.shape[0]
    n_steps = n_rows // blk
    grid_spec = pltpu.PrefetchScalarGridSpec(
        num_scalar_prefetch=3,
        grid=(n_steps,),
        in_specs=[
            pl.BlockSpec(memory_space=pl.ANY),
            pl.BlockSpec((1, 1, D_MODEL, D_EXPERT), lambda i, nu, be, rt: (layer, be[i], 0, 0)),
            pl.BlockSpec((1, 1, D_MODEL, D_EXPERT), lambda i, nu, be, rt: (layer, be[i], 0, 0)),
            pl.BlockSpec((1, 1, D_EXPERT, D_MODEL), lambda i, nu, be, rt: (layer, be[i], 0, 0)),
        ],
        out_specs=pl.BlockSpec((blk, ROW_TILES, LANE), lambda i, nu, be, rt: (i, 0, 0)),
        scratch_shapes=[pltpu.VMEM((blk, ROW_TILES, LANE), jnp.uint32),
                        pltpu.VMEM((blk, ROW_TILES, LANE), jnp.uint32),
                        pltpu.VMEM((D_MODEL, D_EXPERT), BF16), pltpu.VMEM((D_MODEL, D_EXPERT), BF16),
                        pltpu.VMEM((D_EXPERT, D_MODEL), BF16), pltpu.SemaphoreType.DMA((2,))],
    )
    return pl.pallas_call(
        _moe_kernel, grid_spec=grid_spec,
        out_shape=jax.ShapeDtypeStruct((n_rows, ROW_TILES, LANE), jnp.uint32),
        compiler_params=_params(("arbitrary",)),
    )(n_used, block_exp, row_tok, xn, wg, wu, wd)


def _combine_kernel(pos_ref, h_ref, w_ref, g_ref, y_hbm, *rest):
    *o_refs, n_ref, y_even, y_odd, sem = rest
    i = pl.program_id(0)
    n = pl.num_programs(0)
    tm = h_ref.shape[0]
    bufs = (y_even, y_odd)

    def row_copy(b, j, k, slot):
        row = pos_ref[(b * tm + j) * TOP_K + k]
        return pltpu.make_async_copy(y_hbm.at[pl.ds(row, 1)], bufs[slot].at[pl.ds(k * tm + j, 1)],
                                     sem.at[slot])

    @pl.when(i == 0)
    def _():
        def body(j, c):
            for k in range(TOP_K):
                row_copy(0, j, k, 0).start()
            return c
        lax.fori_loop(0, tm, body, 0, unroll=4)

    def step(slot, prefetch):
        pltpu.make_async_copy(y_hbm.at[pl.ds(0, TOP_K * tm)], bufs[slot], sem.at[slot]).wait()
        if prefetch:
            for j in range(tm):
                for k in range(TOP_K):
                    row_copy(i + 1, j, k, 1 - slot).start()
        half = D_MODEL // 2
        lo, hi = h_ref[:, :half], h_ref[:, half:]
        for k in range(TOP_K):
            y_lo, y_hi = _unpack_rows(_load_row_tiles(bufs[slot], slice(k * tm, (k + 1) * tm)))
            lo = lo + y_lo * w_ref[:, k:k + 1]
            hi = hi + y_hi * w_ref[:, k:k + 1]
        out = jnp.concatenate([lo, hi], axis=1)
        for o_ref in o_refs:
            o_ref[...] = out
        n_ref[...] = _rms(out, g_ref[...]).astype(n_ref.dtype)

    for slot in range(2):
        for prefetch in (True, False):
            cond = (i % 2 == slot) & ((i + 1 < n) if prefetch else (i + 1 == n))
            pl.when(cond)(functools.partial(step, slot, prefetch))


def moe_combine(h, y, pos, weight, g_next, norm_dtype, keep_sum, tm=256):
    T = h.shape[0]
    n_out = 2 if keep_sum else 1
    grid_spec = pltpu.PrefetchScalarGridSpec(
        num_scalar_prefetch=1,
        grid=(T // tm,),
        in_specs=[
            pl.BlockSpec((tm, D_MODEL), lambda i, p: (i, 0)),
            pl.BlockSpec((tm, TOP_K), lambda i, p: (i, 0)),
            pl.BlockSpec((1, D_MODEL), lambda i, p: (0, 0)),
            pl.BlockSpec(memory_space=pl.ANY),
        ],
        out_specs=[pl.BlockSpec((tm, D_MODEL), lambda i, p: (i, 0))] * n_out,
        scratch_shapes=[pltpu.VMEM((TOP_K * tm, ROW_TILES, LANE), jnp.uint32),
                        pltpu.VMEM((TOP_K * tm, ROW_TILES, LANE), jnp.uint32),
                        pltpu.SemaphoreType.DMA((2,))],
    )
    return pl.pallas_call(
        _combine_kernel, grid_spec=grid_spec,
        out_shape=[jax.ShapeDtypeStruct((T, D_MODEL), F32)] * (n_out - 1)
        + [jax.ShapeDtypeStruct((T, D_MODEL), norm_dtype)],
        compiler_params=_params(("arbitrary",)),
    )(pos.reshape(-1), h, weight, g_next.reshape(1, D_MODEL), y)


def _dispatch(expert, T, blk):
    onehot = (expert[:, None] == jnp.arange(N_EXPERTS, dtype=jnp.int32)[None, :]).astype(jnp.int32)
    csum = jnp.cumsum(onehot, axis=0)
    rank = jnp.sum(csum * onehot, axis=1) - 1
    counts = csum[-1]
    padded = (counts + blk - 1) // blk * blk
    pad_end = jnp.cumsum(padded)
    pad_start = pad_end - padded
    pos = (jnp.sum(pad_start[None, :] * onehot, axis=1) + rank).astype(jnp.int32)
    n_rows = T * TOP_K + (N_EXPERTS + 1) * blk
    token = jnp.repeat(jnp.arange(T, dtype=jnp.int32), TOP_K)
    row_tok = jnp.zeros((n_rows,), jnp.int32).at[pos].set(token, unique_indices=True)
    n_blocks = n_rows // blk
    starts = jnp.arange(n_blocks, dtype=jnp.int32) * blk
    block_exp = jnp.minimum(jnp.sum((pad_end[None, :] <= starts[:, None]).astype(jnp.int32), axis=1),
                            N_EXPERTS - 1).astype(jnp.int32)
    n_used = (pad_end[-1] // blk).astype(jnp.int32).reshape(1)
    return pos.reshape(T, TOP_K), row_tok, block_exp, n_used


def _moe_layer(h, norm_g, w_group, b_group, w_exp, b_exp, wg, wu, wd, layer, g_next, norm_dtype,
               keep_sum=True):
    T = h.shape[0]
    wr = jnp.zeros((D_MODEL, LANE), F32)
    wr = wr.at[:, :N_GROUPS].set(w_group).at[:, N_GROUPS:N_GROUPS + N_EXPERTS].set(w_exp)
    br = jnp.zeros((LANE,), F32).at[:N_GROUPS].set(b_group).at[N_GROUPS:N_GROUPS + N_EXPERTS].set(b_exp)
    e_t, w_t, xn = router(h, norm_g, wr, br.reshape(LANE, 1))
    expert = e_t[:TOP_K].T.reshape(-1)
    weight = w_t[:TOP_K].T
    pos, row_tok, block_exp, n_used = _dispatch(expert, T, MOE_BLK)
    y = moe_experts(xn, n_used, block_exp, row_tok, wg, wu, wd, layer)
    return moe_combine(h, y, pos, weight, g_next, norm_dtype, keep_sum)


def _gmlp_layer(h, norm_g, w_in, ln_g, ln_b, w_s, b_s, w_out):
    causal = jnp.tril(jnp.ones((GM_CHUNK, GM_CHUNK), w_s.dtype))
    bs_full = jnp.repeat(b_s.T, D_MODEL // GM_GROUPS, axis=1)
    y = gmlp_gate(h, norm_g, w_in.astype(BF16), ln_g, ln_b, (w_s * causal).astype(BF16), bs_full)
    return matmul(y, w_out.astype(BF16), out_dtype=F32, resid=h)


def _overlap_t(S):
    n_c = S // CMP_STRIDE
    n_sel = S // SEL_BLK
    ci = np.arange(n_c)[None, :] * CMP_STRIDE
    sj = np.arange(n_sel)[:, None] * SEL_BLK
    ov = (ci < sj + SEL_BLK) & (ci + CMP_LEN > sj) & (np.arange(n_c)[None, :] < n_c - 1)
    return jnp.asarray(ov.astype(np.float32), dtype=BF16)


def _nsa_layer(h, hn, B, S, w_in, ck_pe, ck_w1, ck_w2, cv_pe, cv_w1, cv_w2, w_out):
    qd = N_HEADS * HEAD_DIM
    kvd = N_KV_HEADS * HEAD_DIM
    w = w_in.astype(BF16)
    o = [qd + i * kvd for i in range(7)]
    w_q, w_kc, w_vc, w_ks, w_vs, w_kw, w_vw = (
        w[:, :o[0]], w[:, o[0]:o[1]], w[:, o[1]:o[2]], w[:, o[2]:o[3]], w[:, o[3]:o[4]],
        w[:, o[4]:o[5]], w[:, o[5]:o[6]])
    w_g = jnp.zeros((D_MODEL, LANE), BF16).at[:, :3 * N_HEADS].set(w[:, o[6]:])
    q = matmul(hn, w_q, out_dtype=BF16, scale=HEAD_DIM ** -0.5 * LOG2E, layout="heads", seq=S)
    kvc = matmul(hn, jnp.concatenate([w_kc, w_vc], 1), out_dtype=F32, layout="heads", seq=S)
    ksw = matmul(hn, jnp.concatenate([w_ks, w_kw], 1), out_dtype=BF16, layout="heads", seq=S)
    vswt = matmul(hn, jnp.concatenate([w_vs, w_vw], 1), out_dtype=BF16, layout="heads_t", seq=S)
    gates = matmul(hn, w_g, out_dtype=F32, act="sigmoid")[:, :3 * N_HEADS]
    gates_t = gates.reshape(B, S, N_KV_HEADS, GQA, 3).transpose(0, 2, 4, 3, 1)

    half = CMP_STRIDE
    pe2 = jnp.stack([jnp.stack([pe[:half].reshape(-1), pe[half:].reshape(-1)])
                     for pe in (ck_pe, cv_pe)])
    w1 = jnp.stack([ck_w1, cv_w1]).astype(BF16)
    w2 = jnp.stack([ck_w2, cv_w2]).astype(BF16)
    cmp, cmp_t = compress_kv(kvc, pe2, w1, w2)
    n_c = S // CMP_STRIDE
    cmp = cmp.reshape(B, 2 * N_KV_HEADS, n_c, HEAD_DIM)
    cmp_t = cmp_t.reshape(B, 2 * N_KV_HEADS, HEAD_DIM, n_c)
    blk_id = (np.arange(S) // SEL_BLK) % LANE
    onehot = jnp.asarray(blk_id[:, None] == np.arange(LANE)[None, :], dtype=BF16)
    ks_aug = jnp.concatenate(
        [ksw[:, :N_KV_HEADS], jnp.broadcast_to(onehot, (B, N_KV_HEADS, S, LANE))], axis=-1)
    o_att = nsa_attention_core(q, cmp, cmp_t, ks_aug, ksw, vswt, gates_t, _overlap_t(S))
    return matmul(o_att.reshape(B * S, qd), w_out.astype(BF16), out_dtype=F32, resid=h)


def kernel(x, norm_mix, norm_ffn, norm_final, a_w_in, a_ln_g, a_ln_b, a_w_s, a_b_s, a_w_out,
           b_w_in, b_ck_pe, b_ck_w1, b_ck_w2, b_cv_pe, b_cv_w1, b_cv_w2, b_w_out,
           r_w_group, r_b_group, r_w_exp, r_b_exp, e_w_gate, e_w_up, e_w_down):
    B, S, _ = x.shape
    h = x.reshape(B * S, D_MODEL)
    h = _gmlp_layer(h, norm_mix[0], a_w_in[0], a_ln_g[0], a_ln_b[0], a_w_s[0], a_b_s[0], a_w_out[0])
    h, hn = _moe_layer(h, norm_ffn[0], r_w_group[0], r_b_group[0], r_w_exp[0], r_b_exp[0],
                       e_w_gate, e_w_up, e_w_down, 0, norm_mix[1], BF16)
    h = _nsa_layer(h, hn, B, S, b_w_in[0], b_ck_pe[0], b_ck_w1[0], b_ck_w2[0],
                   b_cv_pe[0], b_cv_w1[0], b_cv_w2[0], b_w_out[0])
    (out,) = _moe_layer(h, norm_ffn[1], r_w_group[1], r_b_group[1], r_w_exp[1], r_b_exp[1],
                        e_w_gate, e_w_up, e_w_down, 1, norm_final, F32, keep_sum=False)
    return out.reshape(B, S, D_MODEL)
```

```python
import functools

import numpy as np
import jax
import jax.numpy as jnp
from jax import lax
from jax.experimental import pallas as pl
from jax.experimental.pallas import tpu as pltpu

F32 = jnp.float32
BF16 = jnp.bfloat16

D_MODEL = 2048
LANE = 128
GM_GROUPS = 8
GM_CHUNK = 128
HEAD_DIM = 128
N_HEADS = 16
N_KV_HEADS = 4
GQA = N_HEADS // N_KV_HEADS
CMP_LEN = 32
CMP_STRIDE = 16
SEL_BLK = 64
SEL_TOPN = 16
SEL_LOCAL = 2
WINDOW = 512
Q_BLK = 256
FORCE = 1.0e4
N_GROUPS = 4
EXP_PER_GROUP = 8
N_EXPERTS = N_GROUPS * EXP_PER_GROUP
TOP_K = 2
D_EXPERT = 512
EPS = 1e-6
NEG = -1e30
LOG2E = 1.4426950408889634
ONES_ROWS = 16
SEL_TILE = 512
MOE_BLK = 512
MOE_CHUNK = 256
VMEM_LIMIT = 56 * 1024 * 1024

_NT = (((1,), (1,)), ((), ()))


def _params(sem):
    return pltpu.CompilerParams(dimension_semantics=sem, vmem_limit_bytes=VMEM_LIMIT)


def _gelu(x):
    return 0.5 * x * (1.0 + jnp.tanh(0.7978845608028654 * (x + 0.044715 * (x * x * x))))


def _rms(x, g):
    y = x * lax.rsqrt(jnp.mean(x * x, axis=-1, keepdims=True) + EPS)
    return y * g


def _pack_rows(x):
    half = x.shape[1] // 2
    bits = lambda v: lax.bitcast_convert_type(v.astype(BF16).astype(F32), jnp.uint32)
    return (bits(x[:, :half]) >> 16) | (bits(x[:, half:]) & jnp.uint32(0xFFFF0000))


def _unpack_rows(w):
    lo = lax.bitcast_convert_type(w << 16, F32)
    hi = lax.bitcast_convert_type(w & jnp.uint32(0xFFFF0000), F32)
    return lo, hi


def _rmsnorm_kernel(x_ref, g_ref, o_ref):
    o_ref[...] = _rms(x_ref[...], g_ref[...]).astype(o_ref.dtype)


def rmsnorm(x, g, out_dtype=BF16, tm=512):
    T = x.shape[0]
    return pl.pallas_call(
        _rmsnorm_kernel,
        grid=(T // tm,),
        in_specs=[pl.BlockSpec((tm, D_MODEL), lambda i: (i, 0)),
                  pl.BlockSpec((1, D_MODEL), lambda i: (0, 0))],
        out_specs=pl.BlockSpec((tm, D_MODEL), lambda i: (i, 0)),
        out_shape=jax.ShapeDtypeStruct((T, D_MODEL), out_dtype),
        compiler_params=_params(("parallel",)),
    )(x, g.reshape(1, D_MODEL))


def _router_kernel(x_ref, g_ref, wh_ref, wl_ref, br_ref, e_ref, w_ref, xn_ref):
    y = _rms(x_ref[...], g_ref[...])
    xn_ref[...] = _pack_rows(y)
    y_hi = y.astype(BF16)
    y_lo = (y - y_hi.astype(F32)).astype(BF16)
    lg = (jnp.dot(y_hi, wh_ref[...], preferred_element_type=F32)
          + jnp.dot(y_hi, wl_ref[...], preferred_element_type=F32)
          + jnp.dot(y_lo, wh_ref[...], preferred_element_type=F32))
    lt = lg.T + br_ref[...]
    row = lambda i: lt[i:i + 1, :]

    def softmax(rows):
        m = functools.reduce(jnp.maximum, rows)
        e = [jnp.exp(r - m) for r in rows]
        s = functools.reduce(lambda a, b: a + b, e)
        return [v / s for v in e]

    def top1(vals, skip=None):
        best = jnp.full_like(vals[0], -1.0)
        idx = jnp.zeros(vals[0].shape, jnp.int32)
        for k, v in enumerate(vals):
            if skip is not None:
                v = jnp.where(skip == k, -1.0, v)
            upd = v > best
            best = jnp.where(upd, v, best)
            idx = jnp.where(upd, k, idx)
        return best, idx

    g_w, g_idx = top1(softmax([row(j) for j in range(N_GROUPS)]))
    e_logits = []
    for k in range(EXP_PER_GROUP):
        v = row(N_GROUPS + k)
        for gg in range(1, N_GROUPS):
            v = jnp.where(g_idx == gg, row(N_GROUPS + gg * EXP_PER_GROUP + k), v)
        e_logits.append(v)
    e_prob = softmax(e_logits)
    w1, i1 = top1(e_prob)
    w2, i2 = top1(e_prob, skip=i1)
    tot = w1 + w2
    zi = jnp.zeros((6, lt.shape[1]), jnp.int32)
    e_ref[...] = jnp.concatenate([g_idx * EXP_PER_GROUP + i1, g_idx * EXP_PER_GROUP + i2, zi], axis=0)
    w_ref[...] = jnp.concatenate([g_w * (w1 / tot), g_w * (w2 / tot), zi.astype(F32)], axis=0)


def router(x, g, wr, br, tm=512):
    T = x.shape[0]
    wr_hi = wr.astype(BF16)
    wr_lo = (wr - wr_hi.astype(F32)).astype(BF16)
    return pl.pallas_call(
        _router_kernel,
        grid=(T // tm,),
        in_specs=[pl.BlockSpec((tm, D_MODEL), lambda i: (i, 0)),
                  pl.BlockSpec((1, D_MODEL), lambda i: (0, 0)),
                  pl.BlockSpec((D_MODEL, LANE), lambda i: (0, 0)),
                  pl.BlockSpec((D_MODEL, LANE), lambda i: (0, 0)),
                  pl.BlockSpec((LANE, 1), lambda i: (0, 0))],
        out_specs=[pl.BlockSpec((8, tm), lambda i: (0, i)),
                   pl.BlockSpec((8, tm), lambda i: (0, i)),
                   pl.BlockSpec((tm, D_MODEL // 2), lambda i: (i, 0))],
        out_shape=[jax.ShapeDtypeStruct((8, T), jnp.int32),
                   jax.ShapeDtypeStruct((8, T), F32),
                   jax.ShapeDtypeStruct((T, D_MODEL // 2), jnp.uint32)],
        compiler_params=_params(("parallel",)),
    )(x, g.reshape(1, D_MODEL), wr_hi, wr_lo, br)


def _mm_kernel(*refs, act, scale, has_resid, layout):
    a_ref, w_ref = refs[0], refs[1]
    o_ref = refs[-1]
    acc = jnp.dot(a_ref[...], w_ref[...], preferred_element_type=F32)
    if scale is not None:
        acc = acc * scale
    if act == "gelu":
        acc = _gelu(acc)
    elif act == "sigmoid":
        acc = jax.nn.sigmoid(acc)
    if has_resid:
        acc = acc + refs[2][...]
    tm, tn = acc.shape
    if layout == "plain":
        o_ref[...] = acc.astype(o_ref.dtype)
    elif layout == "heads":
        for h in range(tn // HEAD_DIM):
            o_ref[0, h] = acc[:, h * HEAD_DIM:(h + 1) * HEAD_DIM].astype(o_ref.dtype)
    else:
        for h in range(tn // HEAD_DIM):
            for c in range(tm // LANE):
                blk = acc[c * LANE:(c + 1) * LANE, h * HEAD_DIM:(h + 1) * HEAD_DIM]
                o_ref[0, h, c] = blk.T.astype(o_ref.dtype)


def matmul(a, w, *, out_dtype, act=None, scale=None, resid=None, layout="plain",
           seq=None, tm=512, tn=1024):
    M, K = a.shape
    N = w.shape[1]
    tn = min(tn, N)
    grid = (N // tn, M // tm)
    in_specs = [pl.BlockSpec((tm, K), lambda j, i: (i, 0)),
                pl.BlockSpec((K, tn), lambda j, i: (0, j))]
    args = [a, w]
    if resid is not None:
        in_specs.append(pl.BlockSpec((tm, tn), lambda j, i: (i, j)))
        args.append(resid)
    if layout == "plain":
        out_shape = jax.ShapeDtypeStruct((M, N), out_dtype)
        out_spec = pl.BlockSpec((tm, tn), lambda j, i: (i, j))
    else:
        nb = seq // tm
        nh = tn // HEAD_DIM
        if layout == "heads":
            out_shape = jax.ShapeDtypeStruct((M // seq, N // HEAD_DIM, seq, HEAD_DIM), out_dtype)
            out_spec = pl.BlockSpec((1, nh, tm, HEAD_DIM), lambda j, i: (i // nb, j, i % nb, 0))
        else:
            out_shape = jax.ShapeDtypeStruct(
                (M // seq, N // HEAD_DIM, seq // LANE, HEAD_DIM, LANE), out_dtype)
            out_spec = pl.BlockSpec((1, nh, tm // LANE, HEAD_DIM, LANE),
                                    lambda j, i: (i // nb, j, i % nb, 0, 0))
    kern = functools.partial(_mm_kernel, act=act, scale=scale, has_resid=resid is not None,
                             layout=layout)
    return pl.pallas_call(
        kern, grid=grid, in_specs=in_specs, out_specs=out_spec, out_shape=out_shape,
        compiler_params=_params(("parallel", "parallel")),
    )(*args)


def _gate_kernel(x_ref, g_ref, w_ref, lng_ref, lnb_ref, ws_ref, bs_ref, y_ref):
    tm = x_ref.shape[0]
    a = _rms(x_ref[...], g_ref[...]).astype(BF16)
    z = _gelu(jnp.dot(a, w_ref[...], preferred_element_type=F32))
    u = z[:, :D_MODEL]
    v = z[:, D_MODEL:]
    mu = jnp.mean(v, axis=-1, keepdims=True)
    vc = v - mu
    vn = vc * lax.rsqrt(jnp.mean(vc * vc, axis=-1, keepdims=True) + EPS)
    vn = (vn * lng_ref[...] + lnb_ref[...]).astype(BF16)
    gd = D_MODEL // GM_GROUPS
    for c in range(tm // GM_CHUNK):
        rows = slice(c * GM_CHUNK, (c + 1) * GM_CHUNK)
        for g in range(GM_GROUPS):
            cols = slice(g * gd, (g + 1) * gd)
            sv = jnp.dot(ws_ref[g], vn[rows, cols], preferred_element_type=F32)
            y_ref[rows, cols] = (u[rows, cols] * (sv + bs_ref[:, cols])).astype(y_ref.dtype)


def gmlp_gate(x, g, w_in, ln_g, ln_b, ws_masked, bs_full, tm=512):
    T = x.shape[0]
    return pl.pallas_call(
        _gate_kernel,
        grid=(T // tm,),
        in_specs=[pl.BlockSpec((tm, D_MODEL), lambda i: (i, 0)),
                  pl.BlockSpec((1, D_MODEL), lambda i: (0, 0)),
                  pl.BlockSpec((D_MODEL, 2 * D_MODEL), lambda i: (0, 0), pipeline_mode=pl.Buffered(1)),
                  pl.BlockSpec((1, D_MODEL), lambda i: (0, 0)),
                  pl.BlockSpec((1, D_MODEL), lambda i: (0, 0)),
                  pl.BlockSpec((GM_GROUPS, GM_CHUNK, GM_CHUNK), lambda i: (0, 0, 0)),
                  pl.BlockSpec((GM_CHUNK, D_MODEL), lambda i: (0, 0))],
        out_specs=pl.BlockSpec((tm, D_MODEL), lambda i: (i, 0)),
        out_shape=jax.ShapeDtypeStruct((T, D_MODEL), BF16),
        compiler_params=_params(("parallel",)),
    )(x, g.reshape(1, D_MODEL), w_in, ln_g.reshape(1, D_MODEL), ln_b.reshape(1, D_MODEL), ws_masked,
      bs_full)


def _compress_kernel(x_ref, pe_ref, w1_ref, w2_ref, o_ref, ot_ref):
    x = x_ref[0]
    half = CMP_STRIDE * HEAD_DIM
    xa = (x + pe_ref[0, 0:1, :]).astype(BF16)
    xb = (x + pe_ref[0, 1:2, :]).astype(BF16)
    a = jnp.dot(xa, w1_ref[0, :half, :], preferred_element_type=F32)
    b = jnp.dot(xb, w1_ref[0, half:, :], preferred_element_type=F32)
    n = a.shape[0]
    pre = a + pltpu.roll(b, n - 1, axis=0)
    out = jnp.dot(_gelu(pre).astype(BF16), w2_ref[0], preferred_element_type=F32)
    o_ref[0] = out.astype(o_ref.dtype)
    ot_ref[0] = out.T.astype(ot_ref.dtype)


def compress_kv(kvc, pe2, w1, w2):
    B, H2, S, _ = kvc.shape
    n_grp = S // CMP_STRIDE
    x = kvc.reshape(B * H2, n_grp, CMP_STRIDE * HEAD_DIM)
    sel = lambda i: ((i % H2) // N_KV_HEADS, 0, 0)
    return pl.pallas_call(
        _compress_kernel,
        grid=(B * H2,),
        in_specs=[pl.BlockSpec((1, n_grp, CMP_STRIDE * HEAD_DIM), lambda i: (i, 0, 0)),
                  pl.BlockSpec((1, 2, CMP_STRIDE * HEAD_DIM), sel),
                  pl.BlockSpec((1, CMP_LEN * HEAD_DIM, HEAD_DIM), sel),
                  pl.BlockSpec((1, HEAD_DIM, HEAD_DIM), sel)],
        out_specs=[pl.BlockSpec((1, n_grp, HEAD_DIM), lambda i: (i, 0, 0)),
                   pl.BlockSpec((1, HEAD_DIM, n_grp), lambda i: (i, 0, 0))],
        out_shape=[jax.ShapeDtypeStruct((B * H2, n_grp, HEAD_DIM), BF16),
                   jax.ShapeDtypeStruct((B * H2, HEAD_DIM, n_grp), BF16)],
        compiler_params=_params(("parallel",)),
    )(x, pe2, w1, w2)


def _nsa_kernel(q_ref, kc_ref, vct_ref, ksa_ref, vst_ref, kw_ref, vwt_ref, gt_ref, ovt_ref, wm_ref,
                o_ref, qaug_ref, bsel_ref, s_a, s_b):
    nq = GQA * Q_BLK
    qb = pl.program_id(2)
    q0 = qb * Q_BLK
    q_t = q_ref[0].reshape(nq, HEAD_DIM).astype(F32).T.astype(BF16)
    t_lane = q0 + (lax.broadcasted_iota(jnp.int32, (1, nq), 1) & (Q_BLK - 1))

    n_c = kc_ref.shape[2]

    def compressed(n):
        s = jnp.dot(kc_ref[0, 0, :n, :], q_t, preferred_element_type=F32)
        c_end = lax.broadcasted_iota(jnp.int32, (n, 1), 0) * CMP_STRIDE + (CMP_LEN - 1)
        s = jnp.where(c_end <= t_lane, s, NEG)
        m = jnp.max(s, axis=0, keepdims=True)
        p = jnp.exp2(s - m)
        l = jnp.sum(p, axis=0, keepdims=True)
        p = p * jnp.where(m > 0.5 * NEG, 1.0 / l, 0.0)
        o = jnp.dot(vct_ref[0, 0, :, :n], p.astype(BF16), preferred_element_type=F32)
        psum = p[:, 0:Q_BLK]
        for h in range(1, GQA):
            psum = psum + p[:, h * Q_BLK:(h + 1) * Q_BLK]
        p_hi = psum.astype(BF16)
        p_lo = (psum - p_hi.astype(F32)).astype(BF16)
        ov = ovt_ref[:, :n]
        return o, (jnp.dot(ov, p_hi, preferred_element_type=F32)
                   + jnp.dot(ov, p_lo, preferred_element_type=F32))

    n_bkt = 4 if n_c % (4 * LANE) == 0 else 1
    bkt = n_c // n_bkt
    if n_bkt == 1:
        o_cmp, imp = compressed(n_c)
    else:
        o_cmp, imp = lax.switch((q0 + Q_BLK - 1) // (bkt * CMP_STRIDE),
                                [functools.partial(compressed, bkt * (i + 1)) for i in range(n_bkt)])

    n_sel = imp.shape[0]
    blk = lax.broadcasted_iota(jnp.int32, (n_sel, Q_BLK), 0)
    t_q = q0 + lax.broadcasted_iota(jnp.int32, (n_sel, Q_BLK), 1)
    dist = (t_q // SEL_BLK) - blk
    forced = (blk == 0) | ((dist >= 0) & (dist < SEL_LOCAL))
    score = jnp.where(forced, -jnp.inf, jnp.where(blk * SEL_BLK <= t_q, imp, NEG))
    blk_f = blk.astype(F32)
    for _ in range(SEL_TOPN - 1 - SEL_LOCAL):
        top = jnp.max(score, axis=0, keepdims=True)
        first = jnp.min(jnp.where(score == top, blk_f, float(n_sel)), axis=0, keepdims=True)
        score = jnp.where(blk_f == first, -jnp.inf, score)
    picked = score == -jnp.inf
    bsel_ref[...] = jnp.where(picked, 0.0, NEG)
    bias = jnp.where(picked & (blk * SEL_BLK < q0), 0.0, NEG)

    n_half = qaug_ref.shape[0]
    if n_sel < LANE:
        bias = jnp.concatenate([bias, jnp.zeros((LANE - n_sel, Q_BLK), F32)], axis=0)
    for hf in range(n_half):
        bq = bias[hf * LANE:(hf + 1) * LANE, :].astype(BF16)
        qaug_ref[hf] = jnp.concatenate([q_t, jnp.concatenate([bq] * GQA, axis=1)], axis=0)

    wk = WINDOW + Q_BLK
    kstart = pl.multiple_of(jnp.maximum(q0 - WINDOW, 0), LANE)
    sw = jnp.dot(kw_ref[0, 0, pl.ds(kstart, wk), :], q_t, preferred_element_type=F32)
    sw = sw + jnp.concatenate([wm_ref[0]] * GQA, axis=1)
    m_w = jnp.max(sw, axis=0, keepdims=True)
    p_w = jnp.exp2((sw - m_w).astype(BF16))
    vw = jnp.concatenate([vwt_ref[0, 0, kstart // LANE + j] for j in range(wk // LANE)], axis=1)
    vw = jnp.concatenate([vw, jnp.ones((ONES_ROWS, wk), BF16)], axis=0)
    acc_w = jnp.dot(vw, p_w, preferred_element_type=F32)
    o_win = acc_w[:HEAD_DIM] * (1.0 / acc_w[HEAD_DIM:HEAD_DIM + 1])

    kpos = q0 + lax.broadcasted_iota(jnp.int32, (Q_BLK, 1), 0)
    s_d = jnp.dot(ksa_ref[0, 0, pl.ds(pl.multiple_of(q0, Q_BLK), Q_BLK), :HEAD_DIM], q_t,
                  preferred_element_type=F32)
    parts = []
    for j in range(Q_BLK // SEL_BLK):
        b_row = bsel_ref[pl.ds(q0 // SEL_BLK + j, 1), :]
        parts.append(s_d[j * SEL_BLK:(j + 1) * SEL_BLK, :] + jnp.concatenate([b_row] * GQA, axis=1))
    s_d = jnp.where(kpos <= t_lane, jnp.concatenate(parts, axis=0), NEG)
    m_d = jnp.max(s_d, axis=0, keepdims=True)
    p_d = jnp.exp2(s_d - m_d)
    v_d = jnp.concatenate([vst_ref[0, 0, q0 // LANE + j] for j in range(Q_BLK // LANE)], axis=1)
    init = (m_d, jnp.sum(p_d, axis=0, keepdims=True),
            jnp.dot(v_d, p_d.astype(BF16), preferred_element_type=F32))

    bpt = SEL_TILE // SEL_BLK
    vpt = SEL_TILE // LANE

    def scores(kt):
        k0 = pl.multiple_of(kt * SEL_TILE, SEL_TILE)
        return jnp.dot(ksa_ref[0, 0, pl.ds(k0, SEL_TILE), :], qaug_ref[(kt * bpt) // LANE],
                       preferred_element_type=F32)

    def softmax_pv(kt, st, carry):
        m_i, l_i, acc = carry
        m_new = jnp.maximum(m_i, jnp.max(st, axis=0, keepdims=True))
        alpha = jnp.exp2(m_i - m_new)
        pt = jnp.exp2(st - m_new)
        l_new = alpha * l_i + jnp.sum(pt, axis=0, keepdims=True)
        vt = jnp.concatenate([vst_ref[0, 0, kt * vpt + j] for j in range(vpt)], axis=1)
        acc = alpha * acc + jnp.dot(vt, pt.astype(BF16), preferred_element_type=F32)
        return m_new, l_new, acc

    def trip(j, carry):
        st = s_a[...]
        s_b[...] = scores(2 * j + 1)
        carry = softmax_pv(2 * j, st, carry)
        st = s_b[...]
        s_a[...] = scores(2 * j + 2)
        return softmax_pv(2 * j + 1, st, carry)

    n_tiles = (q0 + SEL_TILE - 1) // SEL_TILE
    n_trips = jnp.maximum((n_tiles + 1) // 2, 1)
    s_a[...] = scores(0)
    carry = lax.fori_loop(0, n_trips - 1, trip, init)
    last = 2 * n_trips - 1
    st = s_a[...]
    s_b[...] = scores(last)
    carry = softmax_pv(last - 1, st, carry)
    _, l_s, acc_s = softmax_pv(last, s_b[...], carry)
    o_sel = acc_s * (1.0 / l_s)

    for h in range(GQA):
        cols = slice(h * Q_BLK, (h + 1) * Q_BLK)
        o_t = (gt_ref[0, 0, 0, h:h + 1, :] * o_cmp[:, cols]
               + gt_ref[0, 0, 1, h:h + 1, :] * o_sel[:, cols]
               + gt_ref[0, 0, 2, h:h + 1, :] * o_win[:, cols])
        o_ref[0, :, h * HEAD_DIM:(h + 1) * HEAD_DIM] = o_t.T.astype(o_ref.dtype)


def _window_mask(S):
    wk = WINDOW + Q_BLK
    i = np.arange(wk)[:, None]
    qi = np.arange(Q_BLK)[None, :]
    out = []
    for v in range(WINDOW // Q_BLK + 1):
        t = Q_BLK * v + qi if v < WINDOW // Q_BLK else WINDOW + qi
        out.append(np.where((i <= t) & (i > t - WINDOW), 0.0, NEG))
    return jnp.asarray(np.stack(out), dtype=F32)


def nsa_attention_core(q, kc, vct, ks_aug, kw, vswt, gates_t, ov_t):
    B, _, S, _ = q.shape
    n_c = S // CMP_STRIDE
    n_sel = S // SEL_BLK
    nq = GQA * Q_BLK
    hk = N_KV_HEADS
    wk = WINDOW + Q_BLK
    nv = WINDOW // Q_BLK
    once = pl.Buffered(1)
    return pl.pallas_call(
        _nsa_kernel,
        grid=(B, hk, S // Q_BLK),
        in_specs=[
            pl.BlockSpec((1, GQA, Q_BLK, HEAD_DIM), lambda b, h, i: (b, h, i, 0)),
            pl.BlockSpec((1, 1, n_c, HEAD_DIM), lambda b, h, i: (b, h, 0, 0)),
            pl.BlockSpec((1, 1, HEAD_DIM, n_c), lambda b, h, i: (b, hk + h, 0, 0)),
            pl.BlockSpec((1, 1, S, 2 * HEAD_DIM), lambda b, h, i: (b, h, 0, 0), pipeline_mode=once),
            pl.BlockSpec((1, 1, S // LANE, HEAD_DIM, LANE), lambda b, h, i: (b, h, 0, 0, 0),
                         pipeline_mode=once),
            pl.BlockSpec((1, 1, S, HEAD_DIM), lambda b, h, i: (b, hk + h, 0, 0), pipeline_mode=once),
            pl.BlockSpec((1, 1, S // LANE, HEAD_DIM, LANE), lambda b, h, i: (b, hk + h, 0, 0, 0),
                         pipeline_mode=once),
            pl.BlockSpec((1, 1, 3, GQA, Q_BLK), lambda b, h, i: (b, h, 0, 0, i)),
            pl.BlockSpec((n_sel, n_c), lambda b, h, i: (0, 0)),
            pl.BlockSpec((1, wk, Q_BLK), lambda b, h, i: (jnp.minimum(i, nv), 0, 0)),
        ],
        out_specs=pl.BlockSpec((1, Q_BLK, GQA * HEAD_DIM), lambda b, h, i: (b, i, h)),
        out_shape=jax.ShapeDtypeStruct((B, S, N_HEADS * HEAD_DIM), BF16),
        scratch_shapes=[pltpu.VMEM((max(1, n_sel // LANE), 2 * HEAD_DIM, nq), BF16),
                        pltpu.VMEM((n_sel, Q_BLK), F32),
                        pltpu.VMEM((SEL_TILE, nq), F32), pltpu.VMEM((SEL_TILE, nq), F32)],
        compiler_params=_params(("parallel", "parallel", "arbitrary")),
    )(q, kc, vct, ks_aug, vswt, kw, vswt, gates_t, ov_t, _window_mask(S))


def _moe_kernel(nused_ref, bexp_ref, rtok_ref, h_hbm, wg_ref, wu_ref, wd_ref, y_ref,
                x_even, x_odd, wg_b, wu_b, wd_b, sem):
    i = pl.program_id(0)
    n_used = nused_ref[0]
    blk = x_even.shape[0]
    bufs = (x_even, x_odd)

    def row_copy(b, r, slot, offset=0):
        tok = rtok_ref[b * blk + r] + offset
        return pltpu.make_async_copy(h_hbm.at[pl.ds(tok, 1)], bufs[slot].at[pl.ds(r, 1)],
                                     sem.at[slot])

    def wait_block(slot):
        pltpu.make_async_copy(h_hbm.at[pl.ds(0, blk)], bufs[slot], sem.at[slot]).wait()

    @pl.when(i == 0)
    def _():
        def body(r, c):
            row_copy(0, r, 0).start()
            return c
        lax.fori_loop(0, blk, body, 0, unroll=8)

    @pl.when((i < n_used) & ((i == 0) | (bexp_ref[i] != bexp_ref[jnp.maximum(i - 1, 0)])))
    def _():
        wg_b[...] = wg_ref[0, 0].astype(BF16)
        wu_b[...] = wu_ref[0, 0].astype(BF16)
        wd_b[...] = wd_ref[0, 0].astype(BF16)

    for slot in range(2):
        @pl.when((i < n_used) & (i % 2 == slot))
        def _(slot=slot):
            wait_block(slot)
            x = jnp.concatenate(_unpack_rows(bufs[slot][...]), axis=1).astype(BF16)
            n_chunk = 2 * (D_EXPERT // MOE_CHUNK) + D_MODEL // MOE_CHUNK
            per = -(-blk // n_chunk)
            done = [0]

            def chunk(a, w_ref, c):
                res = jnp.dot(a, w_ref[:, c * MOE_CHUNK:(c + 1) * MOE_CHUNK],
                              preferred_element_type=F32)
                probe = res[0, 0]
                zero = (probe != probe).astype(jnp.int32)
                for r in range(done[0], min(done[0] + per, blk)):
                    row_copy(i + 1, r, 1 - slot, zero).start()
                done[0] = min(done[0] + per, blk)
                return res

            nc = D_EXPERT // MOE_CHUNK
            gate = jnp.concatenate([chunk(x, wg_b, c) for c in range(nc)], axis=1)
            up = jnp.concatenate([chunk(x, wu_b, c) for c in range(nc)], axis=1)
            hid = (gate * jax.nn.sigmoid(gate) * up).astype(BF16)
            y = jnp.concatenate([chunk(hid, wd_b, c) for c in range(D_MODEL // MOE_CHUNK)], axis=1)
            y_ref[...] = _pack_rows(y)

        @pl.when((i == n_used) & (i % 2 == slot))
        def _(slot=slot):
            wait_block(slot)

    @pl.when(i >= n_used)
    def _():
        y_ref[...] = jnp.zeros_like(y_ref)


def moe_experts(xn, n_used, block_exp, row_tok, wg, wu, wd, layer, blk=MOE_BLK):
    n_rows = row_tok.shape[0]
    n_steps = n_rows // blk
    grid_spec = pltpu.PrefetchScalarGridSpec(
        num_scalar_prefetch=3,
        grid=(n_steps,),
        in_specs=[
            pl.BlockSpec(memory_space=pl.ANY),
            pl.BlockSpec((1, 1, D_MODEL, D_EXPERT), lambda i, nu, be, rt: (layer, be[i], 0, 0)),
            pl.BlockSpec((1, 1, D_MODEL, D_EXPERT), lambda i, nu, be, rt: (layer, be[i], 0, 0)),
            pl.BlockSpec((1, 1, D_EXPERT, D_MODEL), lambda i, nu, be, rt: (layer, be[i], 0, 0)),
        ],
        out_specs=pl.BlockSpec((blk, D_MODEL // 2), lambda i, nu, be, rt: (i, 0)),
        scratch_shapes=[pltpu.VMEM((blk, D_MODEL // 2), jnp.uint32),
                        pltpu.VMEM((blk, D_MODEL // 2), jnp.uint32),
                        pltpu.VMEM((D_MODEL, D_EXPERT), BF16), pltpu.VMEM((D_MODEL, D_EXPERT), BF16),
                        pltpu.VMEM((D_EXPERT, D_MODEL), BF16), pltpu.SemaphoreType.DMA((2,))],
    )
    return pl.pallas_call(
        _moe_kernel, grid_spec=grid_spec,
        out_shape=jax.ShapeDtypeStruct((n_rows, D_MODEL // 2), jnp.uint32),
        compiler_params=_params(("arbitrary",)),
    )(n_used, block_exp, row_tok, xn, wg, wu, wd)


def _combine_kernel(pos_ref, h_ref, w_ref, g_ref, y_hbm, *rest):
    *o_refs, n_ref, y_even, y_odd, sem = rest
    i = pl.program_id(0)
    n = pl.num_programs(0)
    tm = h_ref.shape[0]
    bufs = (y_even, y_odd)

    def row_copy(b, j, k, slot):
        row = pos_ref[(b * tm + j) * TOP_K + k]
        return pltpu.make_async_copy(y_hbm.at[pl.ds(row, 1)], bufs[slot].at[pl.ds(k * tm + j, 1)],
                                     sem.at[slot])

    @pl.when(i == 0)
    def _():
        def body(j, c):
            for k in range(TOP_K):
                row_copy(0, j, k, 0).start()
            return c
        lax.fori_loop(0, tm, body, 0, unroll=4)

    def step(slot, prefetch):
        pltpu.make_async_copy(y_hbm.at[pl.ds(0, TOP_K * tm)], bufs[slot], sem.at[slot]).wait()
        if prefetch:
            for j in range(tm):
                for k in range(TOP_K):
                    row_copy(i + 1, j, k, 1 - slot).start()
        half = D_MODEL // 2
        lo, hi = h_ref[:, :half], h_ref[:, half:]
        for k in range(TOP_K):
            y_lo, y_hi = _unpack_rows(bufs[slot][k * tm:(k + 1) * tm, :])
            lo = lo + y_lo * w_ref[:, k:k + 1]
            hi = hi + y_hi * w_ref[:, k:k + 1]
        out = jnp.concatenate([lo, hi], axis=1)
        for o_ref in o_refs:
            o_ref[...] = out
        n_ref[...] = _rms(out, g_ref[...]).astype(n_ref.dtype)

    for slot in range(2):
        for prefetch in (True, False):
            cond = (i % 2 == slot) & ((i + 1 < n) if prefetch else (i + 1 == n))
            pl.when(cond)(functools.partial(step, slot, prefetch))


def moe_combine(h, y, pos, weight, g_next, norm_dtype, keep_sum, tm=256):
    T = h.shape[0]
    n_out = 2 if keep_sum else 1
    grid_spec = pltpu.PrefetchScalarGridSpec(
        num_scalar_prefetch=1,
        grid=(T // tm,),
        in_specs=[
            pl.BlockSpec((tm, D_MODEL), lambda i, p: (i, 0)),
            pl.BlockSpec((tm, TOP_K), lambda i, p: (i, 0)),
            pl.BlockSpec((1, D_MODEL), lambda i, p: (0, 0)),
            pl.BlockSpec(memory_space=pl.ANY),
        ],
        out_specs=[pl.BlockSpec((tm, D_MODEL), lambda i, p: (i, 0))] * n_out,
        scratch_shapes=[pltpu.VMEM((TOP_K * tm, D_MODEL // 2), jnp.uint32),
                        pltpu.VMEM((TOP_K * tm, D_MODEL // 2), jnp.uint32),
                        pltpu.SemaphoreType.DMA((2,))],
    )
    return pl.pallas_call(
        _combine_kernel, grid_spec=grid_spec,
        out_shape=[jax.ShapeDtypeStruct((T, D_MODEL), F32)] * (n_out - 1)
        + [jax.ShapeDtypeStruct((T, D_MODEL), norm_dtype)],
        compiler_params=_params(("arbitrary",)),
    )(pos.reshape(-1), h, weight, g_next.reshape(1, D_MODEL), y)


def _dispatch(expert, T, blk):
    onehot = (expert[:, None] == jnp.arange(N_EXPERTS, dtype=jnp.int32)[None, :]).astype(jnp.int32)
    csum = jnp.cumsum(onehot, axis=0)
    rank = jnp.sum(csum * onehot, axis=1) - 1
    counts = csum[-1]
    padded = (counts + blk - 1) // blk * blk
    pad_end = jnp.cumsum(padded)
    pad_start = pad_end - padded
    pos = (jnp.sum(pad_start[None, :] * onehot, axis=1) + rank).astype(jnp.int32)
    n_rows = T * TOP_K + (N_EXPERTS + 1) * blk
    token = jnp.repeat(jnp.arange(T, dtype=jnp.int32), TOP_K)
    row_tok = jnp.zeros((n_rows,), jnp.int32).at[pos].set(token, unique_indices=True)
    n_blocks = n_rows // blk
    starts = jnp.arange(n_blocks, dtype=jnp.int32) * blk
    block_exp = jnp.minimum(jnp.sum((pad_end[None, :] <= starts[:, None]).astype(jnp.int32), axis=1),
                            N_EXPERTS - 1).astype(jnp.int32)
    n_used = (pad_end[-1] // blk).astype(jnp.int32).reshape(1)
    return pos.reshape(T, TOP_K), row_tok, block_exp, n_used


def _moe_layer(h, norm_g, w_group, b_group, w_exp, b_exp, wg, wu, wd, layer, g_next, norm_dtype,
               keep_sum=True):
    T = h.shape[0]
    wr = jnp.zeros((D_MODEL, LANE), F32)
    wr = wr.at[:, :N_GROUPS].set(w_group).at[:, N_GROUPS:N_GROUPS + N_EXPERTS].set(w_exp)
    br = jnp.zeros((LANE,), F32).at[:N_GROUPS].set(b_group).at[N_GROUPS:N_GROUPS + N_EXPERTS].set(b_exp)
    e_t, w_t, xn = router(h, norm_g, wr, br.reshape(LANE, 1))
    expert = e_t[:TOP_K].T.reshape(-1)
    weight = w_t[:TOP_K].T
    pos, row_tok, block_exp, n_used = _dispatch(expert, T, MOE_BLK)
    y = moe_experts(xn, n_used, block_exp, row_tok, wg, wu, wd, layer)
    return moe_combine(h, y, pos, weight, g_next, norm_dtype, keep_sum)


def _gmlp_layer(h, norm_g, w_in, ln_g, ln_b, w_s, b_s, w_out):
    causal = jnp.tril(jnp.ones((GM_CHUNK, GM_CHUNK), w_s.dtype))
    bs_full = jnp.repeat(b_s.T, D_MODEL // GM_GROUPS, axis=1)
    y = gmlp_gate(h, norm_g, w_in.astype(BF16), ln_g, ln_b, (w_s * causal).astype(BF16), bs_full)
    return matmul(y, w_out.astype(BF16), out_dtype=F32, resid=h)


def _overlap_t(S):
    n_c = S // CMP_STRIDE
    n_sel = S // SEL_BLK
    ci = np.arange(n_c)[None, :] * CMP_STRIDE
    sj = np.arange(n_sel)[:, None] * SEL_BLK
    ov = (ci < sj + SEL_BLK) & (ci + CMP_LEN > sj) & (np.arange(n_c)[None, :] < n_c - 1)
    return jnp.asarray(ov.astype(np.float32), dtype=BF16)


def _nsa_layer(h, hn, B, S, w_in, ck_pe, ck_w1, ck_w2, cv_pe, cv_w1, cv_w2, w_out):
    qd = N_HEADS * HEAD_DIM
    kvd = N_KV_HEADS * HEAD_DIM
    w = w_in.astype(BF16)
    o = [qd + i * kvd for i in range(7)]
    w_q, w_kc, w_vc, w_ks, w_vs, w_kw, w_vw = (
        w[:, :o[0]], w[:, o[0]:o[1]], w[:, o[1]:o[2]], w[:, o[2]:o[3]], w[:, o[3]:o[4]],
        w[:, o[4]:o[5]], w[:, o[5]:o[6]])
    w_g = jnp.zeros((D_MODEL, LANE), BF16).at[:, :3 * N_HEADS].set(w[:, o[6]:])
    q = matmul(hn, w_q, out_dtype=BF16, scale=HEAD_DIM ** -0.5 * LOG2E, layout="heads", seq=S)
    kvc = matmul(hn, jnp.concatenate([w_kc, w_vc], 1), out_dtype=F32, layout="heads", seq=S)
    ksw = matmul(hn, jnp.concatenate([w_ks, w_kw], 1), out_dtype=BF16, layout="heads", seq=S)
    vswt = matmul(hn, jnp.concatenate([w_vs, w_vw], 1), out_dtype=BF16, layout="heads_t", seq=S)
    gates = matmul(hn, w_g, out_dtype=F32, act="sigmoid")[:, :3 * N_HEADS]
    gates_t = gates.reshape(B, S, N_KV_HEADS, GQA, 3).transpose(0, 2, 4, 3, 1)

    half = CMP_STRIDE
    pe2 = jnp.stack([jnp.stack([pe[:half].reshape(-1), pe[half:].reshape(-1)])
                     for pe in (ck_pe, cv_pe)])
    w1 = jnp.stack([ck_w1, cv_w1]).astype(BF16)
    w2 = jnp.stack([ck_w2, cv_w2]).astype(BF16)
    cmp, cmp_t = compress_kv(kvc, pe2, w1, w2)
    n_c = S // CMP_STRIDE
    cmp = cmp.reshape(B, 2 * N_KV_HEADS, n_c, HEAD_DIM)
    cmp_t = cmp_t.reshape(B, 2 * N_KV_HEADS, HEAD_DIM, n_c)
    blk_id = (np.arange(S) // SEL_BLK) % LANE
    onehot = jnp.asarray(blk_id[:, None] == np.arange(LANE)[None, :], dtype=BF16)
    ks_aug = jnp.concatenate(
        [ksw[:, :N_KV_HEADS], jnp.broadcast_to(onehot, (B, N_KV_HEADS, S, LANE))], axis=-1)
    o_att = nsa_attention_core(q, cmp, cmp_t, ks_aug, ksw, vswt, gates_t, _overlap_t(S))
    return matmul(o_att.reshape(B * S, qd), w_out.astype(BF16), out_dtype=F32, resid=h)


def kernel(x, norm_mix, norm_ffn, norm_final, a_w_in, a_ln_g, a_ln_b, a_w_s, a_b_s, a_w_out,
           b_w_in, b_ck_pe, b_ck_w1, b_ck_w2, b_cv_pe, b_cv_w1, b_cv_w2, b_w_out,
           r_w_group, r_b_group, r_w_exp, r_b_exp, e_w_gate, e_w_up, e_w_down):
    B, S, _ = x.shape
    h = x.reshape(B * S, D_MODEL)
    h = _gmlp_layer(h, norm_mix[0], a_w_in[0], a_ln_g[0], a_ln_b[0], a_w_s[0], a_b_s[0], a_w_out[0])
    h, hn = _moe_layer(h, norm_ffn[0], r_w_group[0], r_b_group[0], r_w_exp[0], r_b_exp[0],
                       e_w_gate, e_w_up, e_w_down, 0, norm_mix[1], BF16)
    h = _nsa_layer(h, hn, B, S, b_w_in[0], b_ck_pe[0], b_ck_w1[0], b_ck_w2[0],
                   b_cv_pe[0], b_cv_w1[0], b_cv_w2[0], b_w_out[0])
    (out,) = _moe_layer(h, norm_ffn[1], r_w_group[1], r_b_group[1], r_w_exp[1], r_b_exp[1],
                        e_w_gate, e_w_up, e_w_down, 1, norm_final, F32, keep_sum=False)
    return out.reshape(B, S, D_MODEL)
```

```python
import functools

import numpy as np
import jax
import jax.numpy as jnp
from jax import lax
from jax.experimental import pallas as pl
from jax.experimental.pallas import tpu as pltpu

F32 = jnp.float32
BF16 = jnp.bfloat16

D_MODEL = 2048
LANE = 128
GM_GROUPS = 8
GM_CHUNK = 128
HEAD_DIM = 128
N_HEADS = 16
N_KV_HEADS = 4
GQA = N_HEADS // N_KV_HEADS
CMP_LEN = 32
CMP_STRIDE = 16
SEL_BLK = 64
SEL_TOPN = 16
SEL_LOCAL = 2
WINDOW = 512
Q_BLK = 256
FORCE = 1.0e4
N_GROUPS = 4
EXP_PER_GROUP = 8
N_EXPERTS = N_GROUPS * EXP_PER_GROUP
TOP_K = 2
D_EXPERT = 512
EPS = 1e-6
NEG = -1e30
LOG2E = 1.4426950408889634
ONES_ROWS = 16
SEL_TILE = 512
MOE_BLK = 512
VMEM_LIMIT = 56 * 1024 * 1024

_NT = (((1,), (1,)), ((), ()))


def _params(sem):
    return pltpu.CompilerParams(dimension_semantics=sem, vmem_limit_bytes=VMEM_LIMIT)


def _gelu(x):
    return 0.5 * x * (1.0 + jnp.tanh(0.7978845608028654 * (x + 0.044715 * (x * x * x))))


def _rms(x, g):
    y = x * lax.rsqrt(jnp.mean(x * x, axis=-1, keepdims=True) + EPS)
    return y * g


def _pack_rows(x):
    half = x.shape[1] // 2
    bits = lambda v: lax.bitcast_convert_type(v.astype(BF16).astype(F32), jnp.uint32)
    return (bits(x[:, :half]) >> 16) | (bits(x[:, half:]) & jnp.uint32(0xFFFF0000))


def _unpack_rows(w):
    lo = lax.bitcast_convert_type(w << 16, F32)
    hi = lax.bitcast_convert_type(w & jnp.uint32(0xFFFF0000), F32)
    return lo, hi


def _rmsnorm_kernel(x_ref, g_ref, o_ref):
    o_ref[...] = _rms(x_ref[...], g_ref[...]).astype(o_ref.dtype)


def rmsnorm(x, g, out_dtype=BF16, tm=512):
    T = x.shape[0]
    return pl.pallas_call(
        _rmsnorm_kernel,
        grid=(T // tm,),
        in_specs=[pl.BlockSpec((tm, D_MODEL), lambda i: (i, 0)),
                  pl.BlockSpec((1, D_MODEL), lambda i: (0, 0))],
        out_specs=pl.BlockSpec((tm, D_MODEL), lambda i: (i, 0)),
        out_shape=jax.ShapeDtypeStruct((T, D_MODEL), out_dtype),
        compiler_params=_params(("parallel",)),
    )(x, g.reshape(1, D_MODEL))


def _router_kernel(x_ref, g_ref, wh_ref, wl_ref, br_ref, e_ref, w_ref, xn_ref):
    y = _rms(x_ref[...], g_ref[...])
    xn_ref[...] = _pack_rows(y)
    y_hi = y.astype(BF16)
    y_lo = (y - y_hi.astype(F32)).astype(BF16)
    lg = (jnp.dot(y_hi, wh_ref[...], preferred_element_type=F32)
          + jnp.dot(y_hi, wl_ref[...], preferred_element_type=F32)
          + jnp.dot(y_lo, wh_ref[...], preferred_element_type=F32))
    lt = lg.T + br_ref[...]
    row = lambda i: lt[i:i + 1, :]

    def softmax(rows):
        m = functools.reduce(jnp.maximum, rows)
        e = [jnp.exp(r - m) for r in rows]
        s = functools.reduce(lambda a, b: a + b, e)
        return [v / s for v in e]

    def top1(vals, skip=None):
        best = jnp.full_like(vals[0], -1.0)
        idx = jnp.zeros(vals[0].shape, jnp.int32)
        for k, v in enumerate(vals):
            if skip is not None:
                v = jnp.where(skip == k, -1.0, v)
            upd = v > best
            best = jnp.where(upd, v, best)
            idx = jnp.where(upd, k, idx)
        return best, idx

    g_w, g_idx = top1(softmax([row(j) for j in range(N_GROUPS)]))
    e_logits = []
    for k in range(EXP_PER_GROUP):
        v = row(N_GROUPS + k)
        for gg in range(1, N_GROUPS):
            v = jnp.where(g_idx == gg, row(N_GROUPS + gg * EXP_PER_GROUP + k), v)
        e_logits.append(v)
    e_prob = softmax(e_logits)
    w1, i1 = top1(e_prob)
    w2, i2 = top1(e_prob, skip=i1)
    tot = w1 + w2
    zi = jnp.zeros((6, lt.shape[1]), jnp.int32)
    e_ref[...] = jnp.concatenate([g_idx * EXP_PER_GROUP + i1, g_idx * EXP_PER_GROUP + i2, zi], axis=0)
    w_ref[...] = jnp.concatenate([g_w * (w1 / tot), g_w * (w2 / tot), zi.astype(F32)], axis=0)


def router(x, g, wr, br, tm=512):
    T = x.shape[0]
    wr_hi = wr.astype(BF16)
    wr_lo = (wr - wr_hi.astype(F32)).astype(BF16)
    return pl.pallas_call(
        _router_kernel,
        grid=(T // tm,),
        in_specs=[pl.BlockSpec((tm, D_MODEL), lambda i: (i, 0)),
                  pl.BlockSpec((1, D_MODEL), lambda i: (0, 0)),
                  pl.BlockSpec((D_MODEL, LANE), lambda i: (0, 0)),
                  pl.BlockSpec((D_MODEL, LANE), lambda i: (0, 0)),
                  pl.BlockSpec((LANE, 1), lambda i: (0, 0))],
        out_specs=[pl.BlockSpec((8, tm), lambda i: (0, i)),
                   pl.BlockSpec((8, tm), lambda i: (0, i)),
                   pl.BlockSpec((tm, D_MODEL // 2), lambda i: (i, 0))],
        out_shape=[jax.ShapeDtypeStruct((8, T), jnp.int32),
                   jax.ShapeDtypeStruct((8, T), F32),
                   jax.ShapeDtypeStruct((T, D_MODEL // 2), jnp.uint32)],
        compiler_params=_params(("parallel",)),
    )(x, g.reshape(1, D_MODEL), wr_hi, wr_lo, br)


def _mm_kernel(*refs, act, scale, has_resid, layout):
    a_ref, w_ref = refs[0], refs[1]
    o_ref = refs[-1]
    acc = jnp.dot(a_ref[...], w_ref[...], preferred_element_type=F32)
    if scale is not None:
        acc = acc * scale
    if act == "gelu":
        acc = _gelu(acc)
    elif act == "sigmoid":
        acc = jax.nn.sigmoid(acc)
    if has_resid:
        acc = acc + refs[2][...]
    tm, tn = acc.shape
    if layout == "plain":
        o_ref[...] = acc.astype(o_ref.dtype)
    elif layout == "heads":
        for h in range(tn // HEAD_DIM):
            o_ref[0, h] = acc[:, h * HEAD_DIM:(h + 1) * HEAD_DIM].astype(o_ref.dtype)
    else:
        for h in range(tn // HEAD_DIM):
            for c in range(tm // LANE):
                blk = acc[c * LANE:(c + 1) * LANE, h * HEAD_DIM:(h + 1) * HEAD_DIM]
                o_ref[0, h, c] = blk.T.astype(o_ref.dtype)


def matmul(a, w, *, out_dtype, act=None, scale=None, resid=None, layout="plain",
           seq=None, tm=512, tn=1024):
    M, K = a.shape
    N = w.shape[1]
    tn = min(tn, N)
    grid = (N // tn, M // tm)
    in_specs = [pl.BlockSpec((tm, K), lambda j, i: (i, 0)),
                pl.BlockSpec((K, tn), lambda j, i: (0, j))]
    args = [a, w]
    if resid is not None:
        in_specs.append(pl.BlockSpec((tm, tn), lambda j, i: (i, j)))
        args.append(resid)
    if layout == "plain":
        out_shape = jax.ShapeDtypeStruct((M, N), out_dtype)
        out_spec = pl.BlockSpec((tm, tn), lambda j, i: (i, j))
    else:
        nb = seq // tm
        nh = tn // HEAD_DIM
        if layout == "heads":
            out_shape = jax.ShapeDtypeStruct((M // seq, N // HEAD_DIM, seq, HEAD_DIM), out_dtype)
            out_spec = pl.BlockSpec((1, nh, tm, HEAD_DIM), lambda j, i: (i // nb, j, i % nb, 0))
        else:
            out_shape = jax.ShapeDtypeStruct(
                (M // seq, N // HEAD_DIM, seq // LANE, HEAD_DIM, LANE), out_dtype)
            out_spec = pl.BlockSpec((1, nh, tm // LANE, HEAD_DIM, LANE),
                                    lambda j, i: (i // nb, j, i % nb, 0, 0))
    kern = functools.partial(_mm_kernel, act=act, scale=scale, has_resid=resid is not None,
                             layout=layout)
    return pl.pallas_call(
        kern, grid=grid, in_specs=in_specs, out_specs=out_spec, out_shape=out_shape,
        compiler_params=_params(("parallel", "parallel")),
    )(*args)


def _gate_kernel(x_ref, g_ref, w_ref, lng_ref, lnb_ref, ws_ref, bs_ref, y_ref):
    tm = x_ref.shape[0]
    a = _rms(x_ref[...], g_ref[...]).astype(BF16)
    z = _gelu(jnp.dot(a, w_ref[...], preferred_element_type=F32))
    u = z[:, :D_MODEL]
    v = z[:, D_MODEL:]
    mu = jnp.mean(v, axis=-1, keepdims=True)
    vc = v - mu
    vn = vc * lax.rsqrt(jnp.mean(vc * vc, axis=-1, keepdims=True) + EPS)
    vn = (vn * lng_ref[...] + lnb_ref[...]).astype(BF16)
    gd = D_MODEL // GM_GROUPS
    for c in range(tm // GM_CHUNK):
        rows = slice(c * GM_CHUNK, (c + 1) * GM_CHUNK)
        for g in range(GM_GROUPS):
            cols = slice(g * gd, (g + 1) * gd)
            sv = jnp.dot(ws_ref[g], vn[rows, cols], preferred_element_type=F32)
            y_ref[rows, cols] = (u[rows, cols] * (sv + bs_ref[:, cols])).astype(y_ref.dtype)


def gmlp_gate(x, g, w_in, ln_g, ln_b, ws_masked, bs_full, tm=512):
    T = x.shape[0]
    return pl.pallas_call(
        _gate_kernel,
        grid=(T // tm,),
        in_specs=[pl.BlockSpec((tm, D_MODEL), lambda i: (i, 0)),
                  pl.BlockSpec((1, D_MODEL), lambda i: (0, 0)),
                  pl.BlockSpec((D_MODEL, 2 * D_MODEL), lambda i: (0, 0), pipeline_mode=pl.Buffered(1)),
                  pl.BlockSpec((1, D_MODEL), lambda i: (0, 0)),
                  pl.BlockSpec((1, D_MODEL), lambda i: (0, 0)),
                  pl.BlockSpec((GM_GROUPS, GM_CHUNK, GM_CHUNK), lambda i: (0, 0, 0)),
                  pl.BlockSpec((GM_CHUNK, D_MODEL), lambda i: (0, 0))],
        out_specs=pl.BlockSpec((tm, D_MODEL), lambda i: (i, 0)),
        out_shape=jax.ShapeDtypeStruct((T, D_MODEL), BF16),
        compiler_params=_params(("parallel",)),
    )(x, g.reshape(1, D_MODEL), w_in, ln_g.reshape(1, D_MODEL), ln_b.reshape(1, D_MODEL), ws_masked,
      bs_full)


def _compress_kernel(x_ref, pe_ref, w1_ref, w2_ref, o_ref, ot_ref):
    x = x_ref[0]
    half = CMP_STRIDE * HEAD_DIM
    xa = (x + pe_ref[0, 0:1, :]).astype(BF16)
    xb = (x + pe_ref[0, 1:2, :]).astype(BF16)
    a = jnp.dot(xa, w1_ref[0, :half, :], preferred_element_type=F32)
    b = jnp.dot(xb, w1_ref[0, half:, :], preferred_element_type=F32)
    n = a.shape[0]
    pre = a + pltpu.roll(b, n - 1, axis=0)
    out = jnp.dot(_gelu(pre).astype(BF16), w2_ref[0], preferred_element_type=F32)
    o_ref[0] = out.astype(o_ref.dtype)
    ot_ref[0] = out.T.astype(ot_ref.dtype)


def compress_kv(kvc, pe2, w1, w2):
    B, H2, S, _ = kvc.shape
    n_grp = S // CMP_STRIDE
    x = kvc.reshape(B * H2, n_grp, CMP_STRIDE * HEAD_DIM)
    sel = lambda i: ((i % H2) // N_KV_HEADS, 0, 0)
    return pl.pallas_call(
        _compress_kernel,
        grid=(B * H2,),
        in_specs=[pl.BlockSpec((1, n_grp, CMP_STRIDE * HEAD_DIM), lambda i: (i, 0, 0)),
                  pl.BlockSpec((1, 2, CMP_STRIDE * HEAD_DIM), sel),
                  pl.BlockSpec((1, CMP_LEN * HEAD_DIM, HEAD_DIM), sel),
                  pl.BlockSpec((1, HEAD_DIM, HEAD_DIM), sel)],
        out_specs=[pl.BlockSpec((1, n_grp, HEAD_DIM), lambda i: (i, 0, 0)),
                   pl.BlockSpec((1, HEAD_DIM, n_grp), lambda i: (i, 0, 0))],
        out_shape=[jax.ShapeDtypeStruct((B * H2, n_grp, HEAD_DIM), BF16),
                   jax.ShapeDtypeStruct((B * H2, HEAD_DIM, n_grp), BF16)],
        compiler_params=_params(("parallel",)),
    )(x, pe2, w1, w2)


def _nsa_kernel(q_ref, kc_ref, vct_ref, ksa_ref, vst_ref, kw_ref, vwt_ref, gt_ref, ovt_ref, wm_ref,
                o_ref, qaug_ref, bsel_ref, s_a, s_b):
    nq = GQA * Q_BLK
    qb = pl.program_id(2)
    q0 = qb * Q_BLK
    q_t = q_ref[0].reshape(nq, HEAD_DIM).astype(F32).T.astype(BF16)
    t_lane = q0 + (lax.broadcasted_iota(jnp.int32, (1, nq), 1) & (Q_BLK - 1))

    n_c = kc_ref.shape[2]

    def compressed(n):
        s = jnp.dot(kc_ref[0, 0, :n, :], q_t, preferred_element_type=F32)
        c_end = lax.broadcasted_iota(jnp.int32, (n, 1), 0) * CMP_STRIDE + (CMP_LEN - 1)
        s = jnp.where(c_end <= t_lane, s, NEG)
        m = jnp.max(s, axis=0, keepdims=True)
        p = jnp.exp2(s - m)
        l = jnp.sum(p, axis=0, keepdims=True)
        p = p * jnp.where(m > 0.5 * NEG, 1.0 / l, 0.0)
        o = jnp.dot(vct_ref[0, 0, :, :n], p.astype(BF16), preferred_element_type=F32)
        psum = p[:, 0:Q_BLK]
        for h in range(1, GQA):
            psum = psum + p[:, h * Q_BLK:(h + 1) * Q_BLK]
        p_hi = psum.astype(BF16)
        p_lo = (psum - p_hi.astype(F32)).astype(BF16)
        ov = ovt_ref[:, :n]
        return o, (jnp.dot(ov, p_hi, preferred_element_type=F32)
                   + jnp.dot(ov, p_lo, preferred_element_type=F32))

    n_bkt = 4 if n_c % (4 * LANE) == 0 else 1
    bkt = n_c // n_bkt
    if n_bkt == 1:
        o_cmp, imp = compressed(n_c)
    else:
        o_cmp, imp = lax.switch((q0 + Q_BLK - 1) // (bkt * CMP_STRIDE),
                                [functools.partial(compressed, bkt * (i + 1)) for i in range(n_bkt)])

    n_sel = imp.shape[0]
    blk = lax.broadcasted_iota(jnp.int32, (n_sel, Q_BLK), 0)
    t_q = q0 + lax.broadcasted_iota(jnp.int32, (n_sel, Q_BLK), 1)
    dist = (t_q // SEL_BLK) - blk
    forced = (blk == 0) | ((dist >= 0) & (dist < SEL_LOCAL))
    score = jnp.where(forced, -jnp.inf, jnp.where(blk * SEL_BLK <= t_q, imp, NEG))
    blk_f = blk.astype(F32)
    for _ in range(SEL_TOPN - 1 - SEL_LOCAL):
        top = jnp.max(score, axis=0, keepdims=True)
        first = jnp.min(jnp.where(score == top, blk_f, float(n_sel)), axis=0, keepdims=True)
        score = jnp.where(blk_f == first, -jnp.inf, score)
    picked = score == -jnp.inf
    bsel_ref[...] = jnp.where(picked, 0.0, NEG)
    bias = jnp.where(picked & (blk * SEL_BLK < q0), 0.0, NEG)

    n_half = qaug_ref.shape[0]
    if n_sel < LANE:
        bias = jnp.concatenate([bias, jnp.zeros((LANE - n_sel, Q_BLK), F32)], axis=0)
    for hf in range(n_half):
        bq = bias[hf * LANE:(hf + 1) * LANE, :].astype(BF16)
        qaug_ref[hf] = jnp.concatenate([q_t, jnp.concatenate([bq] * GQA, axis=1)], axis=0)

    wk = WINDOW + Q_BLK
    kstart = pl.multiple_of(jnp.maximum(q0 - WINDOW, 0), LANE)
    sw = jnp.dot(kw_ref[0, 0, pl.ds(kstart, wk), :], q_t, preferred_element_type=F32)
    sw = sw + jnp.concatenate([wm_ref[0]] * GQA, axis=1)
    m_w = jnp.max(sw, axis=0, keepdims=True)
    p_w = jnp.exp2((sw - m_w).astype(BF16))
    vw = jnp.concatenate([vwt_ref[0, 0, kstart // LANE + j] for j in range(wk // LANE)], axis=1)
    vw = jnp.concatenate([vw, jnp.ones((ONES_ROWS, wk), BF16)], axis=0)
    acc_w = jnp.dot(vw, p_w, preferred_element_type=F32)
    o_win = acc_w[:HEAD_DIM] * (1.0 / acc_w[HEAD_DIM:HEAD_DIM + 1])

    kpos = q0 + lax.broadcasted_iota(jnp.int32, (Q_BLK, 1), 0)
    s_d = jnp.dot(ksa_ref[0, 0, pl.ds(pl.multiple_of(q0, Q_BLK), Q_BLK), :HEAD_DIM], q_t,
                  preferred_element_type=F32)
    parts = []
    for j in range(Q_BLK // SEL_BLK):
        b_row = bsel_ref[pl.ds(q0 // SEL_BLK + j, 1), :]
        parts.append(s_d[j * SEL_BLK:(j + 1) * SEL_BLK, :] + jnp.concatenate([b_row] * GQA, axis=1))
    s_d = jnp.where(kpos <= t_lane, jnp.concatenate(parts, axis=0), NEG)
    m_d = jnp.max(s_d, axis=0, keepdims=True)
    p_d = jnp.exp2(s_d - m_d)
    v_d = jnp.concatenate([vst_ref[0, 0, q0 // LANE + j] for j in range(Q_BLK // LANE)], axis=1)
    init = (m_d, jnp.sum(p_d, axis=0, keepdims=True),
            jnp.dot(v_d, p_d.astype(BF16), preferred_element_type=F32))

    bpt = SEL_TILE // SEL_BLK
    vpt = SEL_TILE // LANE

    def scores(kt):
        k0 = pl.multiple_of(kt * SEL_TILE, SEL_TILE)
        return jnp.dot(ksa_ref[0, 0, pl.ds(k0, SEL_TILE), :], qaug_ref[(kt * bpt) // LANE],
                       preferred_element_type=F32)

    def softmax_pv(kt, st, carry):
        m_i, l_i, acc = carry
        m_new = jnp.maximum(m_i, jnp.max(st, axis=0, keepdims=True))
        alpha = jnp.exp2(m_i - m_new)
        pt = jnp.exp2(st - m_new)
        l_new = alpha * l_i + jnp.sum(pt, axis=0, keepdims=True)
        vt = jnp.concatenate([vst_ref[0, 0, kt * vpt + j] for j in range(vpt)], axis=1)
        acc = alpha * acc + jnp.dot(vt, pt.astype(BF16), preferred_element_type=F32)
        return m_new, l_new, acc

    def trip(j, carry):
        st = s_a[...]
        s_b[...] = scores(2 * j + 1)
        carry = softmax_pv(2 * j, st, carry)
        st = s_b[...]
        s_a[...] = scores(2 * j + 2)
        return softmax_pv(2 * j + 1, st, carry)

    n_tiles = (q0 + SEL_TILE - 1) // SEL_TILE
    n_trips = jnp.maximum((n_tiles + 1) // 2, 1)
    s_a[...] = scores(0)
    carry = lax.fori_loop(0, n_trips - 1, trip, init)
    last = 2 * n_trips - 1
    st = s_a[...]
    s_b[...] = scores(last)
    carry = softmax_pv(last - 1, st, carry)
    _, l_s, acc_s = softmax_pv(last, s_b[...], carry)
    o_sel = acc_s * (1.0 / l_s)

    for h in range(GQA):
        cols = slice(h * Q_BLK, (h + 1) * Q_BLK)
        o_t = (gt_ref[0, 0, 0, h:h + 1, :] * o_cmp[:, cols]
               + gt_ref[0, 0, 1, h:h + 1, :] * o_sel[:, cols]
               + gt_ref[0, 0, 2, h:h + 1, :] * o_win[:, cols])
        o_ref[0, :, h * HEAD_DIM:(h + 1) * HEAD_DIM] = o_t.T.astype(o_ref.dtype)


def _window_mask(S):
    wk = WINDOW + Q_BLK
    i = np.arange(wk)[:, None]
    qi = np.arange(Q_BLK)[None, :]
    out = []
    for v in range(WINDOW // Q_BLK + 1):
        t = Q_BLK * v + qi if v < WINDOW // Q_BLK else WINDOW + qi
        out.append(np.where((i <= t) & (i > t - WINDOW), 0.0, NEG))
    return jnp.asarray(np.stack(out), dtype=F32)


def nsa_attention_core(q, kc, vct, ks_aug, kw, vswt, gates_t, ov_t):
    B, _, S, _ = q.shape
    n_c = S // CMP_STRIDE
    n_sel = S // SEL_BLK
    nq = GQA * Q_BLK
    hk = N_KV_HEADS
    wk = WINDOW + Q_BLK
    nv = WINDOW // Q_BLK
    once = pl.Buffered(1)
    return pl.pallas_call(
        _nsa_kernel,
        grid=(B, hk, S // Q_BLK),
        in_specs=[
            pl.BlockSpec((1, GQA, Q_BLK, HEAD_DIM), lambda b, h, i: (b, h, i, 0)),
            pl.BlockSpec((1, 1, n_c, HEAD_DIM), lambda b, h, i: (b, h, 0, 0)),
            pl.BlockSpec((1, 1, HEAD_DIM, n_c), lambda b, h, i: (b, hk + h, 0, 0)),
            pl.BlockSpec((1, 1, S, 2 * HEAD_DIM), lambda b, h, i: (b, h, 0, 0), pipeline_mode=once),
            pl.BlockSpec((1, 1, S // LANE, HEAD_DIM, LANE), lambda b, h, i: (b, h, 0, 0, 0),
                         pipeline_mode=once),
            pl.BlockSpec((1, 1, S, HEAD_DIM), lambda b, h, i: (b, hk + h, 0, 0), pipeline_mode=once),
            pl.BlockSpec((1, 1, S // LANE, HEAD_DIM, LANE), lambda b, h, i: (b, hk + h, 0, 0, 0),
                         pipeline_mode=once),
            pl.BlockSpec((1, 1, 3, GQA, Q_BLK), lambda b, h, i: (b, h, 0, 0, i)),
            pl.BlockSpec((n_sel, n_c), lambda b, h, i: (0, 0)),
            pl.BlockSpec((1, wk, Q_BLK), lambda b, h, i: (jnp.minimum(i, nv), 0, 0)),
        ],
        out_specs=pl.BlockSpec((1, Q_BLK, GQA * HEAD_DIM), lambda b, h, i: (b, i, h)),
        out_shape=jax.ShapeDtypeStruct((B, S, N_HEADS * HEAD_DIM), BF16),
        scratch_shapes=[pltpu.VMEM((max(1, n_sel // LANE), 2 * HEAD_DIM, nq), BF16),
                        pltpu.VMEM((n_sel, Q_BLK), F32),
                        pltpu.VMEM((SEL_TILE, nq), F32), pltpu.VMEM((SEL_TILE, nq), F32)],
        compiler_params=_params(("parallel", "parallel", "arbitrary")),
    )(q, kc, vct, ks_aug, vswt, kw, vswt, gates_t, ov_t, _window_mask(S))


def _moe_kernel(nused_ref, bexp_ref, rtok_ref, h_hbm, wg_ref, wu_ref, wd_ref, y_ref,
                x_even, x_odd, wg_b, wu_b, wd_b, sem):
    i = pl.program_id(0)
    n_used = nused_ref[0]
    blk = x_even.shape[0]
    bufs = (x_even, x_odd)

    def row_copy(b, r, slot):
        tok = rtok_ref[b * blk + r]
        return pltpu.make_async_copy(h_hbm.at[pl.ds(tok, 1)], bufs[slot].at[pl.ds(r, 1)],
                                     sem.at[slot])

    def wait_block(slot):
        pltpu.make_async_copy(h_hbm.at[pl.ds(0, blk)], bufs[slot], sem.at[slot]).wait()

    @pl.when(i == 0)
    def _():
        def body(r, c):
            row_copy(0, r, 0).start()
            return c
        lax.fori_loop(0, blk, body, 0, unroll=8)

    @pl.when((i < n_used) & ((i == 0) | (bexp_ref[i] != bexp_ref[jnp.maximum(i - 1, 0)])))
    def _():
        wg_b[...] = wg_ref[0, 0].astype(BF16)
        wu_b[...] = wu_ref[0, 0].astype(BF16)
        wd_b[...] = wd_ref[0, 0].astype(BF16)

    for slot in range(2):
        @pl.when((i < n_used) & (i % 2 == slot))
        def _(slot=slot):
            wait_block(slot)
            for r in range(blk):
                row_copy(i + 1, r, 1 - slot).start(priority=r % 2)
            x = jnp.concatenate(_unpack_rows(bufs[slot][...]), axis=1).astype(BF16)
            gate = jnp.dot(x, wg_b[...], preferred_element_type=F32)
            up = jnp.dot(x, wu_b[...], preferred_element_type=F32)
            hid = (gate * jax.nn.sigmoid(gate) * up).astype(BF16)
            y_ref[...] = _pack_rows(jnp.dot(hid, wd_b[...], preferred_element_type=F32))

        @pl.when((i == n_used) & (i % 2 == slot))
        def _(slot=slot):
            wait_block(slot)

    @pl.when(i >= n_used)
    def _():
        y_ref[...] = jnp.zeros_like(y_ref)


def moe_experts(xn, n_used, block_exp, row_tok, wg, wu, wd, layer, blk=MOE_BLK):
    n_rows = row_tok.shape[0]
    n_steps = n_rows // blk
    grid_spec = pltpu.PrefetchScalarGridSpec(
        num_scalar_prefetch=3,
        grid=(n_steps,),
        in_specs=[
            pl.BlockSpec(memory_space=pl.ANY),
            pl.BlockSpec((1, 1, D_MODEL, D_EXPERT), lambda i, nu, be, rt: (layer, be[i], 0, 0)),
            pl.BlockSpec((1, 1, D_MODEL, D_EXPERT), lambda i, nu, be, rt: (layer, be[i], 0, 0)),
            pl.BlockSpec((1, 1, D_EXPERT, D_MODEL), lambda i, nu, be, rt: (layer, be[i], 0, 0)),
        ],
        out_specs=pl.BlockSpec((blk, D_MODEL // 2), lambda i, nu, be, rt: (i, 0)),
        scratch_shapes=[pltpu.VMEM((blk, D_MODEL // 2), jnp.uint32),
                        pltpu.VMEM((blk, D_MODEL // 2), jnp.uint32),
                        pltpu.VMEM((D_MODEL, D_EXPERT), BF16), pltpu.VMEM((D_MODEL, D_EXPERT), BF16),
                        pltpu.VMEM((D_EXPERT, D_MODEL), BF16), pltpu.SemaphoreType.DMA((2,))],
    )
    return pl.pallas_call(
        _moe_kernel, grid_spec=grid_spec,
        out_shape=jax.ShapeDtypeStruct((n_rows, D_MODEL // 2), jnp.uint32),
        compiler_params=_params(("arbitrary",)),
    )(n_used, block_exp, row_tok, xn, wg, wu, wd)


def _combine_kernel(pos_ref, h_ref, w_ref, g_ref, y_hbm, *rest):
    *o_refs, n_ref, y_even, y_odd, sem = rest
    i = pl.program_id(0)
    n = pl.num_programs(0)
    tm = h_ref.shape[0]
    bufs = (y_even, y_odd)

    def row_copy(b, j, k, slot):
        row = pos_ref[(b * tm + j) * TOP_K + k]
        return pltpu.make_async_copy(y_hbm.at[pl.ds(row, 1)], bufs[slot].at[pl.ds(k * tm + j, 1)],
                                     sem.at[slot])

    @pl.when(i == 0)
    def _():
        def body(j, c):
            for k in range(TOP_K):
                row_copy(0, j, k, 0).start()
            return c
        lax.fori_loop(0, tm, body, 0, unroll=4)

    def step(slot, prefetch):
        pltpu.make_async_copy(y_hbm.at[pl.ds(0, TOP_K * tm)], bufs[slot], sem.at[slot]).wait()
        if prefetch:
            for j in range(tm):
                for k in range(TOP_K):
                    row_copy(i + 1, j, k, 1 - slot).start(priority=k)
        half = D_MODEL // 2
        lo, hi = h_ref[:, :half], h_ref[:, half:]
        for k in range(TOP_K):
            y_lo, y_hi = _unpack_rows(bufs[slot][k * tm:(k + 1) * tm, :])
            lo = lo + y_lo * w_ref[:, k:k + 1]
            hi = hi + y_hi * w_ref[:, k:k + 1]
        out = jnp.concatenate([lo, hi], axis=1)
        for o_ref in o_refs:
            o_ref[...] = out
        n_ref[...] = _rms(out, g_ref[...]).astype(n_ref.dtype)

    for slot in range(2):
        for prefetch in (True, False):
            cond = (i % 2 == slot) & ((i + 1 < n) if prefetch else (i + 1 == n))
            pl.when(cond)(functools.partial(step, slot, prefetch))


def moe_combine(h, y, pos, weight, g_next, norm_dtype, keep_sum, tm=256):
    T = h.shape[0]
    n_out = 2 if keep_sum else 1
    grid_spec = pltpu.PrefetchScalarGridSpec(
        num_scalar_prefetch=1,
        grid=(T // tm,),
        in_specs=[
            pl.BlockSpec((tm, D_MODEL), lambda i, p: (i, 0)),
            pl.BlockSpec((tm, TOP_K), lambda i, p: (i, 0)),
            pl.BlockSpec((1, D_MODEL), lambda i, p: (0, 0)),
            pl.BlockSpec(memory_space=pl.ANY),
        ],
        out_specs=[pl.BlockSpec((tm, D_MODEL), lambda i, p: (i, 0))] * n_out,
        scratch_shapes=[pltpu.VMEM((TOP_K * tm, D_MODEL // 2), jnp.uint32),
                        pltpu.VMEM((TOP_K * tm, D_MODEL // 2), jnp.uint32),
                        pltpu.SemaphoreType.DMA((2,))],
    )
    return pl.pallas_call(
        _combine_kernel, grid_spec=grid_spec,
        out_shape=[jax.ShapeDtypeStruct((T, D_MODEL), F32)] * (n_out - 1)
        + [jax.ShapeDtypeStruct((T, D_MODEL), norm_dtype)],
        compiler_params=_params(("arbitrary",)),
    )(pos.reshape(-1), h, weight, g_next.reshape(1, D_MODEL), y)


def _dispatch(expert, T, blk):
    onehot = (expert[:, None] == jnp.arange(N_EXPERTS, dtype=jnp.int32)[None, :]).astype(jnp.int32)
    csum = jnp.cumsum(onehot, axis=0)
    rank = jnp.sum(csum * onehot, axis=1) - 1
    counts = csum[-1]
    padded = (counts + blk - 1) // blk * blk
    pad_end = jnp.cumsum(padded)
    pad_start = pad_end - padded
    pos = (jnp.sum(pad_start[None, :] * onehot, axis=1) + rank).astype(jnp.int32)
    n_rows = T * TOP_K + (N_EXPERTS + 1) * blk
    token = jnp.repeat(jnp.arange(T, dtype=jnp.int32), TOP_K)
    row_tok = jnp.zeros((n_rows,), jnp.int32).at[pos].set(token, unique_indices=True)
    n_blocks = n_rows // blk
    starts = jnp.arange(n_blocks, dtype=jnp.int32) * blk
    block_exp = jnp.minimum(jnp.sum((pad_end[None, :] <= starts[:, None]).astype(jnp.int32), axis=1),
                            N_EXPERTS - 1).astype(jnp.int32)
    n_used = (pad_end[-1] // blk).astype(jnp.int32).reshape(1)
    return pos.reshape(T, TOP_K), row_tok, block_exp, n_used


def _moe_layer(h, norm_g, w_group, b_group, w_exp, b_exp, wg, wu, wd, layer, g_next, norm_dtype,
               keep_sum=True):
    T = h.shape[0]
    wr = jnp.zeros((D_MODEL, LANE), F32)
    wr = wr.at[:, :N_GROUPS].set(w_group).at[:, N_GROUPS:N_GROUPS + N_EXPERTS].set(w_exp)
    br = jnp.zeros((LANE,), F32).at[:N_GROUPS].set(b_group).at[N_GROUPS:N_GROUPS + N_EXPERTS].set(b_exp)
    e_t, w_t, xn = router(h, norm_g, wr, br.reshape(LANE, 1))
    expert = e_t[:TOP_K].T.reshape(-1)
    weight = w_t[:TOP_K].T
    pos, row_tok, block_exp, n_used = _dispatch(expert, T, MOE_BLK)
    y = moe_experts(xn, n_used, block_exp, row_tok, wg, wu, wd, layer)
    return moe_combine(h, y, pos, weight, g_next, norm_dtype, keep_sum)


def _gmlp_layer(h, norm_g, w_in, ln_g, ln_b, w_s, b_s, w_out):
    causal = jnp.tril(jnp.ones((GM_CHUNK, GM_CHUNK), w_s.dtype))
    bs_full = jnp.repeat(b_s.T, D_MODEL // GM_GROUPS, axis=1)
    y = gmlp_gate(h, norm_g, w_in.astype(BF16), ln_g, ln_b, (w_s * causal).astype(BF16), bs_full)
    return matmul(y, w_out.astype(BF16), out_dtype=F32, resid=h)


def _overlap_t(S):
    n_c = S // CMP_STRIDE
    n_sel = S // SEL_BLK
    ci = np.arange(n_c)[None, :] * CMP_STRIDE
    sj = np.arange(n_sel)[:, None] * SEL_BLK
    ov = (ci < sj + SEL_BLK) & (ci + CMP_LEN > sj) & (np.arange(n_c)[None, :] < n_c - 1)
    return jnp.asarray(ov.astype(np.float32), dtype=BF16)


def _nsa_layer(h, hn, B, S, w_in, ck_pe, ck_w1, ck_w2, cv_pe, cv_w1, cv_w2, w_out):
    qd = N_HEADS * HEAD_DIM
    kvd = N_KV_HEADS * HEAD_DIM
    w = w_in.astype(BF16)
    o = [qd + i * kvd for i in range(7)]
    w_q, w_kc, w_vc, w_ks, w_vs, w_kw, w_vw = (
        w[:, :o[0]], w[:, o[0]:o[1]], w[:, o[1]:o[2]], w[:, o[2]:o[3]], w[:, o[3]:o[4]],
        w[:, o[4]:o[5]], w[:, o[5]:o[6]])
    w_g = jnp.zeros((D_MODEL, LANE), BF16).at[:, :3 * N_HEADS].set(w[:, o[6]:])
    q = matmul(hn, w_q, out_dtype=BF16, scale=HEAD_DIM ** -0.5 * LOG2E, layout="heads", seq=S)
    kvc = matmul(hn, jnp.concatenate([w_kc, w_vc], 1), out_dtype=F32, layout="heads", seq=S)
    ksw = matmul(hn, jnp.concatenate([w_ks, w_kw], 1), out_dtype=BF16, layout="heads", seq=S)
    vswt = matmul(hn, jnp.concatenate([w_vs, w_vw], 1), out_dtype=BF16, layout="heads_t", seq=S)
    gates = matmul(hn, w_g, out_dtype=F32, act="sigmoid")[:, :3 * N_HEADS]
    gates_t = gates.reshape(B, S, N_KV_HEADS, GQA, 3).transpose(0, 2, 4, 3, 1)

    half = CMP_STRIDE
    pe2 = jnp.stack([jnp.stack([pe[:half].reshape(-1), pe[half:].reshape(-1)])
                     for pe in (ck_pe, cv_pe)])
    w1 = jnp.stack([ck_w1, cv_w1]).astype(BF16)
    w2 = jnp.stack([ck_w2, cv_w2]).astype(BF16)
    cmp, cmp_t = compress_kv(kvc, pe2, w1, w2)
    n_c = S // CMP_STRIDE
    cmp = cmp.reshape(B, 2 * N_KV_HEADS, n_c, HEAD_DIM)
    cmp_t = cmp_t.reshape(B, 2 * N_KV_HEADS, HEAD_DIM, n_c)
    blk_id = (np.arange(S) // SEL_BLK) % LANE
    onehot = jnp.asarray(blk_id[:, None] == np.arange(LANE)[None, :], dtype=BF16)
    ks_aug = jnp.concatenate(
        [ksw[:, :N_KV_HEADS], jnp.broadcast_to(onehot, (B, N_KV_HEADS, S, LANE))], axis=-1)
    o_att = nsa_attention_core(q, cmp, cmp_t, ks_aug, ksw, vswt, gates_t, _overlap_t(S))
    return matmul(o_att.reshape(B * S, qd), w_out.astype(BF16), out_dtype=F32, resid=h)


def kernel(x, norm_mix, norm_ffn, norm_final, a_w_in, a_ln_g, a_ln_b, a_w_s, a_b_s, a_w_out,
           b_w_in, b_ck_pe, b_ck_w1, b_ck_w2, b_cv_pe, b_cv_w1, b_cv_w2, b_w_out,
           r_w_group, r_b_group, r_w_exp, r_b_exp, e_w_gate, e_w_up, e_w_down):
    B, S, _ = x.shape
    h = x.reshape(B * S, D_MODEL)
    h = _gmlp_layer(h, norm_mix[0], a_w_in[0], a_ln_g[0], a_ln_b[0], a_w_s[0], a_b_s[0], a_w_out[0])
    h, hn = _moe_layer(h, norm_ffn[0], r_w_group[0], r_b_group[0], r_w_exp[0], r_b_exp[0],
                       e_w_gate, e_w_up, e_w_down, 0, norm_mix[1], BF16)
    h = _nsa_layer(h, hn, B, S, b_w_in[0], b_ck_pe[0], b_ck_w1[0], b_ck_w2[0],
                   b_cv_pe[0], b_cv_w1[0], b_cv_w2[0], b_w_out[0])
    (out,) = _moe_layer(h, norm_ffn[1], r_w_group[1], r_b_group[1], r_w_exp[1], r_b_exp[1],
                        e_w_gate, e_w_up, e_w_down, 1, norm_final, F32, keep_sum=False)
    return out.reshape(B, S, D_MODEL)
```

```python
import functools

import numpy as np
import jax
import jax.numpy as jnp
from jax import lax
from jax.experimental import pallas as pl
from jax.experimental.pallas import tpu as pltpu

F32 = jnp.float32
BF16 = jnp.bfloat16

D_MODEL = 2048
LANE = 128
GM_GROUPS = 8
GM_CHUNK = 128
HEAD_DIM = 128
N_HEADS = 16
N_KV_HEADS = 4
GQA = N_HEADS // N_KV_HEADS
CMP_LEN = 32
CMP_STRIDE = 16
SEL_BLK = 64
SEL_TOPN = 16
SEL_LOCAL = 2
WINDOW = 512
Q_BLK = 256
FORCE = 1.0e4
N_GROUPS = 4
EXP_PER_GROUP = 8
N_EXPERTS = N_GROUPS * EXP_PER_GROUP
TOP_K = 2
D_EXPERT = 512
EPS = 1e-6
NEG = -1e30
LOG2E = 1.4426950408889634
ONES_ROWS = 16
SEL_TILE = 512
MOE_BLK = 512
VMEM_LIMIT = 56 * 1024 * 1024

_NT = (((1,), (1,)), ((), ()))


def _params(sem):
    return pltpu.CompilerParams(dimension_semantics=sem, vmem_limit_bytes=VMEM_LIMIT)


def _gelu(x):
    return 0.5 * x * (1.0 + jnp.tanh(0.7978845608028654 * (x + 0.044715 * (x * x * x))))


def _rms(x, g):
    y = x * lax.rsqrt(jnp.mean(x * x, axis=-1, keepdims=True) + EPS)
    return y * g


def _pack_rows(x):
    half = x.shape[1] // 2
    bits = lambda v: lax.bitcast_convert_type(v.astype(BF16).astype(F32), jnp.uint32)
    return (bits(x[:, :half]) >> 16) | (bits(x[:, half:]) & jnp.uint32(0xFFFF0000))


def _unpack_rows(w):
    lo = lax.bitcast_convert_type(w << 16, F32)
    hi = lax.bitcast_convert_type(w & jnp.uint32(0xFFFF0000), F32)
    return lo, hi


def _rmsnorm_kernel(x_ref, g_ref, o_ref):
    o_ref[...] = _rms(x_ref[...], g_ref[...]).astype(o_ref.dtype)


def rmsnorm(x, g, out_dtype=BF16, tm=512):
    T = x.shape[0]
    return pl.pallas_call(
        _rmsnorm_kernel,
        grid=(T // tm,),
        in_specs=[pl.BlockSpec((tm, D_MODEL), lambda i: (i, 0)),
                  pl.BlockSpec((1, D_MODEL), lambda i: (0, 0))],
        out_specs=pl.BlockSpec((tm, D_MODEL), lambda i: (i, 0)),
        out_shape=jax.ShapeDtypeStruct((T, D_MODEL), out_dtype),
        compiler_params=_params(("parallel",)),
    )(x, g.reshape(1, D_MODEL))


def _router_kernel(x_ref, g_ref, wh_ref, wl_ref, br_ref, e_ref, w_ref, xn_ref):
    y = _rms(x_ref[...], g_ref[...])
    xn_ref[...] = _pack_rows(y)
    y_hi = y.astype(BF16)
    y_lo = (y - y_hi.astype(F32)).astype(BF16)
    lg = (jnp.dot(y_hi, wh_ref[...], preferred_element_type=F32)
          + jnp.dot(y_hi, wl_ref[...], preferred_element_type=F32)
          + jnp.dot(y_lo, wh_ref[...], preferred_element_type=F32))
    lt = lg.T + br_ref[...]
    row = lambda i: lt[i:i + 1, :]

    def softmax(rows):
        m = functools.reduce(jnp.maximum, rows)
        e = [jnp.exp(r - m) for r in rows]
        s = functools.reduce(lambda a, b: a + b, e)
        return [v / s for v in e]

    def top1(vals, skip=None):
        best = jnp.full_like(vals[0], -1.0)
        idx = jnp.zeros(vals[0].shape, jnp.int32)
        for k, v in enumerate(vals):
            if skip is not None:
                v = jnp.where(skip == k, -1.0, v)
            upd = v > best
            best = jnp.where(upd, v, best)
            idx = jnp.where(upd, k, idx)
        return best, idx

    g_w, g_idx = top1(softmax([row(j) for j in range(N_GROUPS)]))
    e_logits = []
    for k in range(EXP_PER_GROUP):
        v = row(N_GROUPS + k)
        for gg in range(1, N_GROUPS):
            v = jnp.where(g_idx == gg, row(N_GROUPS + gg * EXP_PER_GROUP + k), v)
        e_logits.append(v)
    e_prob = softmax(e_logits)
    w1, i1 = top1(e_prob)
    w2, i2 = top1(e_prob, skip=i1)
    tot = w1 + w2
    zi = jnp.zeros((6, lt.shape[1]), jnp.int32)
    e_ref[...] = jnp.concatenate([g_idx * EXP_PER_GROUP + i1, g_idx * EXP_PER_GROUP + i2, zi], axis=0)
    w_ref[...] = jnp.concatenate([g_w * (w1 / tot), g_w * (w2 / tot), zi.astype(F32)], axis=0)


def router(x, g, wr, br, tm=512):
    T = x.shape[0]
    wr_hi = wr.astype(BF16)
    wr_lo = (wr - wr_hi.astype(F32)).astype(BF16)
    return pl.pallas_call(
        _router_kernel,
        grid=(T // tm,),
        in_specs=[pl.BlockSpec((tm, D_MODEL), lambda i: (i, 0)),
                  pl.BlockSpec((1, D_MODEL), lambda i: (0, 0)),
                  pl.BlockSpec((D_MODEL, LANE), lambda i: (0, 0)),
                  pl.BlockSpec((D_MODEL, LANE), lambda i: (0, 0)),
                  pl.BlockSpec((LANE, 1), lambda i: (0, 0))],
        out_specs=[pl.BlockSpec((8, tm), lambda i: (0, i)),
                   pl.BlockSpec((8, tm), lambda i: (0, i)),
                   pl.BlockSpec((tm, D_MODEL // 2), lambda i: (i, 0))],
        out_shape=[jax.ShapeDtypeStruct((8, T), jnp.int32),
                   jax.ShapeDtypeStruct((8, T), F32),
                   jax.ShapeDtypeStruct((T, D_MODEL // 2), jnp.uint32)],
        compiler_params=_params(("parallel",)),
    )(x, g.reshape(1, D_MODEL), wr_hi, wr_lo, br)


def _mm_kernel(*refs, act, scale, has_resid, layout):
    a_ref, w_ref = refs[0], refs[1]
    o_ref = refs[-2] if layout == "groups" else refs[-1]
    acc = jnp.dot(a_ref[...], w_ref[...], preferred_element_type=F32)
    if scale is not None:
        acc = acc * scale
    if act == "gelu":
        acc = _gelu(acc)
    elif act == "sigmoid":
        acc = jax.nn.sigmoid(acc)
    if has_resid:
        acc = acc + refs[2][...]
    tm, tn = acc.shape
    if layout == "plain":
        o_ref[...] = acc.astype(o_ref.dtype)
    elif layout == "heads":
        for h in range(tn // HEAD_DIM):
            o_ref[0, h] = acc[:, h * HEAD_DIM:(h + 1) * HEAD_DIM].astype(o_ref.dtype)
    elif layout == "groups":
        tmp = refs[-1]
        for h in range(tn // HEAD_DIM):
            tmp[...] = acc[:, h * HEAD_DIM:(h + 1) * HEAD_DIM]
            for t in range(CMP_STRIDE):
                o_ref[0, h, :, t * HEAD_DIM:(t + 1) * HEAD_DIM] = (
                    tmp[pl.ds(t, tm // CMP_STRIDE, stride=CMP_STRIDE), :].astype(o_ref.dtype))
    else:
        for h in range(tn // HEAD_DIM):
            for c in range(tm // LANE):
                blk = acc[c * LANE:(c + 1) * LANE, h * HEAD_DIM:(h + 1) * HEAD_DIM]
                o_ref[0, h, c] = blk.T.astype(o_ref.dtype)


def matmul(a, w, *, out_dtype, act=None, scale=None, resid=None, layout="plain",
           seq=None, tm=512, tn=1024):
    M, K = a.shape
    N = w.shape[1]
    tn = min(tn, N)
    grid = (N // tn, M // tm)
    in_specs = [pl.BlockSpec((tm, K), lambda j, i: (i, 0)),
                pl.BlockSpec((K, tn), lambda j, i: (0, j))]
    args = [a, w]
    if resid is not None:
        in_specs.append(pl.BlockSpec((tm, tn), lambda j, i: (i, j)))
        args.append(resid)
    scratch = []
    if layout == "plain":
        out_shape = jax.ShapeDtypeStruct((M, N), out_dtype)
        out_spec = pl.BlockSpec((tm, tn), lambda j, i: (i, j))
    else:
        nb = seq // tm
        nh = tn // HEAD_DIM
        if layout == "heads":
            out_shape = jax.ShapeDtypeStruct((M // seq, N // HEAD_DIM, seq, HEAD_DIM), out_dtype)
            out_spec = pl.BlockSpec((1, nh, tm, HEAD_DIM), lambda j, i: (i // nb, j, i % nb, 0))
        elif layout == "groups":
            grp = CMP_STRIDE * HEAD_DIM
            out_shape = jax.ShapeDtypeStruct((M // seq, N // HEAD_DIM, seq // CMP_STRIDE, grp), out_dtype)
            out_spec = pl.BlockSpec((1, nh, tm // CMP_STRIDE, grp), lambda j, i: (i // nb, j, i % nb, 0))
            scratch = [pltpu.VMEM((tm, HEAD_DIM), F32)]
        else:
            out_shape = jax.ShapeDtypeStruct(
                (M // seq, N // HEAD_DIM, seq // LANE, HEAD_DIM, LANE), out_dtype)
            out_spec = pl.BlockSpec((1, nh, tm // LANE, HEAD_DIM, LANE),
                                    lambda j, i: (i // nb, j, i % nb, 0, 0))
    kern = functools.partial(_mm_kernel, act=act, scale=scale, has_resid=resid is not None,
                             layout=layout)
    return pl.pallas_call(
        kern, grid=grid, in_specs=in_specs, out_specs=out_spec, out_shape=out_shape,
        scratch_shapes=scratch, compiler_params=_params(("parallel", "parallel")),
    )(*args)


def _gate_kernel(x_ref, g_ref, w_ref, lng_ref, lnb_ref, ws_ref, bs_ref, y_ref):
    tm = x_ref.shape[0]
    a = _rms(x_ref[...], g_ref[...]).astype(BF16)
    z = _gelu(jnp.dot(a, w_ref[...], preferred_element_type=F32))
    u = z[:, :D_MODEL]
    v = z[:, D_MODEL:]
    mu = jnp.mean(v, axis=-1, keepdims=True)
    vc = v - mu
    vn = vc * lax.rsqrt(jnp.mean(vc * vc, axis=-1, keepdims=True) + EPS)
    vn = (vn * lng_ref[...] + lnb_ref[...]).astype(BF16)
    gd = D_MODEL // GM_GROUPS
    for c in range(tm // GM_CHUNK):
        rows = slice(c * GM_CHUNK, (c + 1) * GM_CHUNK)
        for g in range(GM_GROUPS):
            cols = slice(g * gd, (g + 1) * gd)
            sv = jnp.dot(ws_ref[g], vn[rows, cols], preferred_element_type=F32)
            y_ref[rows, cols] = (u[rows, cols] * (sv + bs_ref[:, cols])).astype(y_ref.dtype)


def gmlp_gate(x, g, w_in, ln_g, ln_b, ws_masked, bs_full, tm=512):
    T = x.shape[0]
    return pl.pallas_call(
        _gate_kernel,
        grid=(T // tm,),
        in_specs=[pl.BlockSpec((tm, D_MODEL), lambda i: (i, 0)),
                  pl.BlockSpec((1, D_MODEL), lambda i: (0, 0)),
                  pl.BlockSpec((D_MODEL, 2 * D_MODEL), lambda i: (0, 0), pipeline_mode=pl.Buffered(1)),
                  pl.BlockSpec((1, D_MODEL), lambda i: (0, 0)),
                  pl.BlockSpec((1, D_MODEL), lambda i: (0, 0)),
                  pl.BlockSpec((GM_GROUPS, GM_CHUNK, GM_CHUNK), lambda i: (0, 0, 0)),
                  pl.BlockSpec((GM_CHUNK, D_MODEL), lambda i: (0, 0))],
        out_specs=pl.BlockSpec((tm, D_MODEL), lambda i: (i, 0)),
        out_shape=jax.ShapeDtypeStruct((T, D_MODEL), BF16),
        compiler_params=_params(("parallel",)),
    )(x, g.reshape(1, D_MODEL), w_in, ln_g.reshape(1, D_MODEL), ln_b.reshape(1, D_MODEL), ws_masked,
      bs_full)


def _compress_kernel(x_ref, pe_ref, w1_ref, w2_ref, o_ref, ot_ref):
    x = x_ref[0]
    half = CMP_STRIDE * HEAD_DIM
    xa = (x + pe_ref[0, 0:1, :]).astype(BF16)
    xb = (x + pe_ref[0, 1:2, :]).astype(BF16)
    a = jnp.dot(xa, w1_ref[0, :half, :], preferred_element_type=F32)
    b = jnp.dot(xb, w1_ref[0, half:, :], preferred_element_type=F32)
    n = a.shape[0]
    pre = a + pltpu.roll(b, n - 1, axis=0)
    out = jnp.dot(_gelu(pre).astype(BF16), w2_ref[0], preferred_element_type=F32)
    o_ref[0] = out.astype(o_ref.dtype)
    ot_ref[0] = out.T.astype(ot_ref.dtype)


def compress_kv(kvc, pe2, w1, w2):
    B, H2, n_grp, _ = kvc.shape
    x = kvc.reshape(B * H2, n_grp, CMP_STRIDE * HEAD_DIM)
    sel = lambda i: ((i % H2) // N_KV_HEADS, 0, 0)
    return pl.pallas_call(
        _compress_kernel,
        grid=(B * H2,),
        in_specs=[pl.BlockSpec((1, n_grp, CMP_STRIDE * HEAD_DIM), lambda i: (i, 0, 0)),
                  pl.BlockSpec((1, 2, CMP_STRIDE * HEAD_DIM), sel),
                  pl.BlockSpec((1, CMP_LEN * HEAD_DIM, HEAD_DIM), sel),
                  pl.BlockSpec((1, HEAD_DIM, HEAD_DIM), sel)],
        out_specs=[pl.BlockSpec((1, n_grp, HEAD_DIM), lambda i: (i, 0, 0)),
                   pl.BlockSpec((1, HEAD_DIM, n_grp), lambda i: (i, 0, 0))],
        out_shape=[jax.ShapeDtypeStruct((B * H2, n_grp, HEAD_DIM), BF16),
                   jax.ShapeDtypeStruct((B * H2, HEAD_DIM, n_grp), BF16)],
        compiler_params=_params(("parallel",)),
    )(x, pe2, w1, w2)


def _nsa_kernel(q_ref, kc_ref, vct_ref, ksa_ref, vst_ref, kw_ref, vwt_ref, gt_ref, ovt_ref, wm_ref,
                o_ref, qaug_ref, bsel_ref, s_a, s_b):
    nq = GQA * Q_BLK
    qb = pl.program_id(2)
    q0 = qb * Q_BLK
    q_t = q_ref[0].reshape(nq, HEAD_DIM).astype(F32).T.astype(BF16)
    t_lane = q0 + (lax.broadcasted_iota(jnp.int32, (1, nq), 1) & (Q_BLK - 1))

    n_c = kc_ref.shape[2]

    def compressed(n):
        s = jnp.dot(kc_ref[0, 0, :n, :], q_t, preferred_element_type=F32)
        c_end = lax.broadcasted_iota(jnp.int32, (n, 1), 0) * CMP_STRIDE + (CMP_LEN - 1)
        s = jnp.where(c_end <= t_lane, s, NEG)
        m = jnp.max(s, axis=0, keepdims=True)
        p = jnp.exp2(s - m)
        l = jnp.sum(p, axis=0, keepdims=True)
        p = p * jnp.where(m > 0.5 * NEG, 1.0 / l, 0.0)
        o = jnp.dot(vct_ref[0, 0, :, :n], p.astype(BF16), preferred_element_type=F32)
        psum = p[:, 0:Q_BLK]
        for h in range(1, GQA):
            psum = psum + p[:, h * Q_BLK:(h + 1) * Q_BLK]
        p_hi = psum.astype(BF16)
        p_lo = (psum - p_hi.astype(F32)).astype(BF16)
        ov = ovt_ref[:, :n]
        return o, (jnp.dot(ov, p_hi, preferred_element_type=F32)
                   + jnp.dot(ov, p_lo, preferred_element_type=F32))

    n_bkt = 4 if n_c % (4 * LANE) == 0 else 1
    bkt = n_c // n_bkt
    if n_bkt == 1:
        o_cmp, imp = compressed(n_c)
    else:
        o_cmp, imp = lax.switch((q0 + Q_BLK - 1) // (bkt * CMP_STRIDE),
                                [functools.partial(compressed, bkt * (i + 1)) for i in range(n_bkt)])

    n_sel = imp.shape[0]
    blk = lax.broadcasted_iota(jnp.int32, (n_sel, Q_BLK), 0)
    t_q = q0 + lax.broadcasted_iota(jnp.int32, (n_sel, Q_BLK), 1)
    dist = (t_q // SEL_BLK) - blk
    forced = (blk == 0) | ((dist >= 0) & (dist < SEL_LOCAL))
    score = jnp.where(forced, -jnp.inf, jnp.where(blk * SEL_BLK <= t_q, imp, NEG))
    blk_f = blk.astype(F32)
    for _ in range(SEL_TOPN - 1 - SEL_LOCAL):
        top = jnp.max(score, axis=0, keepdims=True)
        first = jnp.min(jnp.where(score == top, blk_f, float(n_sel)), axis=0, keepdims=True)
        score = jnp.where(blk_f == first, -jnp.inf, score)
    picked = score == -jnp.inf
    bsel_ref[...] = jnp.where(picked, 0.0, NEG)
    bias = jnp.where(picked & (blk * SEL_BLK < q0), 0.0, NEG)

    n_half = qaug_ref.shape[0]
    if n_sel < LANE:
        bias = jnp.concatenate([bias, jnp.zeros((LANE - n_sel, Q_BLK), F32)], axis=0)
    for hf in range(n_half):
        bq = bias[hf * LANE:(hf + 1) * LANE, :].astype(BF16)
        qaug_ref[hf] = jnp.concatenate([q_t, jnp.concatenate([bq] * GQA, axis=1)], axis=0)

    wk = WINDOW + Q_BLK
    kstart = pl.multiple_of(jnp.maximum(q0 - WINDOW, 0), LANE)
    sw = jnp.dot(kw_ref[0, 0, pl.ds(kstart, wk), :], q_t, preferred_element_type=F32)
    sw = sw + jnp.concatenate([wm_ref[0]] * GQA, axis=1)
    m_w = jnp.max(sw, axis=0, keepdims=True)
    p_w = jnp.exp2((sw - m_w).astype(BF16))
    vw = jnp.concatenate([vwt_ref[0, 0, kstart // LANE + j] for j in range(wk // LANE)], axis=1)
    vw = jnp.concatenate([vw, jnp.ones((ONES_ROWS, wk), BF16)], axis=0)
    acc_w = jnp.dot(vw, p_w, preferred_element_type=F32)
    o_win = acc_w[:HEAD_DIM] * (1.0 / acc_w[HEAD_DIM:HEAD_DIM + 1])

    kpos = q0 + lax.broadcasted_iota(jnp.int32, (Q_BLK, 1), 0)
    s_d = jnp.dot(ksa_ref[0, 0, pl.ds(pl.multiple_of(q0, Q_BLK), Q_BLK), :HEAD_DIM], q_t,
                  preferred_element_type=F32)
    parts = []
    for j in range(Q_BLK // SEL_BLK):
        b_row = bsel_ref[pl.ds(q0 // SEL_BLK + j, 1), :]
        parts.append(s_d[j * SEL_BLK:(j + 1) * SEL_BLK, :] + jnp.concatenate([b_row] * GQA, axis=1))
    s_d = jnp.where(kpos <= t_lane, jnp.concatenate(parts, axis=0), NEG)
    m_d = jnp.max(s_d, axis=0, keepdims=True)
    p_d = jnp.exp2(s_d - m_d)
    v_d = jnp.concatenate([vst_ref[0, 0, q0 // LANE + j] for j in range(Q_BLK // LANE)], axis=1)
    init = (m_d, jnp.sum(p_d, axis=0, keepdims=True),
            jnp.dot(v_d, p_d.astype(BF16), preferred_element_type=F32))

    bpt = SEL_TILE // SEL_BLK
    vpt = SEL_TILE // LANE

    def scores(kt):
        k0 = pl.multiple_of(kt * SEL_TILE, SEL_TILE)
        return jnp.dot(ksa_ref[0, 0, pl.ds(k0, SEL_TILE), :], qaug_ref[(kt * bpt) // LANE],
                       preferred_element_type=F32)

    def softmax_pv(kt, st, carry):
        m_i, l_i, acc = carry
        m_new = jnp.maximum(m_i, jnp.max(st, axis=0, keepdims=True))
        alpha = jnp.exp2(m_i - m_new)
        pt = jnp.exp2(st - m_new)
        l_new = alpha * l_i + jnp.sum(pt, axis=0, keepdims=True)
        vt = jnp.concatenate([vst_ref[0, 0, kt * vpt + j] for j in range(vpt)], axis=1)
        acc = alpha * acc + jnp.dot(vt, pt.astype(BF16), preferred_element_type=F32)
        return m_new, l_new, acc

    def trip(j, carry):
        st = s_a[...]
        s_b[...] = scores(2 * j + 1)
        carry = softmax_pv(2 * j, st, carry)
        st = s_b[...]
        s_a[...] = scores(2 * j + 2)
        return softmax_pv(2 * j + 1, st, carry)

    n_tiles = (q0 + SEL_TILE - 1) // SEL_TILE
    n_trips = jnp.maximum((n_tiles + 1) // 2, 1)
    s_a[...] = scores(0)
    carry = lax.fori_loop(0, n_trips - 1, trip, init)
    last = 2 * n_trips - 1
    st = s_a[...]
    s_b[...] = scores(last)
    carry = softmax_pv(last - 1, st, carry)
    _, l_s, acc_s = softmax_pv(last, s_b[...], carry)
    o_sel = acc_s * (1.0 / l_s)

    for h in range(GQA):
        cols = slice(h * Q_BLK, (h + 1) * Q_BLK)
        o_t = (gt_ref[0, 0, 0, h:h + 1, :] * o_cmp[:, cols]
               + gt_ref[0, 0, 1, h:h + 1, :] * o_sel[:, cols]
               + gt_ref[0, 0, 2, h:h + 1, :] * o_win[:, cols])
        o_ref[0, :, h * HEAD_DIM:(h + 1) * HEAD_DIM] = o_t.T.astype(o_ref.dtype)


def _window_mask(S):
    wk = WINDOW + Q_BLK
    i = np.arange(wk)[:, None]
    qi = np.arange(Q_BLK)[None, :]
    out = []
    for v in range(WINDOW // Q_BLK + 1):
        t = Q_BLK * v + qi if v < WINDOW // Q_BLK else WINDOW + qi
        out.append(np.where((i <= t) & (i > t - WINDOW), 0.0, NEG))
    return jnp.asarray(np.stack(out), dtype=F32)


def nsa_attention_core(q, kc, vct, ks_aug, kw, vswt, gates_t, ov_t):
    B, _, S, _ = q.shape
    n_c = S // CMP_STRIDE
    n_sel = S // SEL_BLK
    nq = GQA * Q_BLK
    hk = N_KV_HEADS
    wk = WINDOW + Q_BLK
    nv = WINDOW // Q_BLK
    once = pl.Buffered(1)
    return pl.pallas_call(
        _nsa_kernel,
        grid=(B, hk, S // Q_BLK),
        in_specs=[
            pl.BlockSpec((1, GQA, Q_BLK, HEAD_DIM), lambda b, h, i: (b, h, i, 0)),
            pl.BlockSpec((1, 1, n_c, HEAD_DIM), lambda b, h, i: (b, h, 0, 0)),
            pl.BlockSpec((1, 1, HEAD_DIM, n_c), lambda b, h, i: (b, hk + h, 0, 0)),
            pl.BlockSpec((1, 1, S, 2 * HEAD_DIM), lambda b, h, i: (b, h, 0, 0), pipeline_mode=once),
            pl.BlockSpec((1, 1, S // LANE, HEAD_DIM, LANE), lambda b, h, i: (b, h, 0, 0, 0),
                         pipeline_mode=once),
            pl.BlockSpec((1, 1, S, HEAD_DIM), lambda b, h, i: (b, hk + h, 0, 0), pipeline_mode=once),
            pl.BlockSpec((1, 1, S // LANE, HEAD_DIM, LANE), lambda b, h, i: (b, hk + h, 0, 0, 0),
                         pipeline_mode=once),
            pl.BlockSpec((1, 1, 3, GQA, Q_BLK), lambda b, h, i: (b, h, 0, 0, i)),
            pl.BlockSpec((n_sel, n_c), lambda b, h, i: (0, 0)),
            pl.BlockSpec((1, wk, Q_BLK), lambda b, h, i: (jnp.minimum(i, nv), 0, 0)),
        ],
        out_specs=pl.BlockSpec((1, Q_BLK, GQA * HEAD_DIM), lambda b, h, i: (b, i, h)),
        out_shape=jax.ShapeDtypeStruct((B, S, N_HEADS * HEAD_DIM), BF16),
        scratch_shapes=[pltpu.VMEM((max(1, n_sel // LANE), 2 * HEAD_DIM, nq), BF16),
                        pltpu.VMEM((n_sel, Q_BLK), F32),
                        pltpu.VMEM((SEL_TILE, nq), F32), pltpu.VMEM((SEL_TILE, nq), F32)],
        compiler_params=_params(("parallel", "parallel", "arbitrary")),
    )(q, kc, vct, ks_aug, vswt, kw, vswt, gates_t, ov_t, _window_mask(S))


def _moe_kernel(nused_ref, bexp_ref, rtok_ref, h_hbm, wg_ref, wu_ref, wd_ref, y_ref,
                x_even, x_odd, wg_b, wu_b, wd_b, sem):
    i = pl.program_id(0)
    n_used = nused_ref[0]
    blk = x_even.shape[0]
    bufs = (x_even, x_odd)

    def row_copy(b, r, slot):
        tok = rtok_ref[b * blk + r]
        return pltpu.make_async_copy(h_hbm.at[pl.ds(tok, 1)], bufs[slot].at[pl.ds(r, 1)],
                                     sem.at[slot])

    def wait_block(slot):
        pltpu.make_async_copy(h_hbm.at[pl.ds(0, blk)], bufs[slot], sem.at[slot]).wait()

    @pl.when(i == 0)
    def _():
        def body(r, c):
            row_copy(0, r, 0).start()
            return c
        lax.fori_loop(0, blk, body, 0, unroll=8)

    @pl.when((i < n_used) & ((i == 0) | (bexp_ref[i] != bexp_ref[jnp.maximum(i - 1, 0)])))
    def _():
        wg_b[...] = wg_ref[0, 0].astype(BF16)
        wu_b[...] = wu_ref[0, 0].astype(BF16)
        wd_b[...] = wd_ref[0, 0].astype(BF16)

    for slot in range(2):
        @pl.when((i < n_used) & (i % 2 == slot))
        def _(slot=slot):
            wait_block(slot)
            for r in range(blk):
                row_copy(i + 1, r, 1 - slot).start(priority=r % 2)
            x = jnp.concatenate(_unpack_rows(bufs[slot][...]), axis=1).astype(BF16)
            gate = jnp.dot(x, wg_b[...], preferred_element_type=F32)
            up = jnp.dot(x, wu_b[...], preferred_element_type=F32)
            hid = (gate * jax.nn.sigmoid(gate) * up).astype(BF16)
            y_ref[...] = _pack_rows(jnp.dot(hid, wd_b[...], preferred_element_type=F32))

        @pl.when((i == n_used) & (i % 2 == slot))
        def _(slot=slot):
            wait_block(slot)

    @pl.when(i >= n_used)
    def _():
        y_ref[...] = jnp.zeros_like(y_ref)


def moe_experts(xn, n_used, block_exp, row_tok, wg, wu, wd, layer, blk=MOE_BLK):
    n_rows = row_tok.shape[0]
    n_steps = n_rows // blk
    grid_spec = pltpu.PrefetchScalarGridSpec(
        num_scalar_prefetch=3,
        grid=(n_steps,),
        in_specs=[
            pl.BlockSpec(memory_space=pl.ANY),
            pl.BlockSpec((1, 1, D_MODEL, D_EXPERT), lambda i, nu, be, rt: (layer, be[i], 0, 0)),
            pl.BlockSpec((1, 1, D_MODEL, D_EXPERT), lambda i, nu, be, rt: (layer, be[i], 0, 0)),
            pl.BlockSpec((1, 1, D_EXPERT, D_MODEL), lambda i, nu, be, rt: (layer, be[i], 0, 0)),
        ],
        out_specs=pl.BlockSpec((blk, D_MODEL // 2), lambda i, nu, be, rt: (i, 0)),
        scratch_shapes=[pltpu.VMEM((blk, D_MODEL // 2), jnp.uint32),
                        pltpu.VMEM((blk, D_MODEL // 2), jnp.uint32),
                        pltpu.VMEM((D_MODEL, D_EXPERT), BF16), pltpu.VMEM((D_MODEL, D_EXPERT), BF16),
                        pltpu.VMEM((D_EXPERT, D_MODEL), BF16), pltpu.SemaphoreType.DMA((2,))],
    )
    return pl.pallas_call(
        _moe_kernel, grid_spec=grid_spec,
        out_shape=jax.ShapeDtypeStruct((n_rows, D_MODEL // 2), jnp.uint32),
        compiler_params=_params(("arbitrary",)),
    )(n_used, block_exp, row_tok, xn, wg, wu, wd)


def _combine_kernel(pos_ref, h_ref, w_ref, g_ref, y_hbm, *rest):
    *o_refs, n_ref, y_even, y_odd, sem = rest
    i = pl.program_id(0)
    n = pl.num_programs(0)
    tm = h_ref.shape[0]
    bufs = (y_even, y_odd)

    def row_copy(b, j, k, slot):
        row = pos_ref[(b * tm + j) * TOP_K + k]
        return pltpu.make_async_copy(y_hbm.at[pl.ds(row, 1)], bufs[slot].at[pl.ds(k * tm + j, 1)],
                                     sem.at[slot])

    @pl.when(i == 0)
    def _():
        def body(j, c):
            for k in range(TOP_K):
                row_copy(0, j, k, 0).start()
            return c
        lax.fori_loop(0, tm, body, 0, unroll=4)

    def step(slot, prefetch):
        pltpu.make_async_copy(y_hbm.at[pl.ds(0, TOP_K * tm)], bufs[slot], sem.at[slot]).wait()
        if prefetch:
            for j in range(tm):
                for k in range(TOP_K):
                    row_copy(i + 1, j, k, 1 - slot).start(priority=k)
        half = D_MODEL // 2
        lo, hi = h_ref[:, :half], h_ref[:, half:]
        for k in range(TOP_K):
            y_lo, y_hi = _unpack_rows(bufs[slot][k * tm:(k + 1) * tm, :])
            lo = lo + y_lo * w_ref[:, k:k + 1]
            hi = hi + y_hi * w_ref[:, k:k + 1]
        out = jnp.concatenate([lo, hi], axis=1)
        for o_ref in o_refs:
            o_ref[...] = out
        n_ref[...] = _rms(out, g_ref[...]).astype(n_ref.dtype)

    for slot in range(2):
        for prefetch in (True, False):
            cond = (i % 2 == slot) & ((i + 1 < n) if prefetch else (i + 1 == n))
            pl.when(cond)(functools.partial(step, slot, prefetch))


def moe_combine(h, y, pos, weight, g_next, norm_dtype, keep_sum, tm=256):
    T = h.shape[0]
    n_out = 2 if keep_sum else 1
    grid_spec = pltpu.PrefetchScalarGridSpec(
        num_scalar_prefetch=1,
        grid=(T // tm,),
        in_specs=[
            pl.BlockSpec((tm, D_MODEL), lambda i, p: (i, 0)),
            pl.BlockSpec((tm, TOP_K), lambda i, p: (i, 0)),
            pl.BlockSpec((1, D_MODEL), lambda i, p: (0, 0)),
            pl.BlockSpec(memory_space=pl.ANY),
        ],
        out_specs=[pl.BlockSpec((tm, D_MODEL), lambda i, p: (i, 0))] * n_out,
        scratch_shapes=[pltpu.VMEM((TOP_K * tm, D_MODEL // 2), jnp.uint32),
                        pltpu.VMEM((TOP_K * tm, D_MODEL // 2), jnp.uint32),
                        pltpu.SemaphoreType.DMA((2,))],
    )
    return pl.pallas_call(
        _combine_kernel, grid_spec=grid_spec,
        out_shape=[jax.ShapeDtypeStruct((T, D_MODEL), F32)] * (n_out - 1)
        + [jax.ShapeDtypeStruct((T, D_MODEL), norm_dtype)],
        compiler_params=_params(("arbitrary",)),
    )(pos.reshape(-1), h, weight, g_next.reshape(1, D_MODEL), y)


def _dispatch(expert, T, blk):
    onehot = (expert[:, None] == jnp.arange(N_EXPERTS, dtype=jnp.int32)[None, :]).astype(jnp.int32)
    csum = jnp.cumsum(onehot, axis=0)
    rank = jnp.sum(csum * onehot, axis=1) - 1
    counts = csum[-1]
    padded = (counts + blk - 1) // blk * blk
    pad_end = jnp.cumsum(padded)
    pad_start = pad_end - padded
    pos = (jnp.sum(pad_start[None, :] * onehot, axis=1) + rank).astype(jnp.int32)
    n_rows = T * TOP_K + (N_EXPERTS + 1) * blk
    token = jnp.repeat(jnp.arange(T, dtype=jnp.int32), TOP_K)
    row_tok = jnp.zeros((n_rows,), jnp.int32).at[pos].set(token, unique_indices=True)
    n_blocks = n_rows // blk
    starts = jnp.arange(n_blocks, dtype=jnp.int32) * blk
    block_exp = jnp.minimum(jnp.sum((pad_end[None, :] <= starts[:, None]).astype(jnp.int32), axis=1),
                            N_EXPERTS - 1).astype(jnp.int32)
    n_used = (pad_end[-1] // blk).astype(jnp.int32).reshape(1)
    return pos.reshape(T, TOP_K), row_tok, block_exp, n_used


def _moe_layer(h, norm_g, w_group, b_group, w_exp, b_exp, wg, wu, wd, layer, g_next, norm_dtype,
               keep_sum=True):
    T = h.shape[0]
    wr = jnp.zeros((D_MODEL, LANE), F32)
    wr = wr.at[:, :N_GROUPS].set(w_group).at[:, N_GROUPS:N_GROUPS + N_EXPERTS].set(w_exp)
    br = jnp.zeros((LANE,), F32).at[:N_GROUPS].set(b_group).at[N_GROUPS:N_GROUPS + N_EXPERTS].set(b_exp)
    e_t, w_t, xn = router(h, norm_g, wr, br.reshape(LANE, 1))
    expert = e_t[:TOP_K].T.reshape(-1)
    weight = w_t[:TOP_K].T
    pos, row_tok, block_exp, n_used = _dispatch(expert, T, MOE_BLK)
    y = moe_experts(xn, n_used, block_exp, row_tok, wg, wu, wd, layer)
    return moe_combine(h, y, pos, weight, g_next, norm_dtype, keep_sum)


def _gmlp_layer(h, norm_g, w_in, ln_g, ln_b, w_s, b_s, w_out):
    causal = jnp.tril(jnp.ones((GM_CHUNK, GM_CHUNK), w_s.dtype))
    bs_full = jnp.repeat(b_s.T, D_MODEL // GM_GROUPS, axis=1)
    y = gmlp_gate(h, norm_g, w_in.astype(BF16), ln_g, ln_b, (w_s * causal).astype(BF16), bs_full)
    return matmul(y, w_out.astype(BF16), out_dtype=F32, resid=h)


def _overlap_t(S):
    n_c = S // CMP_STRIDE
    n_sel = S // SEL_BLK
    ci = np.arange(n_c)[None, :] * CMP_STRIDE
    sj = np.arange(n_sel)[:, None] * SEL_BLK
    ov = (ci < sj + SEL_BLK) & (ci + CMP_LEN > sj) & (np.arange(n_c)[None, :] < n_c - 1)
    return jnp.asarray(ov.astype(np.float32), dtype=BF16)


def _nsa_layer(h, hn, B, S, w_in, ck_pe, ck_w1, ck_w2, cv_pe, cv_w1, cv_w2, w_out):
    qd = N_HEADS * HEAD_DIM
    kvd = N_KV_HEADS * HEAD_DIM
    w = w_in.astype(BF16)
    o = [qd + i * kvd for i in range(7)]
    w_q, w_kc, w_vc, w_ks, w_vs, w_kw, w_vw = (
        w[:, :o[0]], w[:, o[0]:o[1]], w[:, o[1]:o[2]], w[:, o[2]:o[3]], w[:, o[3]:o[4]],
        w[:, o[4]:o[5]], w[:, o[5]:o[6]])
    w_g = jnp.zeros((D_MODEL, LANE), BF16).at[:, :3 * N_HEADS].set(w[:, o[6]:])
    q = matmul(hn, w_q, out_dtype=BF16, scale=HEAD_DIM ** -0.5 * LOG2E, layout="heads", seq=S)
    kvc = matmul(hn, jnp.concatenate([w_kc, w_vc], 1), out_dtype=F32, layout="groups", seq=S)
    ksw = matmul(hn, jnp.concatenate([w_ks, w_kw], 1), out_dtype=BF16, layout="heads", seq=S)
    vswt = matmul(hn, jnp.concatenate([w_vs, w_vw], 1), out_dtype=BF16, layout="heads_t", seq=S)
    gates = matmul(hn, w_g, out_dtype=F32, act="sigmoid")[:, :3 * N_HEADS]
    gates_t = gates.reshape(B, S, N_KV_HEADS, GQA, 3).transpose(0, 2, 4, 3, 1)

    half = CMP_STRIDE
    pe2 = jnp.stack([jnp.stack([pe[:half].reshape(-1), pe[half:].reshape(-1)])
                     for pe in (ck_pe, cv_pe)])
    w1 = jnp.stack([ck_w1, cv_w1]).astype(BF16)
    w2 = jnp.stack([ck_w2, cv_w2]).astype(BF16)
    cmp, cmp_t = compress_kv(kvc, pe2, w1, w2)
    n_c = S // CMP_STRIDE
    cmp = cmp.reshape(B, 2 * N_KV_HEADS, n_c, HEAD_DIM)
    cmp_t = cmp_t.reshape(B, 2 * N_KV_HEADS, HEAD_DIM, n_c)
    blk_id = (np.arange(S) // SEL_BLK) % LANE
    onehot = jnp.asarray(blk_id[:, None] == np.arange(LANE)[None, :], dtype=BF16)
    ks_aug = jnp.concatenate(
        [ksw[:, :N_KV_HEADS], jnp.broadcast_to(onehot, (B, N_KV_HEADS, S, LANE))], axis=-1)
    o_att = nsa_attention_core(q, cmp, cmp_t, ks_aug, ksw, vswt, gates_t, _overlap_t(S))
    return matmul(o_att.reshape(B * S, qd), w_out.astype(BF16), out_dtype=F32, resid=h)


def kernel(x, norm_mix, norm_ffn, norm_final, a_w_in, a_ln_g, a_ln_b, a_w_s, a_b_s, a_w_out,
           b_w_in, b_ck_pe, b_ck_w1, b_ck_w2, b_cv_pe, b_cv_w1, b_cv_w2, b_w_out,
           r_w_group, r_b_group, r_w_exp, r_b_exp, e_w_gate, e_w_up, e_w_down):
    B, S, _ = x.shape
    h = x.reshape(B * S, D_MODEL)
    h = _gmlp_layer(h, norm_mix[0], a_w_in[0], a_ln_g[0], a_ln_b[0], a_w_s[0], a_b_s[0], a_w_out[0])
    h, hn = _moe_layer(h, norm_ffn[0], r_w_group[0], r_b_group[0], r_w_exp[0], r_b_exp[0],
                       e_w_gate, e_w_up, e_w_down, 0, norm_mix[1], BF16)
    h = _nsa_layer(h, hn, B, S, b_w_in[0], b_ck_pe[0], b_ck_w1[0], b_ck_w2[0],
                   b_cv_pe[0], b_cv_w1[0], b_cv_w2[0], b_w_out[0])
    (out,) = _moe_layer(h, norm_ffn[1], r_w_group[1], r_b_group[1], r_w_exp[1], r_b_exp[1],
                        e_w_gate, e_w_up, e_w_down, 1, norm_final, F32, keep_sum=False)
    return out.reshape(B, S, D_MODEL)
```

```python
import functools

import numpy as np
import jax
import jax.numpy as jnp
from jax import lax
from jax.experimental import pallas as pl
from jax.experimental.pallas import tpu as pltpu

F32 = jnp.float32
BF16 = jnp.bfloat16

D_MODEL = 2048
LANE = 128
GM_GROUPS = 8
GM_CHUNK = 128
HEAD_DIM = 128
N_HEADS = 16
N_KV_HEADS = 4
GQA = N_HEADS // N_KV_HEADS
CMP_LEN = 32
CMP_STRIDE = 16
SEL_BLK = 64
SEL_TOPN = 16
SEL_LOCAL = 2
WINDOW = 512
Q_BLK = 256
FORCE = 1.0e4
N_GROUPS = 4
EXP_PER_GROUP = 8
N_EXPERTS = N_GROUPS * EXP_PER_GROUP
TOP_K = 2
D_EXPERT = 512
EPS = 1e-6
NEG = -1e30
LOG2E = 1.4426950408889634
ONES_ROWS = 16
SEL_TILE = 512
CMP_BUCKETS = 8
MOE_BLK = 512
VMEM_LIMIT = 56 * 1024 * 1024

_NT = (((1,), (1,)), ((), ()))


def _params(sem):
    return pltpu.CompilerParams(dimension_semantics=sem, vmem_limit_bytes=VMEM_LIMIT)


def _gelu(x):
    return 0.5 * x * (1.0 + jnp.tanh(0.7978845608028654 * (x + 0.044715 * (x * x * x))))


def _rms(x, g):
    y = x * lax.rsqrt(jnp.mean(x * x, axis=-1, keepdims=True) + EPS)
    return y * g


def _pack_rows(x):
    half = x.shape[1] // 2
    bits = lambda v: lax.bitcast_convert_type(v.astype(BF16).astype(F32), jnp.uint32)
    return (bits(x[:, :half]) >> 16) | (bits(x[:, half:]) & jnp.uint32(0xFFFF0000))


def _unpack_rows(w):
    lo = lax.bitcast_convert_type(w << 16, F32)
    hi = lax.bitcast_convert_type(w & jnp.uint32(0xFFFF0000), F32)
    return lo, hi


def _rmsnorm_kernel(x_ref, g_ref, o_ref):
    o_ref[...] = _rms(x_ref[...], g_ref[...]).astype(o_ref.dtype)


def rmsnorm(x, g, out_dtype=BF16, tm=512):
    T = x.shape[0]
    return pl.pallas_call(
        _rmsnorm_kernel,
        grid=(T // tm,),
        in_specs=[pl.BlockSpec((tm, D_MODEL), lambda i: (i, 0)),
                  pl.BlockSpec((1, D_MODEL), lambda i: (0, 0))],
        out_specs=pl.BlockSpec((tm, D_MODEL), lambda i: (i, 0)),
        out_shape=jax.ShapeDtypeStruct((T, D_MODEL), out_dtype),
        compiler_params=_params(("parallel",)),
    )(x, g.reshape(1, D_MODEL))


def _router_kernel(x_ref, g_ref, wh_ref, wl_ref, br_ref, e_ref, w_ref, xn_ref):
    y = _rms(x_ref[...], g_ref[...])
    xn_ref[...] = _pack_rows(y)
    y_hi = y.astype(BF16)
    y_lo = (y - y_hi.astype(F32)).astype(BF16)
    lg = (jnp.dot(y_hi, wh_ref[...], preferred_element_type=F32)
          + jnp.dot(y_hi, wl_ref[...], preferred_element_type=F32)
          + jnp.dot(y_lo, wh_ref[...], preferred_element_type=F32))
    lt = lg.T + br_ref[...]
    row = lambda i: lt[i:i + 1, :]

    def softmax(rows):
        m = functools.reduce(jnp.maximum, rows)
        e = [jnp.exp(r - m) for r in rows]
        s = functools.reduce(lambda a, b: a + b, e)
        return [v / s for v in e]

    def top1(vals, skip=None):
        best = jnp.full_like(vals[0], -1.0)
        idx = jnp.zeros(vals[0].shape, jnp.int32)
        for k, v in enumerate(vals):
            if skip is not None:
                v = jnp.where(skip == k, -1.0, v)
            upd = v > best
            best = jnp.where(upd, v, best)
            idx = jnp.where(upd, k, idx)
        return best, idx

    g_w, g_idx = top1(softmax([row(j) for j in range(N_GROUPS)]))
    e_logits = []
    for k in range(EXP_PER_GROUP):
        v = row(N_GROUPS + k)
        for gg in range(1, N_GROUPS):
            v = jnp.where(g_idx == gg, row(N_GROUPS + gg * EXP_PER_GROUP + k), v)
        e_logits.append(v)
    e_prob = softmax(e_logits)
    w1, i1 = top1(e_prob)
    w2, i2 = top1(e_prob, skip=i1)
    tot = w1 + w2
    zi = jnp.zeros((6, lt.shape[1]), jnp.int32)
    e_ref[...] = jnp.concatenate([g_idx * EXP_PER_GROUP + i1, g_idx * EXP_PER_GROUP + i2, zi], axis=0)
    w_ref[...] = jnp.concatenate([g_w * (w1 / tot), g_w * (w2 / tot), zi.astype(F32)], axis=0)


def router(x, g, wr, br, tm=512):
    T = x.shape[0]
    wr_hi = wr.astype(BF16)
    wr_lo = (wr - wr_hi.astype(F32)).astype(BF16)
    return pl.pallas_call(
        _router_kernel,
        grid=(T // tm,),
        in_specs=[pl.BlockSpec((tm, D_MODEL), lambda i: (i, 0)),
                  pl.BlockSpec((1, D_MODEL), lambda i: (0, 0)),
                  pl.BlockSpec((D_MODEL, LANE), lambda i: (0, 0)),
                  pl.BlockSpec((D_MODEL, LANE), lambda i: (0, 0)),
                  pl.BlockSpec((LANE, 1), lambda i: (0, 0))],
        out_specs=[pl.BlockSpec((8, tm), lambda i: (0, i)),
                   pl.BlockSpec((8, tm), lambda i: (0, i)),
                   pl.BlockSpec((tm, D_MODEL // 2), lambda i: (i, 0))],
        out_shape=[jax.ShapeDtypeStruct((8, T), jnp.int32),
                   jax.ShapeDtypeStruct((8, T), F32),
                   jax.ShapeDtypeStruct((T, D_MODEL // 2), jnp.uint32)],
        compiler_params=_params(("parallel",)),
    )(x, g.reshape(1, D_MODEL), wr_hi, wr_lo, br)


def _mm_kernel(*refs, act, scale, has_resid, layout):
    a_ref, w_ref = refs[0], refs[1]
    o_ref = refs[-2] if layout == "groups" else refs[-1]
    acc = jnp.dot(a_ref[...], w_ref[...], preferred_element_type=F32)
    if scale is not None:
        acc = acc * scale
    if act == "gelu":
        acc = _gelu(acc)
    elif act == "sigmoid":
        acc = jax.nn.sigmoid(acc)
    if has_resid:
        acc = acc + refs[2][...]
    tm, tn = acc.shape
    if layout == "plain":
        o_ref[...] = acc.astype(o_ref.dtype)
    elif layout == "heads":
        for h in range(tn // HEAD_DIM):
            o_ref[0, h] = acc[:, h * HEAD_DIM:(h + 1) * HEAD_DIM].astype(o_ref.dtype)
    elif layout == "groups":
        tmp = refs[-1]
        for h in range(tn // HEAD_DIM):
            tmp[...] = acc[:, h * HEAD_DIM:(h + 1) * HEAD_DIM]
            for t in range(CMP_STRIDE):
                o_ref[0, h, :, t * HEAD_DIM:(t + 1) * HEAD_DIM] = (
                    tmp[pl.ds(t, tm // CMP_STRIDE, stride=CMP_STRIDE), :].astype(o_ref.dtype))
    else:
        for h in range(tn // HEAD_DIM):
            for c in range(tm // LANE):
                blk = acc[c * LANE:(c + 1) * LANE, h * HEAD_DIM:(h + 1) * HEAD_DIM]
                o_ref[0, h, c] = blk.T.astype(o_ref.dtype)


def matmul(a, w, *, out_dtype, act=None, scale=None, resid=None, layout="plain",
           seq=None, tm=512, tn=1024):
    M, K = a.shape
    N = w.shape[1]
    tn = min(tn, N)
    grid = (N // tn, M // tm)
    in_specs = [pl.BlockSpec((tm, K), lambda j, i: (i, 0)),
                pl.BlockSpec((K, tn), lambda j, i: (0, j))]
    args = [a, w]
    if resid is not None:
        in_specs.append(pl.BlockSpec((tm, tn), lambda j, i: (i, j)))
        args.append(resid)
    scratch = []
    if layout == "plain":
        out_shape = jax.ShapeDtypeStruct((M, N), out_dtype)
        out_spec = pl.BlockSpec((tm, tn), lambda j, i: (i, j))
    else:
        nb = seq // tm
        nh = tn // HEAD_DIM
        if layout == "heads":
            out_shape = jax.ShapeDtypeStruct((M // seq, N // HEAD_DIM, seq, HEAD_DIM), out_dtype)
            out_spec = pl.BlockSpec((1, nh, tm, HEAD_DIM), lambda j, i: (i // nb, j, i % nb, 0))
        elif layout == "groups":
            grp = CMP_STRIDE * HEAD_DIM
            out_shape = jax.ShapeDtypeStruct((M // seq, N // HEAD_DIM, seq // CMP_STRIDE, grp), out_dtype)
            out_spec = pl.BlockSpec((1, nh, tm // CMP_STRIDE, grp), lambda j, i: (i // nb, j, i % nb, 0))
            scratch = [pltpu.VMEM((tm, HEAD_DIM), F32)]
        else:
            out_shape = jax.ShapeDtypeStruct(
                (M // seq, N // HEAD_DIM, seq // LANE, HEAD_DIM, LANE), out_dtype)
            out_spec = pl.BlockSpec((1, nh, tm // LANE, HEAD_DIM, LANE),
                                    lambda j, i: (i // nb, j, i % nb, 0, 0))
    kern = functools.partial(_mm_kernel, act=act, scale=scale, has_resid=resid is not None,
                             layout=layout)
    return pl.pallas_call(
        kern, grid=grid, in_specs=in_specs, out_specs=out_spec, out_shape=out_shape,
        scratch_shapes=scratch, compiler_params=_params(("parallel", "parallel")),
    )(*args)


def _gate_kernel(x_ref, g_ref, w_ref, lng_ref, lnb_ref, ws_ref, bs_ref, y_ref):
    tm = x_ref.shape[0]
    a = _rms(x_ref[...], g_ref[...]).astype(BF16)
    z = _gelu(jnp.dot(a, w_ref[...], preferred_element_type=F32))
    u = z[:, :D_MODEL]
    v = z[:, D_MODEL:]
    mu = jnp.mean(v, axis=-1, keepdims=True)
    vc = v - mu
    vn = vc * lax.rsqrt(jnp.mean(vc * vc, axis=-1, keepdims=True) + EPS)
    vn = (vn * lng_ref[...] + lnb_ref[...]).astype(BF16)
    gd = D_MODEL // GM_GROUPS
    for c in range(tm // GM_CHUNK):
        rows = slice(c * GM_CHUNK, (c + 1) * GM_CHUNK)
        for g in range(GM_GROUPS):
            cols = slice(g * gd, (g + 1) * gd)
            sv = jnp.dot(ws_ref[g], vn[rows, cols], preferred_element_type=F32)
            y_ref[rows, cols] = (u[rows, cols] * (sv + bs_ref[:, cols])).astype(y_ref.dtype)


def gmlp_gate(x, g, w_in, ln_g, ln_b, ws_masked, bs_full, tm=512):
    T = x.shape[0]
    return pl.pallas_call(
        _gate_kernel,
        grid=(T // tm,),
        in_specs=[pl.BlockSpec((tm, D_MODEL), lambda i: (i, 0)),
                  pl.BlockSpec((1, D_MODEL), lambda i: (0, 0)),
                  pl.BlockSpec((D_MODEL, 2 * D_MODEL), lambda i: (0, 0), pipeline_mode=pl.Buffered(1)),
                  pl.BlockSpec((1, D_MODEL), lambda i: (0, 0)),
                  pl.BlockSpec((1, D_MODEL), lambda i: (0, 0)),
                  pl.BlockSpec((GM_GROUPS, GM_CHUNK, GM_CHUNK), lambda i: (0, 0, 0)),
                  pl.BlockSpec((GM_CHUNK, D_MODEL), lambda i: (0, 0))],
        out_specs=pl.BlockSpec((tm, D_MODEL), lambda i: (i, 0)),
        out_shape=jax.ShapeDtypeStruct((T, D_MODEL), BF16),
        compiler_params=_params(("parallel",)),
    )(x, g.reshape(1, D_MODEL), w_in, ln_g.reshape(1, D_MODEL), ln_b.reshape(1, D_MODEL), ws_masked,
      bs_full)


def _compress_kernel(x_ref, pe_ref, w1_ref, w2_ref, o_ref, ot_ref):
    x = x_ref[0]
    half = CMP_STRIDE * HEAD_DIM
    xa = (x + pe_ref[0, 0:1, :]).astype(BF16)
    xb = (x + pe_ref[0, 1:2, :]).astype(BF16)
    a = jnp.dot(xa, w1_ref[0, :half, :], preferred_element_type=F32)
    b = jnp.dot(xb, w1_ref[0, half:, :], preferred_element_type=F32)
    n = a.shape[0]
    pre = a + pltpu.roll(b, n - 1, axis=0)
    out = jnp.dot(_gelu(pre).astype(BF16), w2_ref[0], preferred_element_type=F32)
    o_ref[0] = out.astype(o_ref.dtype)
    ot_ref[0] = out.T.astype(ot_ref.dtype)


def compress_kv(kvc, pe2, w1, w2):
    B, H2, n_grp, _ = kvc.shape
    x = kvc.reshape(B * H2, n_grp, CMP_STRIDE * HEAD_DIM)
    sel = lambda i: ((i % H2) // N_KV_HEADS, 0, 0)
    return pl.pallas_call(
        _compress_kernel,
        grid=(B * H2,),
        in_specs=[pl.BlockSpec((1, n_grp, CMP_STRIDE * HEAD_DIM), lambda i: (i, 0, 0)),
                  pl.BlockSpec((1, 2, CMP_STRIDE * HEAD_DIM), sel),
                  pl.BlockSpec((1, CMP_LEN * HEAD_DIM, HEAD_DIM), sel),
                  pl.BlockSpec((1, HEAD_DIM, HEAD_DIM), sel)],
        out_specs=[pl.BlockSpec((1, n_grp, HEAD_DIM), lambda i: (i, 0, 0)),
                   pl.BlockSpec((1, HEAD_DIM, n_grp), lambda i: (i, 0, 0))],
        out_shape=[jax.ShapeDtypeStruct((B * H2, n_grp, HEAD_DIM), BF16),
                   jax.ShapeDtypeStruct((B * H2, HEAD_DIM, n_grp), BF16)],
        compiler_params=_params(("parallel",)),
    )(x, pe2, w1, w2)


def _nsa_kernel(q_ref, kc_ref, vct_ref, ksa_ref, vst_ref, kw_ref, vwt_ref, gt_ref, ovt_ref, wm_ref,
                o_ref, qaug_ref, bsel_ref, s_a, s_b):
    nq = GQA * Q_BLK
    qb = pl.program_id(2)
    q0 = qb * Q_BLK
    q_t = q_ref[0].reshape(nq, HEAD_DIM).astype(F32).T.astype(BF16)
    t_lane = q0 + (lax.broadcasted_iota(jnp.int32, (1, nq), 1) & (Q_BLK - 1))

    n_c = kc_ref.shape[2]

    def compressed(n):
        s = jnp.dot(kc_ref[0, 0, :n, :], q_t, preferred_element_type=F32)
        c_end = lax.broadcasted_iota(jnp.int32, (n, 1), 0) * CMP_STRIDE + (CMP_LEN - 1)
        s = jnp.where(c_end <= t_lane, s, NEG)
        m = jnp.max(s, axis=0, keepdims=True)
        p = jnp.exp2(s - m)
        l = jnp.sum(p, axis=0, keepdims=True)
        p = p * jnp.where(m > 0.5 * NEG, 1.0 / l, 0.0)
        o = jnp.dot(vct_ref[0, 0, :, :n], p.astype(BF16), preferred_element_type=F32)
        psum = p[:, 0:Q_BLK]
        for h in range(1, GQA):
            psum = psum + p[:, h * Q_BLK:(h + 1) * Q_BLK]
        p_hi = psum.astype(BF16)
        p_lo = (psum - p_hi.astype(F32)).astype(BF16)
        ov = ovt_ref[:, :n]
        return o, (jnp.dot(ov, p_hi, preferred_element_type=F32)
                   + jnp.dot(ov, p_lo, preferred_element_type=F32))

    n_bkt = CMP_BUCKETS if n_c % (CMP_BUCKETS * LANE) == 0 else 1
    bkt = n_c // n_bkt
    if n_bkt == 1:
        o_cmp, imp = compressed(n_c)
    else:
        o_cmp, imp = lax.switch((q0 + Q_BLK - 1) // (bkt * CMP_STRIDE),
                                [functools.partial(compressed, bkt * (i + 1)) for i in range(n_bkt)])

    n_sel = imp.shape[0]
    blk = lax.broadcasted_iota(jnp.int32, (n_sel, Q_BLK), 0)
    t_q = q0 + lax.broadcasted_iota(jnp.int32, (n_sel, Q_BLK), 1)
    dist = (t_q // SEL_BLK) - blk
    forced = (blk == 0) | ((dist >= 0) & (dist < SEL_LOCAL))
    score = jnp.where(forced, -jnp.inf, jnp.where(blk * SEL_BLK <= t_q, imp, NEG))
    blk_f = blk.astype(F32)
    for _ in range(SEL_TOPN - 1 - SEL_LOCAL):
        top = jnp.max(score, axis=0, keepdims=True)
        first = jnp.min(jnp.where(score == top, blk_f, float(n_sel)), axis=0, keepdims=True)
        score = jnp.where(blk_f == first, -jnp.inf, score)
    picked = score == -jnp.inf
    bsel_ref[...] = jnp.where(picked, 0.0, NEG)
    bias = jnp.where(picked & (blk * SEL_BLK < q0), 0.0, NEG)

    n_half = qaug_ref.shape[0]
    if n_sel < LANE:
        bias = jnp.concatenate([bias, jnp.zeros((LANE - n_sel, Q_BLK), F32)], axis=0)
    for hf in range(n_half):
        bq = bias[hf * LANE:(hf + 1) * LANE, :].astype(BF16)
        qaug_ref[hf] = jnp.concatenate([q_t, jnp.concatenate([bq] * GQA, axis=1)], axis=0)

    wk = WINDOW + Q_BLK
    kstart = pl.multiple_of(jnp.maximum(q0 - WINDOW, 0), LANE)
    sw = jnp.dot(kw_ref[0, 0, pl.ds(kstart, wk), :], q_t, preferred_element_type=F32)
    sw = sw + jnp.concatenate([wm_ref[0]] * GQA, axis=1)
    m_w = jnp.max(sw, axis=0, keepdims=True)
    p_w = jnp.exp2((sw - m_w).astype(BF16))
    vw = jnp.concatenate([vwt_ref[0, 0, kstart // LANE + j] for j in range(wk // LANE)], axis=1)
    vw = jnp.concatenate([vw, jnp.ones((ONES_ROWS, wk), BF16)], axis=0)
    acc_w = jnp.dot(vw, p_w, preferred_element_type=F32)
    o_win = acc_w[:HEAD_DIM] * (1.0 / acc_w[HEAD_DIM:HEAD_DIM + 1])

    kpos = q0 + lax.broadcasted_iota(jnp.int32, (Q_BLK, 1), 0)
    s_d = jnp.dot(ksa_ref[0, 0, pl.ds(pl.multiple_of(q0, Q_BLK), Q_BLK), :HEAD_DIM], q_t,
                  preferred_element_type=F32)
    parts = []
    for j in range(Q_BLK // SEL_BLK):
        b_row = bsel_ref[pl.ds(q0 // SEL_BLK + j, 1), :]
        parts.append(s_d[j * SEL_BLK:(j + 1) * SEL_BLK, :] + jnp.concatenate([b_row] * GQA, axis=1))
    s_d = jnp.where(kpos <= t_lane, jnp.concatenate(parts, axis=0), NEG)
    m_d = jnp.max(s_d, axis=0, keepdims=True)
    p_d = jnp.exp2(s_d - m_d)
    v_d = jnp.concatenate([vst_ref[0, 0, q0 // LANE + j] for j in range(Q_BLK // LANE)], axis=1)
    init = (m_d, jnp.sum(p_d, axis=0, keepdims=True),
            jnp.dot(v_d, p_d.astype(BF16), preferred_element_type=F32))

    bpt = SEL_TILE // SEL_BLK
    vpt = SEL_TILE // LANE

    def scores(kt):
        k0 = pl.multiple_of(kt * SEL_TILE, SEL_TILE)
        return jnp.dot(ksa_ref[0, 0, pl.ds(k0, SEL_TILE), :], qaug_ref[(kt * bpt) // LANE],
                       preferred_element_type=F32)

    def softmax_pv(kt, st, carry):
        m_i, l_i, acc = carry
        m_new = jnp.maximum(m_i, jnp.max(st, axis=0, keepdims=True))
        alpha = jnp.exp2(m_i - m_new)
        pt = jnp.exp2(st - m_new)
        l_new = alpha * l_i + jnp.sum(pt, axis=0, keepdims=True)
        vt = jnp.concatenate([vst_ref[0, 0, kt * vpt + j] for j in range(vpt)], axis=1)
        acc = alpha * acc + jnp.dot(vt, pt.astype(BF16), preferred_element_type=F32)
        return m_new, l_new, acc

    def trip(j, carry):
        st = s_a[...]
        s_b[...] = scores(2 * j + 1)
        carry = softmax_pv(2 * j, st, carry)
        st = s_b[...]
        s_a[...] = scores(2 * j + 2)
        return softmax_pv(2 * j + 1, st, carry)

    n_tiles = (q0 + SEL_TILE - 1) // SEL_TILE
    n_trips = jnp.maximum((n_tiles + 1) // 2, 1)
    s_a[...] = scores(0)
    carry = lax.fori_loop(0, n_trips - 1, trip, init)
    last = 2 * n_trips - 1
    st = s_a[...]
    s_b[...] = scores(last)
    carry = softmax_pv(last - 1, st, carry)
    _, l_s, acc_s = softmax_pv(last, s_b[...], carry)
    o_sel = acc_s * (1.0 / l_s)

    for h in range(GQA):
        cols = slice(h * Q_BLK, (h + 1) * Q_BLK)
        o_t = (gt_ref[0, 0, 0, h:h + 1, :] * o_cmp[:, cols]
               + gt_ref[0, 0, 1, h:h + 1, :] * o_sel[:, cols]
               + gt_ref[0, 0, 2, h:h + 1, :] * o_win[:, cols])
        o_ref[0, :, h * HEAD_DIM:(h + 1) * HEAD_DIM] = o_t.T.astype(o_ref.dtype)


def _window_mask(S):
    wk = WINDOW + Q_BLK
    i = np.arange(wk)[:, None]
    qi = np.arange(Q_BLK)[None, :]
    out = []
    for v in range(WINDOW // Q_BLK + 1):
        t = Q_BLK * v + qi if v < WINDOW // Q_BLK else WINDOW + qi
        out.append(np.where((i <= t) & (i > t - WINDOW), 0.0, NEG))
    return jnp.asarray(np.stack(out), dtype=F32)


def nsa_attention_core(q, kc, vct, ks_aug, kw, vswt, gates_t, ov_t):
    B, _, S, _ = q.shape
    n_c = S // CMP_STRIDE
    n_sel = S // SEL_BLK
    nq = GQA * Q_BLK
    hk = N_KV_HEADS
    wk = WINDOW + Q_BLK
    nv = WINDOW // Q_BLK
    once = pl.Buffered(1)
    return pl.pallas_call(
        _nsa_kernel,
        grid=(B, hk, S // Q_BLK),
        in_specs=[
            pl.BlockSpec((1, GQA, Q_BLK, HEAD_DIM), lambda b, h, i: (b, h, i, 0)),
            pl.BlockSpec((1, 1, n_c, HEAD_DIM), lambda b, h, i: (b, h, 0, 0)),
            pl.BlockSpec((1, 1, HEAD_DIM, n_c), lambda b, h, i: (b, hk + h, 0, 0)),
            pl.BlockSpec((1, 1, S, 2 * HEAD_DIM), lambda b, h, i: (b, h, 0, 0), pipeline_mode=once),
            pl.BlockSpec((1, 1, S // LANE, HEAD_DIM, LANE), lambda b, h, i: (b, h, 0, 0, 0),
                         pipeline_mode=once),
            pl.BlockSpec((1, 1, S, HEAD_DIM), lambda b, h, i: (b, hk + h, 0, 0), pipeline_mode=once),
            pl.BlockSpec((1, 1, S // LANE, HEAD_DIM, LANE), lambda b, h, i: (b, hk + h, 0, 0, 0),
                         pipeline_mode=once),
            pl.BlockSpec((1, 1, 3, GQA, Q_BLK), lambda b, h, i: (b, h, 0, 0, i)),
            pl.BlockSpec((n_sel, n_c), lambda b, h, i: (0, 0)),
            pl.BlockSpec((1, wk, Q_BLK), lambda b, h, i: (jnp.minimum(i, nv), 0, 0)),
        ],
        out_specs=pl.BlockSpec((1, Q_BLK, GQA * HEAD_DIM), lambda b, h, i: (b, i, h)),
        out_shape=jax.ShapeDtypeStruct((B, S, N_HEADS * HEAD_DIM), BF16),
        scratch_shapes=[pltpu.VMEM((max(1, n_sel // LANE), 2 * HEAD_DIM, nq), BF16),
                        pltpu.VMEM((n_sel, Q_BLK), F32),
                        pltpu.VMEM((SEL_TILE, nq), F32), pltpu.VMEM((SEL_TILE, nq), F32)],
        compiler_params=_params(("parallel", "parallel", "arbitrary")),
    )(q, kc, vct, ks_aug, vswt, kw, vswt, gates_t, ov_t, _window_mask(S))


def _moe_kernel(nused_ref, bexp_ref, rtok_ref, h_hbm, wg_ref, wu_ref, wd_ref, y_ref,
                x_even, x_odd, wg_b, wu_b, wd_b, sem):
    i = pl.program_id(0)
    n_used = nused_ref[0]
    blk = x_even.shape[0]
    bufs = (x_even, x_odd)

    def row_copy(b, r, slot):
        tok = rtok_ref[b * blk + r]
        return pltpu.make_async_copy(h_hbm.at[pl.ds(tok, 1)], bufs[slot].at[pl.ds(r, 1)],
                                     sem.at[slot])

    def wait_block(slot):
        pltpu.make_async_copy(h_hbm.at[pl.ds(0, blk)], bufs[slot], sem.at[slot]).wait()

    @pl.when(i == 0)
    def _():
        def body(r, c):
            row_copy(0, r, 0).start()
            return c
        lax.fori_loop(0, blk, body, 0, unroll=8)

    @pl.when((i < n_used) & ((i == 0) | (bexp_ref[i] != bexp_ref[jnp.maximum(i - 1, 0)])))
    def _():
        wg_b[...] = wg_ref[0, 0].astype(BF16)
        wu_b[...] = wu_ref[0, 0].astype(BF16)
        wd_b[...] = wd_ref[0, 0].astype(BF16)

    for slot in range(2):
        @pl.when((i < n_used) & (i % 2 == slot))
        def _(slot=slot):
            wait_block(slot)
            for r in range(blk):
                row_copy(i + 1, r, 1 - slot).start(priority=r % 2)
            x = jnp.concatenate(_unpack_rows(bufs[slot][...]), axis=1).astype(BF16)
            gate = jnp.dot(x, wg_b[...], preferred_element_type=F32)
            up = jnp.dot(x, wu_b[...], preferred_element_type=F32)
            hid = (gate * jax.nn.sigmoid(gate) * up).astype(BF16)
            y_ref[...] = _pack_rows(jnp.dot(hid, wd_b[...], preferred_element_type=F32))

        @pl.when((i == n_used) & (i % 2 == slot))
        def _(slot=slot):
            wait_block(slot)

    @pl.when(i >= n_used)
    def _():
        y_ref[...] = jnp.zeros_like(y_ref)


def moe_experts(xn, n_used, block_exp, row_tok, wg, wu, wd, layer, blk=MOE_BLK):
    n_rows = row_tok.shape[0]
    n_steps = n_rows // blk
    grid_spec = pltpu.PrefetchScalarGridSpec(
        num_scalar_prefetch=3,
        grid=(n_steps,),
        in_specs=[
            pl.BlockSpec(memory_space=pl.ANY),
            pl.BlockSpec((1, 1, D_MODEL, D_EXPERT), lambda i, nu, be, rt: (layer, be[i], 0, 0)),
            pl.BlockSpec((1, 1, D_MODEL, D_EXPERT), lambda i, nu, be, rt: (layer, be[i], 0, 0)),
            pl.BlockSpec((1, 1, D_EXPERT, D_MODEL), lambda i, nu, be, rt: (layer, be[i], 0, 0)),
        ],
        out_specs=pl.BlockSpec((blk, D_MODEL // 2), lambda i, nu, be, rt: (i, 0)),
        scratch_shapes=[pltpu.VMEM((blk, D_MODEL // 2), jnp.uint32),
                        pltpu.VMEM((blk, D_MODEL // 2), jnp.uint32),
                        pltpu.VMEM((D_MODEL, D_EXPERT), BF16), pltpu.VMEM((D_MODEL, D_EXPERT), BF16),
                        pltpu.VMEM((D_EXPERT, D_MODEL), BF16), pltpu.SemaphoreType.DMA((2,))],
    )
    return pl.pallas_call(
        _moe_kernel, grid_spec=grid_spec,
        out_shape=jax.ShapeDtypeStruct((n_rows, D_MODEL // 2), jnp.uint32),
        compiler_params=_params(("arbitrary",)),
    )(n_used, block_exp, row_tok, xn, wg, wu, wd)


def _combine_kernel(pos_ref, h_ref, w_ref, g_ref, y_hbm, *rest):
    *o_refs, n_ref, y_even, y_odd, sem = rest
    i = pl.program_id(0)
    n = pl.num_programs(0)
    tm = h_ref.shape[0]
    bufs = (y_even, y_odd)

    def row_copy(b, j, k, slot):
        row = pos_ref[(b * tm + j) * TOP_K + k]
        return pltpu.make_async_copy(y_hbm.at[pl.ds(row, 1)], bufs[slot].at[pl.ds(k * tm + j, 1)],
                                     sem.at[slot])

    @pl.when(i == 0)
    def _():
        def body(j, c):
            for k in range(TOP_K):
                row_copy(0, j, k, 0).start()
            return c
        lax.fori_loop(0, tm, body, 0, unroll=4)

    def step(slot, prefetch):
        pltpu.make_async_copy(y_hbm.at[pl.ds(0, TOP_K * tm)], bufs[slot], sem.at[slot]).wait()
        if prefetch:
            for j in range(tm):
                for k in range(TOP_K):
                    row_copy(i + 1, j, k, 1 - slot).start(priority=k)
        half = D_MODEL // 2
        lo, hi = h_ref[:, :half], h_ref[:, half:]
        for k in range(TOP_K):
            y_lo, y_hi = _unpack_rows(bufs[slot][k * tm:(k + 1) * tm, :])
            lo = lo + y_lo * w_ref[:, k:k + 1]
            hi = hi + y_hi * w_ref[:, k:k + 1]
        out = jnp.concatenate([lo, hi], axis=1)
        for o_ref in o_refs:
            o_ref[...] = out
        n_ref[...] = _rms(out, g_ref[...]).astype(n_ref.dtype)

    for slot in range(2):
        for prefetch in (True, False):
            cond = (i % 2 == slot) & ((i + 1 < n) if prefetch else (i + 1 == n))
            pl.when(cond)(functools.partial(step, slot, prefetch))


def moe_combine(h, y, pos, weight, g_next, norm_dtype, keep_sum, tm=256):
    T = h.shape[0]
    n_out = 2 if keep_sum else 1
    grid_spec = pltpu.PrefetchScalarGridSpec(
        num_scalar_prefetch=1,
        grid=(T // tm,),
        in_specs=[
            pl.BlockSpec((tm, D_MODEL), lambda i, p: (i, 0)),
            pl.BlockSpec((tm, TOP_K), lambda i, p: (i, 0)),
            pl.BlockSpec((1, D_MODEL), lambda i, p: (0, 0)),
            pl.BlockSpec(memory_space=pl.ANY),
        ],
        out_specs=[pl.BlockSpec((tm, D_MODEL), lambda i, p: (i, 0))] * n_out,
        scratch_shapes=[pltpu.VMEM((TOP_K * tm, D_MODEL // 2), jnp.uint32),
                        pltpu.VMEM((TOP_K * tm, D_MODEL // 2), jnp.uint32),
                        pltpu.SemaphoreType.DMA((2,))],
    )
    return pl.pallas_call(
        _combine_kernel, grid_spec=grid_spec,
        out_shape=[jax.ShapeDtypeStruct((T, D_MODEL), F32)] * (n_out - 1)
        + [jax.ShapeDtypeStruct((T, D_MODEL), norm_dtype)],
        compiler_params=_params(("arbitrary",)),
    )(pos.reshape(-1), h, weight, g_next.reshape(1, D_MODEL), y)


def _dispatch(expert, T, blk):
    onehot = (expert[:, None] == jnp.arange(N_EXPERTS, dtype=jnp.int32)[None, :]).astype(jnp.int32)
    csum = jnp.cumsum(onehot, axis=0)
    rank = jnp.sum(csum * onehot, axis=1) - 1
    counts = csum[-1]
    padded = (counts + blk - 1) // blk * blk
    pad_end = jnp.cumsum(padded)
    pad_start = pad_end - padded
    pos = (jnp.sum(pad_start[None, :] * onehot, axis=1) + rank).astype(jnp.int32)
    n_rows = T * TOP_K + (N_EXPERTS + 1) * blk
    token = jnp.repeat(jnp.arange(T, dtype=jnp.int32), TOP_K)
    row_tok = jnp.zeros((n_rows,), jnp.int32).at[pos].set(token, unique_indices=True)
    n_blocks = n_rows // blk
    starts = jnp.arange(n_blocks, dtype=jnp.int32) * blk
    block_exp = jnp.minimum(jnp.sum((pad_end[None, :] <= starts[:, None]).astype(jnp.int32), axis=1),
                            N_EXPERTS - 1).astype(jnp.int32)
    n_used = (pad_end[-1] // blk).astype(jnp.int32).reshape(1)
    return pos.reshape(T, TOP_K), row_tok, block_exp, n_used


def _moe_layer(h, norm_g, w_group, b_group, w_exp, b_exp, wg, wu, wd, layer, g_next, norm_dtype,
               keep_sum=True):
    T = h.shape[0]
    wr = jnp.zeros((D_MODEL, LANE), F32)
    wr = wr.at[:, :N_GROUPS].set(w_group).at[:, N_GROUPS:N_GROUPS + N_EXPERTS].set(w_exp)
    br = jnp.zeros((LANE,), F32).at[:N_GROUPS].set(b_group).at[N_GROUPS:N_GROUPS + N_EXPERTS].set(b_exp)
    e_t, w_t, xn = router(h, norm_g, wr, br.reshape(LANE, 1))
    expert = e_t[:TOP_K].T.reshape(-1)
    weight = w_t[:TOP_K].T
    pos, row_tok, block_exp, n_used = _dispatch(expert, T, MOE_BLK)
    y = moe_experts(xn, n_used, block_exp, row_tok, wg, wu, wd, layer)
    return moe_combine(h, y, pos, weight, g_next, norm_dtype, keep_sum)


def _gmlp_layer(h, norm_g, w_in, ln_g, ln_b, w_s, b_s, w_out):
    causal = jnp.tril(jnp.ones((GM_CHUNK, GM_CHUNK), w_s.dtype))
    bs_full = jnp.repeat(b_s.T, D_MODEL // GM_GROUPS, axis=1)
    y = gmlp_gate(h, norm_g, w_in.astype(BF16), ln_g, ln_b, (w_s * causal).astype(BF16), bs_full)
    return matmul(y, w_out.astype(BF16), out_dtype=F32, resid=h)


def _overlap_t(S):
    n_c = S // CMP_STRIDE
    n_sel = S // SEL_BLK
    ci = np.arange(n_c)[None, :] * CMP_STRIDE
    sj = np.arange(n_sel)[:, None] * SEL_BLK
    ov = (ci < sj + SEL_BLK) & (ci + CMP_LEN > sj) & (np.arange(n_c)[None, :] < n_c - 1)
    return jnp.asarray(ov.astype(np.float32), dtype=BF16)


def _nsa_layer(h, hn, B, S, w_in, ck_pe, ck_w1, ck_w2, cv_pe, cv_w1, cv_w2, w_out):
    qd = N_HEADS * HEAD_DIM
    kvd = N_KV_HEADS * HEAD_DIM
    w = w_in.astype(BF16)
    o = [qd + i * kvd for i in range(7)]
    w_q, w_kc, w_vc, w_ks, w_vs, w_kw, w_vw = (
        w[:, :o[0]], w[:, o[0]:o[1]], w[:, o[1]:o[2]], w[:, o[2]:o[3]], w[:, o[3]:o[4]],
        w[:, o[4]:o[5]], w[:, o[5]:o[6]])
    w_g = jnp.zeros((D_MODEL, LANE), BF16).at[:, :3 * N_HEADS].set(w[:, o[6]:])
    q = matmul(hn, w_q, out_dtype=BF16, scale=HEAD_DIM ** -0.5 * LOG2E, layout="heads", seq=S)
    kvc = matmul(hn, jnp.concatenate([w_kc, w_vc], 1), out_dtype=F32, layout="groups", seq=S)
    ksw = matmul(hn, jnp.concatenate([w_ks, w_kw], 1), out_dtype=BF16, layout="heads", seq=S)
    vswt = matmul(hn, jnp.concatenate([w_vs, w_vw], 1), out_dtype=BF16, layout="heads_t", seq=S)
    gates = matmul(hn, w_g, out_dtype=F32, act="sigmoid")[:, :3 * N_HEADS]
    gates_t = gates.reshape(B, S, N_KV_HEADS, GQA, 3).transpose(0, 2, 4, 3, 1)

    half = CMP_STRIDE
    pe2 = jnp.stack([jnp.stack([pe[:half].reshape(-1), pe[half:].reshape(-1)])
                     for pe in (ck_pe, cv_pe)])
    w1 = jnp.stack([ck_w1, cv_w1]).astype(BF16)
    w2 = jnp.stack([ck_w2, cv_w2]).astype(BF16)
    cmp, cmp_t = compress_kv(kvc, pe2, w1, w2)
    n_c = S // CMP_STRIDE
    cmp = cmp.reshape(B, 2 * N_KV_HEADS, n_c, HEAD_DIM)
    cmp_t = cmp_t.reshape(B, 2 * N_KV_HEADS, HEAD_DIM, n_c)
    blk_id = (np.arange(S) // SEL_BLK) % LANE
    onehot = jnp.asarray(blk_id[:, None] == np.arange(LANE)[None, :], dtype=BF16)
    ks_aug = jnp.concatenate(
        [ksw[:, :N_KV_HEADS], jnp.broadcast_to(onehot, (B, N_KV_HEADS, S, LANE))], axis=-1)
    o_att = nsa_attention_core(q, cmp, cmp_t, ks_aug, ksw, vswt, gates_t, _overlap_t(S))
    return matmul(o_att.reshape(B * S, qd), w_out.astype(BF16), out_dtype=F32, resid=h)


def kernel(x, norm_mix, norm_ffn, norm_final, a_w_in, a_ln_g, a_ln_b, a_w_s, a_b_s, a_w_out,
           b_w_in, b_ck_pe, b_ck_w1, b_ck_w2, b_cv_pe, b_cv_w1, b_cv_w2, b_w_out,
           r_w_group, r_b_group, r_w_exp, r_b_exp, e_w_gate, e_w_up, e_w_down):
    B, S, _ = x.shape
    h = x.reshape(B * S, D_MODEL)
    h = _gmlp_layer(h, norm_mix[0], a_w_in[0], a_ln_g[0], a_ln_b[0], a_w_s[0], a_b_s[0], a_w_out[0])
    h, hn = _moe_layer(h, norm_ffn[0], r_w_group[0], r_b_group[0], r_w_exp[0], r_b_exp[0],
                       e_w_gate, e_w_up, e_w_down, 0, norm_mix[1], BF16)
    h = _nsa_layer(h, hn, B, S, b_w_in[0], b_ck_pe[0], b_ck_w1[0], b_ck_w2[0],
                   b_cv_pe[0], b_cv_w1[0], b_cv_w2[0], b_w_out[0])
    (out,) = _moe_layer(h, norm_ffn[1], r_w_group[1], r_b_group[1], r_w_exp[1], r_b_exp[1],
                        e_w_gate, e_w_up, e_w_down, 1, norm_final, F32, keep_sum=False)
    return out.reshape(B, S, D_MODEL)
```

```python
import functools

import numpy as np
import jax
import jax.numpy as jnp
from jax import lax
from jax.experimental import pallas as pl
from jax.experimental.pallas import tpu as pltpu

F32 = jnp.float32
BF16 = jnp.bfloat16

D_MODEL = 2048
LANE = 128
GM_GROUPS = 8
GM_CHUNK = 128
HEAD_DIM = 128
N_HEADS = 16
N_KV_HEADS = 4
GQA = N_HEADS // N_KV_HEADS
CMP_LEN = 32
CMP_STRIDE = 16
SEL_BLK = 64
SEL_TOPN = 16
SEL_LOCAL = 2
WINDOW = 512
Q_BLK = 256
FORCE = 1.0e4
N_GROUPS = 4
EXP_PER_GROUP = 8
N_EXPERTS = N_GROUPS * EXP_PER_GROUP
TOP_K = 2
D_EXPERT = 512
EPS = 1e-6
NEG = -1e30
LOG2E = 1.4426950408889634
ONES_ROWS = 16
SEL_TILE = 512
MOE_BLK = 512
VMEM_LIMIT = 56 * 1024 * 1024

_NT = (((1,), (1,)), ((), ()))


def _params(sem):
    return pltpu.CompilerParams(dimension_semantics=sem, vmem_limit_bytes=VMEM_LIMIT)


def _gelu(x):
    return 0.5 * x * (1.0 + jnp.tanh(0.7978845608028654 * (x + 0.044715 * (x * x * x))))


def _rms(x, g):
    y = x * lax.rsqrt(jnp.mean(x * x, axis=-1, keepdims=True) + EPS)
    return y * g


def _pack_rows(x):
    half = x.shape[1] // 2
    bits = lambda v: lax.bitcast_convert_type(v.astype(BF16).astype(F32), jnp.uint32)
    return (bits(x[:, :half]) >> 16) | (bits(x[:, half:]) & jnp.uint32(0xFFFF0000))


def _unpack_rows(w):
    lo = lax.bitcast_convert_type(w << 16, F32)
    hi = lax.bitcast_convert_type(w & jnp.uint32(0xFFFF0000), F32)
    return lo, hi


def _rmsnorm_kernel(x_ref, g_ref, o_ref):
    o_ref[...] = _rms(x_ref[...], g_ref[...]).astype(o_ref.dtype)


def rmsnorm(x, g, out_dtype=BF16, tm=512):
    T = x.shape[0]
    return pl.pallas_call(
        _rmsnorm_kernel,
        grid=(T // tm,),
        in_specs=[pl.BlockSpec((tm, D_MODEL), lambda i: (i, 0)),
                  pl.BlockSpec((1, D_MODEL), lambda i: (0, 0))],
        out_specs=pl.BlockSpec((tm, D_MODEL), lambda i: (i, 0)),
        out_shape=jax.ShapeDtypeStruct((T, D_MODEL), out_dtype),
        compiler_params=_params(("parallel",)),
    )(x, g.reshape(1, D_MODEL))


def _router_kernel(x_ref, g_ref, wh_ref, wl_ref, br_ref, e_ref, w_ref, xn_ref):
    y = _rms(x_ref[...], g_ref[...])
    xn_ref[...] = _pack_rows(y)
    y_hi = y.astype(BF16)
    y_lo = (y - y_hi.astype(F32)).astype(BF16)
    lg = (jnp.dot(y_hi, wh_ref[...], preferred_element_type=F32)
          + jnp.dot(y_hi, wl_ref[...], preferred_element_type=F32)
          + jnp.dot(y_lo, wh_ref[...], preferred_element_type=F32))
    lt = lg.T + br_ref[...]
    row = lambda i: lt[i:i + 1, :]

    def softmax(rows):
        m = functools.reduce(jnp.maximum, rows)
        e = [jnp.exp(r - m) for r in rows]
        s = functools.reduce(lambda a, b: a + b, e)
        return [v / s for v in e]

    def top1(vals, skip=None):
        best = jnp.full_like(vals[0], -1.0)
        idx = jnp.zeros(vals[0].shape, jnp.int32)
        for k, v in enumerate(vals):
            if skip is not None:
                v = jnp.where(skip == k, -1.0, v)
            upd = v > best
            best = jnp.where(upd, v, best)
            idx = jnp.where(upd, k, idx)
        return best, idx

    g_w, g_idx = top1(softmax([row(j) for j in range(N_GROUPS)]))
    e_logits = []
    for k in range(EXP_PER_GROUP):
        v = row(N_GROUPS + k)
        for gg in range(1, N_GROUPS):
            v = jnp.where(g_idx == gg, row(N_GROUPS + gg * EXP_PER_GROUP + k), v)
        e_logits.append(v)
    e_prob = softmax(e_logits)
    w1, i1 = top1(e_prob)
    w2, i2 = top1(e_prob, skip=i1)
    tot = w1 + w2
    zi = jnp.zeros((6, lt.shape[1]), jnp.int32)
    e_ref[...] = jnp.concatenate([g_idx * EXP_PER_GROUP + i1, g_idx * EXP_PER_GROUP + i2, zi], axis=0)
    w_ref[...] = jnp.concatenate([g_w * (w1 / tot), g_w * (w2 / tot), zi.astype(F32)], axis=0)


def router(x, g, wr, br, tm=512):
    T = x.shape[0]
    wr_hi = wr.astype(BF16)
    wr_lo = (wr - wr_hi.astype(F32)).astype(BF16)
    return pl.pallas_call(
        _router_kernel,
        grid=(T // tm,),
        in_specs=[pl.BlockSpec((tm, D_MODEL), lambda i: (i, 0)),
                  pl.BlockSpec((1, D_MODEL), lambda i: (0, 0)),
                  pl.BlockSpec((D_MODEL, LANE), lambda i: (0, 0)),
                  pl.BlockSpec((D_MODEL, LANE), lambda i: (0, 0)),
                  pl.BlockSpec((LANE, 1), lambda i: (0, 0))],
        out_specs=[pl.BlockSpec((8, tm), lambda i: (0, i)),
                   pl.BlockSpec((8, tm), lambda i: (0, i)),
                   pl.BlockSpec((tm, D_MODEL // 2), lambda i: (i, 0))],
        out_shape=[jax.ShapeDtypeStruct((8, T), jnp.int32),
                   jax.ShapeDtypeStruct((8, T), F32),
                   jax.ShapeDtypeStruct((T, D_MODEL // 2), jnp.uint32)],
        compiler_params=_params(("parallel",)),
    )(x, g.reshape(1, D_MODEL), wr_hi, wr_lo, br)


def _mm_kernel(*refs, act, scale, has_resid, layout):
    a_ref, w_ref = refs[0], refs[1]
    o_ref = refs[-2] if layout == "groups" else refs[-1]
    acc = jnp.dot(a_ref[...], w_ref[...], preferred_element_type=F32)
    if scale is not None:
        acc = acc * scale
    if act == "gelu":
        acc = _gelu(acc)
    elif act == "sigmoid":
        acc = jax.nn.sigmoid(acc)
    if has_resid:
        acc = acc + refs[2][...]
    tm, tn = acc.shape
    if layout == "plain":
        o_ref[...] = acc.astype(o_ref.dtype)
    elif layout == "heads":
        for h in range(tn // HEAD_DIM):
            o_ref[0, h] = acc[:, h * HEAD_DIM:(h + 1) * HEAD_DIM].astype(o_ref.dtype)
    elif layout == "groups":
        tmp = refs[-1]
        for h in range(tn // HEAD_DIM):
            tmp[...] = acc[:, h * HEAD_DIM:(h + 1) * HEAD_DIM]
            for t in range(CMP_STRIDE):
                o_ref[0, h, :, t * HEAD_DIM:(t + 1) * HEAD_DIM] = (
                    tmp[pl.ds(t, tm // CMP_STRIDE, stride=CMP_STRIDE), :].astype(o_ref.dtype))
    else:
        for h in range(tn // HEAD_DIM):
            for c in range(tm // LANE):
                blk = acc[c * LANE:(c + 1) * LANE, h * HEAD_DIM:(h + 1) * HEAD_DIM]
                o_ref[0, h, c] = blk.T.astype(o_ref.dtype)


def matmul(a, w, *, out_dtype, act=None, scale=None, resid=None, layout="plain",
           seq=None, tm=1024, tn=1024):
    M, K = a.shape
    N = w.shape[1]
    tn = min(tn, N)
    grid = (N // tn, M // tm)
    in_specs = [pl.BlockSpec((tm, K), lambda j, i: (i, 0)),
                pl.BlockSpec((K, tn), lambda j, i: (0, j))]
    args = [a, w]
    if resid is not None:
        in_specs.append(pl.BlockSpec((tm, tn), lambda j, i: (i, j)))
        args.append(resid)
    scratch = []
    if layout == "plain":
        out_shape = jax.ShapeDtypeStruct((M, N), out_dtype)
        out_spec = pl.BlockSpec((tm, tn), lambda j, i: (i, j))
    else:
        nb = seq // tm
        nh = tn // HEAD_DIM
        if layout == "heads":
            out_shape = jax.ShapeDtypeStruct((M // seq, N // HEAD_DIM, seq, HEAD_DIM), out_dtype)
            out_spec = pl.BlockSpec((1, nh, tm, HEAD_DIM), lambda j, i: (i // nb, j, i % nb, 0))
        elif layout == "groups":
            grp = CMP_STRIDE * HEAD_DIM
            out_shape = jax.ShapeDtypeStruct((M // seq, N // HEAD_DIM, seq // CMP_STRIDE, grp), out_dtype)
            out_spec = pl.BlockSpec((1, nh, tm // CMP_STRIDE, grp), lambda j, i: (i // nb, j, i % nb, 0))
            scratch = [pltpu.VMEM((tm, HEAD_DIM), F32)]
        else:
            out_shape = jax.ShapeDtypeStruct(
                (M // seq, N // HEAD_DIM, seq // LANE, HEAD_DIM, LANE), out_dtype)
            out_spec = pl.BlockSpec((1, nh, tm // LANE, HEAD_DIM, LANE),
                                    lambda j, i: (i // nb, j, i % nb, 0, 0))
    kern = functools.partial(_mm_kernel, act=act, scale=scale, has_resid=resid is not None,
                             layout=layout)
    return pl.pallas_call(
        kern, grid=grid, in_specs=in_specs, out_specs=out_spec, out_shape=out_shape,
        scratch_shapes=scratch, compiler_params=_params(("parallel", "parallel")),
    )(*args)


def _gate_kernel(x_ref, g_ref, w_ref, lng_ref, lnb_ref, ws_ref, bs_ref, y_ref):
    tm = x_ref.shape[0]
    a = _rms(x_ref[...], g_ref[...]).astype(BF16)
    z = _gelu(jnp.dot(a, w_ref[...], preferred_element_type=F32))
    u = z[:, :D_MODEL]
    v = z[:, D_MODEL:]
    mu = jnp.mean(v, axis=-1, keepdims=True)
    vc = v - mu
    vn = vc * lax.rsqrt(jnp.mean(vc * vc, axis=-1, keepdims=True) + EPS)
    vn = (vn * lng_ref[...] + lnb_ref[...]).astype(BF16)
    gd = D_MODEL // GM_GROUPS
    for c in range(tm // GM_CHUNK):
        rows = slice(c * GM_CHUNK, (c + 1) * GM_CHUNK)
        for g in range(GM_GROUPS):
            cols = slice(g * gd, (g + 1) * gd)
            sv = jnp.dot(ws_ref[g], vn[rows, cols], preferred_element_type=F32)
            y_ref[rows, cols] = (u[rows, cols] * (sv + bs_ref[:, cols])).astype(y_ref.dtype)


def gmlp_gate(x, g, w_in, ln_g, ln_b, ws_masked, bs_full, tm=512):
    T = x.shape[0]
    return pl.pallas_call(
        _gate_kernel,
        grid=(T // tm,),
        in_specs=[pl.BlockSpec((tm, D_MODEL), lambda i: (i, 0)),
                  pl.BlockSpec((1, D_MODEL), lambda i: (0, 0)),
                  pl.BlockSpec((D_MODEL, 2 * D_MODEL), lambda i: (0, 0), pipeline_mode=pl.Buffered(1)),
                  pl.BlockSpec((1, D_MODEL), lambda i: (0, 0)),
                  pl.BlockSpec((1, D_MODEL), lambda i: (0, 0)),
                  pl.BlockSpec((GM_GROUPS, GM_CHUNK, GM_CHUNK), lambda i: (0, 0, 0)),
                  pl.BlockSpec((GM_CHUNK, D_MODEL), lambda i: (0, 0))],
        out_specs=pl.BlockSpec((tm, D_MODEL), lambda i: (i, 0)),
        out_shape=jax.ShapeDtypeStruct((T, D_MODEL), BF16),
        compiler_params=_params(("parallel",)),
    )(x, g.reshape(1, D_MODEL), w_in, ln_g.reshape(1, D_MODEL), ln_b.reshape(1, D_MODEL), ws_masked,
      bs_full)


def _compress_kernel(x_ref, pe_ref, w1_ref, w2_ref, o_ref, ot_ref):
    x = x_ref[0]
    half = CMP_STRIDE * HEAD_DIM
    xa = (x + pe_ref[0, 0:1, :]).astype(BF16)
    xb = (x + pe_ref[0, 1:2, :]).astype(BF16)
    a = jnp.dot(xa, w1_ref[0, :half, :], preferred_element_type=F32)
    b = jnp.dot(xb, w1_ref[0, half:, :], preferred_element_type=F32)
    n = a.shape[0]
    pre = a + pltpu.roll(b, n - 1, axis=0)
    out = jnp.dot(_gelu(pre).astype(BF16), w2_ref[0], preferred_element_type=F32)
    o_ref[0] = out.astype(o_ref.dtype)
    ot_ref[0] = out.T.astype(ot_ref.dtype)


def compress_kv(kvc, pe2, w1, w2):
    B, H2, n_grp, _ = kvc.shape
    x = kvc.reshape(B * H2, n_grp, CMP_STRIDE * HEAD_DIM)
    sel = lambda i: ((i % H2) // N_KV_HEADS, 0, 0)
    return pl.pallas_call(
        _compress_kernel,
        grid=(B * H2,),
        in_specs=[pl.BlockSpec((1, n_grp, CMP_STRIDE * HEAD_DIM), lambda i: (i, 0, 0)),
                  pl.BlockSpec((1, 2, CMP_STRIDE * HEAD_DIM), sel),
                  pl.BlockSpec((1, CMP_LEN * HEAD_DIM, HEAD_DIM), sel),
                  pl.BlockSpec((1, HEAD_DIM, HEAD_DIM), sel)],
        out_specs=[pl.BlockSpec((1, n_grp, HEAD_DIM), lambda i: (i, 0, 0)),
                   pl.BlockSpec((1, HEAD_DIM, n_grp), lambda i: (i, 0, 0))],
        out_shape=[jax.ShapeDtypeStruct((B * H2, n_grp, HEAD_DIM), BF16),
                   jax.ShapeDtypeStruct((B * H2, HEAD_DIM, n_grp), BF16)],
        compiler_params=_params(("parallel",)),
    )(x, pe2, w1, w2)


def _nsa_kernel(q_ref, kc_ref, vct_ref, ksa_ref, vst_ref, kw_ref, vwt_ref, gt_ref, ovt_ref, wm_ref,
                o_ref, qaug_ref, bsel_ref, s_a, s_b):
    nq = GQA * Q_BLK
    qb = pl.program_id(2)
    q0 = qb * Q_BLK
    q_t = q_ref[0].reshape(nq, HEAD_DIM).astype(F32).T.astype(BF16)
    t_lane = q0 + (lax.broadcasted_iota(jnp.int32, (1, nq), 1) & (Q_BLK - 1))

    n_c = kc_ref.shape[2]

    def compressed(n):
        s = jnp.dot(kc_ref[0, 0, :n, :], q_t, preferred_element_type=F32)
        c_end = lax.broadcasted_iota(jnp.int32, (n, 1), 0) * CMP_STRIDE + (CMP_LEN - 1)
        s = jnp.where(c_end <= t_lane, s, NEG)
        m = jnp.max(s, axis=0, keepdims=True)
        p = jnp.exp2(s - m)
        l = jnp.sum(p, axis=0, keepdims=True)
        p = p * jnp.where(m > 0.5 * NEG, 1.0 / l, 0.0)
        o = jnp.dot(vct_ref[0, 0, :, :n], p.astype(BF16), preferred_element_type=F32)
        psum = p[:, 0:Q_BLK]
        for h in range(1, GQA):
            psum = psum + p[:, h * Q_BLK:(h + 1) * Q_BLK]
        p_hi = psum.astype(BF16)
        p_lo = (psum - p_hi.astype(F32)).astype(BF16)
        ov = ovt_ref[:, :n]
        return o, (jnp.dot(ov, p_hi, preferred_element_type=F32)
                   + jnp.dot(ov, p_lo, preferred_element_type=F32))

    n_bkt = 4 if n_c % (4 * LANE) == 0 else 1
    bkt = n_c // n_bkt
    if n_bkt == 1:
        o_cmp, imp = compressed(n_c)
    else:
        o_cmp, imp = lax.switch((q0 + Q_BLK - 1) // (bkt * CMP_STRIDE),
                                [functools.partial(compressed, bkt * (i + 1)) for i in range(n_bkt)])

    n_sel = imp.shape[0]
    blk = lax.broadcasted_iota(jnp.int32, (n_sel, Q_BLK), 0)
    t_q = q0 + lax.broadcasted_iota(jnp.int32, (n_sel, Q_BLK), 1)
    dist = (t_q // SEL_BLK) - blk
    forced = (blk == 0) | ((dist >= 0) & (dist < SEL_LOCAL))
    score = jnp.where(forced, -jnp.inf, jnp.where(blk * SEL_BLK <= t_q, imp, NEG))
    blk_f = blk.astype(F32)
    for _ in range(SEL_TOPN - 1 - SEL_LOCAL):
        top = jnp.max(score, axis=0, keepdims=True)
        first = jnp.min(jnp.where(score == top, blk_f, float(n_sel)), axis=0, keepdims=True)
        score = jnp.where(blk_f == first, -jnp.inf, score)
    picked = score == -jnp.inf
    bsel_ref[...] = jnp.where(picked, 0.0, NEG)
    bias = jnp.where(picked & (blk * SEL_BLK < q0), 0.0, NEG)

    n_half = qaug_ref.shape[0]
    if n_sel < LANE:
        bias = jnp.concatenate([bias, jnp.zeros((LANE - n_sel, Q_BLK), F32)], axis=0)
    for hf in range(n_half):
        bq = bias[hf * LANE:(hf + 1) * LANE, :].astype(BF16)
        qaug_ref[hf] = jnp.concatenate([q_t, jnp.concatenate([bq] * GQA, axis=1)], axis=0)

    wk = WINDOW + Q_BLK
    kstart = pl.multiple_of(jnp.maximum(q0 - WINDOW, 0), LANE)
    sw = jnp.dot(kw_ref[0, 0, pl.ds(kstart, wk), :], q_t, preferred_element_type=F32)
    sw = sw + jnp.concatenate([wm_ref[0]] * GQA, axis=1)
    m_w = jnp.max(sw, axis=0, keepdims=True)
    p_w = jnp.exp2((sw - m_w).astype(BF16))
    vw = jnp.concatenate([vwt_ref[0, 0, kstart // LANE + j] for j in range(wk // LANE)], axis=1)
    vw = jnp.concatenate([vw, jnp.ones((ONES_ROWS, wk), BF16)], axis=0)
    acc_w = jnp.dot(vw, p_w, preferred_element_type=F32)
    o_win = acc_w[:HEAD_DIM] * (1.0 / acc_w[HEAD_DIM:HEAD_DIM + 1])

    kpos = q0 + lax.broadcasted_iota(jnp.int32, (Q_BLK, 1), 0)
    s_d = jnp.dot(ksa_ref[0, 0, pl.ds(pl.multiple_of(q0, Q_BLK), Q_BLK), :HEAD_DIM], q_t,
                  preferred_element_type=F32)
    parts = []
    for j in range(Q_BLK // SEL_BLK):
        b_row = bsel_ref[pl.ds(q0 // SEL_BLK + j, 1), :]
        parts.append(s_d[j * SEL_BLK:(j + 1) * SEL_BLK, :] + jnp.concatenate([b_row] * GQA, axis=1))
    s_d = jnp.where(kpos <= t_lane, jnp.concatenate(parts, axis=0), NEG)
    m_d = jnp.max(s_d, axis=0, keepdims=True)
    p_d = jnp.exp2(s_d - m_d)
    v_d = jnp.concatenate([vst_ref[0, 0, q0 // LANE + j] for j in range(Q_BLK // LANE)], axis=1)
    init = (m_d, jnp.sum(p_d, axis=0, keepdims=True),
            jnp.dot(v_d, p_d.astype(BF16), preferred_element_type=F32))

    bpt = SEL_TILE // SEL_BLK
    vpt = SEL_TILE // LANE

    def scores(kt):
        k0 = pl.multiple_of(kt * SEL_TILE, SEL_TILE)
        return jnp.dot(ksa_ref[0, 0, pl.ds(k0, SEL_TILE), :], qaug_ref[(kt * bpt) // LANE],
                       preferred_element_type=F32)

    def softmax_pv(kt, st, carry):
        m_i, l_i, acc = carry
        m_new = jnp.maximum(m_i, jnp.max(st, axis=0, keepdims=True))
        alpha = jnp.exp2(m_i - m_new)
        pt = jnp.exp2(st - m_new)
        l_new = alpha * l_i + jnp.sum(pt, axis=0, keepdims=True)
        vt = jnp.concatenate([vst_ref[0, 0, kt * vpt + j] for j in range(vpt)], axis=1)
        acc = alpha * acc + jnp.dot(vt, pt.astype(BF16), preferred_element_type=F32)
        return m_new, l_new, acc

    def trip(j, carry):
        st = s_a[...]
        s_b[...] = scores(2 * j + 1)
        carry = softmax_pv(2 * j, st, carry)
        st = s_b[...]
        s_a[...] = scores(2 * j + 2)
        return softmax_pv(2 * j + 1, st, carry)

    n_tiles = (q0 + SEL_TILE - 1) // SEL_TILE
    n_trips = jnp.maximum((n_tiles + 1) // 2, 1)
    s_a[...] = scores(0)
    carry = lax.fori_loop(0, n_trips - 1, trip, init)
    last = 2 * n_trips - 1
    st = s_a[...]
    s_b[...] = scores(last)
    carry = softmax_pv(last - 1, st, carry)
    _, l_s, acc_s = softmax_pv(last, s_b[...], carry)
    o_sel = acc_s * (1.0 / l_s)

    for h in range(GQA):
        cols = slice(h * Q_BLK, (h + 1) * Q_BLK)
        o_t = (gt_ref[0, 0, 0, h:h + 1, :] * o_cmp[:, cols]
               + gt_ref[0, 0, 1, h:h + 1, :] * o_sel[:, cols]
               + gt_ref[0, 0, 2, h:h + 1, :] * o_win[:, cols])
        o_ref[0, :, h * HEAD_DIM:(h + 1) * HEAD_DIM] = o_t.T.astype(o_ref.dtype)


def _window_mask(S):
    wk = WINDOW + Q_BLK
    i = np.arange(wk)[:, None]
    qi = np.arange(Q_BLK)[None, :]
    out = []
    for v in range(WINDOW // Q_BLK + 1):
        t = Q_BLK * v + qi if v < WINDOW // Q_BLK else WINDOW + qi
        out.append(np.where((i <= t) & (i > t - WINDOW), 0.0, NEG))
    return jnp.asarray(np.stack(out), dtype=F32)


def nsa_attention_core(q, kc, vct, ks_aug, kw, vswt, gates_t, ov_t):
    B, _, S, _ = q.shape
    n_c = S // CMP_STRIDE
    n_sel = S // SEL_BLK
    nq = GQA * Q_BLK
    hk = N_KV_HEADS
    wk = WINDOW + Q_BLK
    nv = WINDOW // Q_BLK
    once = pl.Buffered(1)
    return pl.pallas_call(
        _nsa_kernel,
        grid=(B, hk, S // Q_BLK),
        in_specs=[
            pl.BlockSpec((1, GQA, Q_BLK, HEAD_DIM), lambda b, h, i: (b, h, i, 0)),
            pl.BlockSpec((1, 1, n_c, HEAD_DIM), lambda b, h, i: (b, h, 0, 0)),
            pl.BlockSpec((1, 1, HEAD_DIM, n_c), lambda b, h, i: (b, hk + h, 0, 0)),
            pl.BlockSpec((1, 1, S, 2 * HEAD_DIM), lambda b, h, i: (b, h, 0, 0), pipeline_mode=once),
            pl.BlockSpec((1, 1, S // LANE, HEAD_DIM, LANE), lambda b, h, i: (b, h, 0, 0, 0),
                         pipeline_mode=once),
            pl.BlockSpec((1, 1, S, HEAD_DIM), lambda b, h, i: (b, hk + h, 0, 0), pipeline_mode=once),
            pl.BlockSpec((1, 1, S // LANE, HEAD_DIM, LANE), lambda b, h, i: (b, hk + h, 0, 0, 0),
                         pipeline_mode=once),
            pl.BlockSpec((1, 1, 3, GQA, Q_BLK), lambda b, h, i: (b, h, 0, 0, i)),
            pl.BlockSpec((n_sel, n_c), lambda b, h, i: (0, 0)),
            pl.BlockSpec((1, wk, Q_BLK), lambda b, h, i: (jnp.minimum(i, nv), 0, 0)),
        ],
        out_specs=pl.BlockSpec((1, Q_BLK, GQA * HEAD_DIM), lambda b, h, i: (b, i, h)),
        out_shape=jax.ShapeDtypeStruct((B, S, N_HEADS * HEAD_DIM), BF16),
        scratch_shapes=[pltpu.VMEM((max(1, n_sel // LANE), 2 * HEAD_DIM, nq), BF16),
                        pltpu.VMEM((n_sel, Q_BLK), F32),
                        pltpu.VMEM((SEL_TILE, nq), F32), pltpu.VMEM((SEL_TILE, nq), F32)],
        compiler_params=_params(("parallel", "parallel", "arbitrary")),
    )(q, kc, vct, ks_aug, vswt, kw, vswt, gates_t, ov_t, _window_mask(S))


def _moe_kernel(nused_ref, bexp_ref, rtok_ref, h_hbm, wg_ref, wu_ref, wd_ref, y_ref,
                x_even, x_odd, wg_b, wu_b, wd_b, sem):
    i = pl.program_id(0)
    n_used = nused_ref[0]
    blk = x_even.shape[0]
    bufs = (x_even, x_odd)

    def row_copy(b, r, slot):
        tok = rtok_ref[b * blk + r]
        return pltpu.make_async_copy(h_hbm.at[pl.ds(tok, 1)], bufs[slot].at[pl.ds(r, 1)],
                                     sem.at[slot])

    def wait_block(slot):
        pltpu.make_async_copy(h_hbm.at[pl.ds(0, blk)], bufs[slot], sem.at[slot]).wait()

    @pl.when(i == 0)
    def _():
        def body(r, c):
            row_copy(0, r, 0).start()
            return c
        lax.fori_loop(0, blk, body, 0, unroll=8)

    @pl.when((i < n_used) & ((i == 0) | (bexp_ref[i] != bexp_ref[jnp.maximum(i - 1, 0)])))
    def _():
        wg_b[...] = wg_ref[0, 0].astype(BF16)
        wu_b[...] = wu_ref[0, 0].astype(BF16)
        wd_b[...] = wd_ref[0, 0].astype(BF16)

    for slot in range(2):
        @pl.when((i < n_used) & (i % 2 == slot))
        def _(slot=slot):
            wait_block(slot)
            for r in range(blk):
                row_copy(i + 1, r, 1 - slot).start(priority=r % 2)
            x = jnp.concatenate(_unpack_rows(bufs[slot][...]), axis=1).astype(BF16)
            gate = jnp.dot(x, wg_b[...], preferred_element_type=F32)
            up = jnp.dot(x, wu_b[...], preferred_element_type=F32)
            hid = (gate * jax.nn.sigmoid(gate) * up).astype(BF16)
            y_ref[...] = _pack_rows(jnp.dot(hid, wd_b[...], preferred_element_type=F32))

        @pl.when((i == n_used) & (i % 2 == slot))
        def _(slot=slot):
            wait_block(slot)

    @pl.when(i >= n_used)
    def _():
        y_ref[...] = jnp.zeros_like(y_ref)


def moe_experts(xn, n_used, block_exp, row_tok, wg, wu, wd, layer, blk=MOE_BLK):
    n_rows = row_tok.shape[0]
    n_steps = n_rows // blk
    grid_spec = pltpu.PrefetchScalarGridSpec(
        num_scalar_prefetch=3,
        grid=(n_steps,),
        in_specs=[
            pl.BlockSpec(memory_space=pl.ANY),
            pl.BlockSpec((1, 1, D_MODEL, D_EXPERT), lambda i, nu, be, rt: (layer, be[i], 0, 0)),
            pl.BlockSpec((1, 1, D_MODEL, D_EXPERT), lambda i, nu, be, rt: (layer, be[i], 0, 0)),
            pl.BlockSpec((1, 1, D_EXPERT, D_MODEL), lambda i, nu, be, rt: (layer, be[i], 0, 0)),
        ],
        out_specs=pl.BlockSpec((blk, D_MODEL // 2), lambda i, nu, be, rt: (i, 0)),
        scratch_shapes=[pltpu.VMEM((blk, D_MODEL // 2), jnp.uint32),
                        pltpu.VMEM((blk, D_MODEL // 2), jnp.uint32),
                        pltpu.VMEM((D_MODEL, D_EXPERT), BF16), pltpu.VMEM((D_MODEL, D_EXPERT), BF16),
                        pltpu.VMEM((D_EXPERT, D_MODEL), BF16), pltpu.SemaphoreType.DMA((2,))],
    )
    return pl.pallas_call(
        _moe_kernel, grid_spec=grid_spec,
        out_shape=jax.ShapeDtypeStruct((n_rows, D_MODEL // 2), jnp.uint32),
        compiler_params=_params(("arbitrary",)),
    )(n_used, block_exp, row_tok, xn, wg, wu, wd)


def _combine_kernel(pos_ref, h_ref, w_ref, g_ref, y_hbm, *rest):
    *o_refs, n_ref, y_even, y_odd, sem = rest
    i = pl.program_id(0)
    n = pl.num_programs(0)
    tm = h_ref.shape[0]
    bufs = (y_even, y_odd)

    def row_copy(b, j, k, slot):
        row = pos_ref[(b * tm + j) * TOP_K + k]
        return pltpu.make_async_copy(y_hbm.at[pl.ds(row, 1)], bufs[slot].at[pl.ds(k * tm + j, 1)],
                                     sem.at[slot])

    @pl.when(i == 0)
    def _():
        def body(j, c):
            for k in range(TOP_K):
                row_copy(0, j, k, 0).start()
            return c
        lax.fori_loop(0, tm, body, 0, unroll=4)

    def step(slot, prefetch):
        pltpu.make_async_copy(y_hbm.at[pl.ds(0, TOP_K * tm)], bufs[slot], sem.at[slot]).wait()
        if prefetch:
            for j in range(tm):
                for k in range(TOP_K):
                    row_copy(i + 1, j, k, 1 - slot).start(priority=k)
        half = D_MODEL // 2
        lo, hi = h_ref[:, :half], h_ref[:, half:]
        for k in range(TOP_K):
            y_lo, y_hi = _unpack_rows(bufs[slot][k * tm:(k + 1) * tm, :])
            lo = lo + y_lo * w_ref[:, k:k + 1]
            hi = hi + y_hi * w_ref[:, k:k + 1]
        out = jnp.concatenate([lo, hi], axis=1)
        for o_ref in o_refs:
            o_ref[...] = out
        n_ref[...] = _rms(out, g_ref[...]).astype(n_ref.dtype)

    for slot in range(2):
        for prefetch in (True, False):
            cond = (i % 2 == slot) & ((i + 1 < n) if prefetch else (i + 1 == n))
            pl.when(cond)(functools.partial(step, slot, prefetch))


def moe_combine(h, y, pos, weight, g_next, norm_dtype, keep_sum, tm=256):
    T = h.shape[0]
    n_out = 2 if keep_sum else 1
    grid_spec = pltpu.PrefetchScalarGridSpec(
        num_scalar_prefetch=1,
        grid=(T // tm,),
        in_specs=[
            pl.BlockSpec((tm, D_MODEL), lambda i, p: (i, 0)),
            pl.BlockSpec((tm, TOP_K), lambda i, p: (i, 0)),
            pl.BlockSpec((1, D_MODEL), lambda i, p: (0, 0)),
            pl.BlockSpec(memory_space=pl.ANY),
        ],
        out_specs=[pl.BlockSpec((tm, D_MODEL), lambda i, p: (i, 0))] * n_out,
        scratch_shapes=[pltpu.VMEM((TOP_K * tm, D_MODEL // 2), jnp.uint32),
                        pltpu.VMEM((TOP_K * tm, D_MODEL // 2), jnp.uint32),
                        pltpu.SemaphoreType.DMA((2,))],
    )
    return pl.pallas_call(
        _combine_kernel, grid_spec=grid_spec,
        out_shape=[jax.ShapeDtypeStruct((T, D_MODEL), F32)] * (n_out - 1)
        + [jax.ShapeDtypeStruct((T, D_MODEL), norm_dtype)],
        compiler_params=_params(("arbitrary",)),
    )(pos.reshape(-1), h, weight, g_next.reshape(1, D_MODEL), y)


def _dispatch(expert, T, blk):
    onehot = (expert[:, None] == jnp.arange(N_EXPERTS, dtype=jnp.int32)[None, :]).astype(jnp.int32)
    csum = jnp.cumsum(onehot, axis=0)
    rank = jnp.sum(csum * onehot, axis=1) - 1
    counts = csum[-1]
    padded = (counts + blk - 1) // blk * blk
    pad_end = jnp.cumsum(padded)
    pad_start = pad_end - padded
    pos = (jnp.sum(pad_start[None, :] * onehot, axis=1) + rank).astype(jnp.int32)
    n_rows = T * TOP_K + (N_EXPERTS + 1) * blk
    token = jnp.repeat(jnp.arange(T, dtype=jnp.int32), TOP_K)
    row_tok = jnp.zeros((n_rows,), jnp.int32).at[pos].set(token, unique_indices=True)
    n_blocks = n_rows // blk
    starts = jnp.arange(n_blocks, dtype=jnp.int32) * blk
    block_exp = jnp.minimum(jnp.sum((pad_end[None, :] <= starts[:, None]).astype(jnp.int32), axis=1),
                            N_EXPERTS - 1).astype(jnp.int32)
    n_used = (pad_end[-1] // blk).astype(jnp.int32).reshape(1)
    return pos.reshape(T, TOP_K), row_tok, block_exp, n_used


def _moe_layer(h, norm_g, w_group, b_group, w_exp, b_exp, wg, wu, wd, layer, g_next, norm_dtype,
               keep_sum=True):
    T = h.shape[0]
    wr = jnp.zeros((D_MODEL, LANE), F32)
    wr = wr.at[:, :N_GROUPS].set(w_group).at[:, N_GROUPS:N_GROUPS + N_EXPERTS].set(w_exp)
    br = jnp.zeros((LANE,), F32).at[:N_GROUPS].set(b_group).at[N_GROUPS:N_GROUPS + N_EXPERTS].set(b_exp)
    e_t, w_t, xn = router(h, norm_g, wr, br.reshape(LANE, 1))
    expert = e_t[:TOP_K].T.reshape(-1)
    weight = w_t[:TOP_K].T
    pos, row_tok, block_exp, n_used = _dispatch(expert, T, MOE_BLK)
    y = moe_experts(xn, n_used, block_exp, row_tok, wg, wu, wd, layer)
    return moe_combine(h, y, pos, weight, g_next, norm_dtype, keep_sum)


def _gmlp_layer(h, norm_g, w_in, ln_g, ln_b, w_s, b_s, w_out):
    causal = jnp.tril(jnp.ones((GM_CHUNK, GM_CHUNK), w_s.dtype))
    bs_full = jnp.repeat(b_s.T, D_MODEL // GM_GROUPS, axis=1)
    y = gmlp_gate(h, norm_g, w_in.astype(BF16), ln_g, ln_b, (w_s * causal).astype(BF16), bs_full)
    return matmul(y, w_out.astype(BF16), out_dtype=F32, resid=h)


def _overlap_t(S):
    n_c = S // CMP_STRIDE
    n_sel = S // SEL_BLK
    ci = np.arange(n_c)[None, :] * CMP_STRIDE
    sj = np.arange(n_sel)[:, None] * SEL_BLK
    ov = (ci < sj + SEL_BLK) & (ci + CMP_LEN > sj) & (np.arange(n_c)[None, :] < n_c - 1)
    return jnp.asarray(ov.astype(np.float32), dtype=BF16)


def _nsa_layer(h, hn, B, S, w_in, ck_pe, ck_w1, ck_w2, cv_pe, cv_w1, cv_w2, w_out):
    qd = N_HEADS * HEAD_DIM
    kvd = N_KV_HEADS * HEAD_DIM
    w = w_in.astype(BF16)
    o = [qd + i * kvd for i in range(7)]
    w_q, w_kc, w_vc, w_ks, w_vs, w_kw, w_vw = (
        w[:, :o[0]], w[:, o[0]:o[1]], w[:, o[1]:o[2]], w[:, o[2]:o[3]], w[:, o[3]:o[4]],
        w[:, o[4]:o[5]], w[:, o[5]:o[6]])
    w_g = jnp.zeros((D_MODEL, LANE), BF16).at[:, :3 * N_HEADS].set(w[:, o[6]:])
    q = matmul(hn, w_q, out_dtype=BF16, scale=HEAD_DIM ** -0.5 * LOG2E, layout="heads", seq=S)
    kvc = matmul(hn, jnp.concatenate([w_kc, w_vc], 1), out_dtype=F32, layout="groups", seq=S)
    ksw = matmul(hn, jnp.concatenate([w_ks, w_kw], 1), out_dtype=BF16, layout="heads", seq=S)
    vswt = matmul(hn, jnp.concatenate([w_vs, w_vw], 1), out_dtype=BF16, layout="heads_t", seq=S)
    gates = matmul(hn, w_g, out_dtype=F32, act="sigmoid")[:, :3 * N_HEADS]
    gates_t = gates.reshape(B, S, N_KV_HEADS, GQA, 3).transpose(0, 2, 4, 3, 1)

    half = CMP_STRIDE
    pe2 = jnp.stack([jnp.stack([pe[:half].reshape(-1), pe[half:].reshape(-1)])
                     for pe in (ck_pe, cv_pe)])
    w1 = jnp.stack([ck_w1, cv_w1]).astype(BF16)
    w2 = jnp.stack([ck_w2, cv_w2]).astype(BF16)
    cmp, cmp_t = compress_kv(kvc, pe2, w1, w2)
    n_c = S // CMP_STRIDE
    cmp = cmp.reshape(B, 2 * N_KV_HEADS, n_c, HEAD_DIM)
    cmp_t = cmp_t.reshape(B, 2 * N_KV_HEADS, HEAD_DIM, n_c)
    blk_id = (np.arange(S) // SEL_BLK) % LANE
    onehot = jnp.asarray(blk_id[:, None] == np.arange(LANE)[None, :], dtype=BF16)
    ks_aug = jnp.concatenate(
        [ksw[:, :N_KV_HEADS], jnp.broadcast_to(onehot, (B, N_KV_HEADS, S, LANE))], axis=-1)
    o_att = nsa_attention_core(q, cmp, cmp_t, ks_aug, ksw, vswt, gates_t, _overlap_t(S))
    return matmul(o_att.reshape(B * S, qd), w_out.astype(BF16), out_dtype=F32, resid=h)


def kernel(x, norm_mix, norm_ffn, norm_final, a_w_in, a_ln_g, a_ln_b, a_w_s, a_b_s, a_w_out,
           b_w_in, b_ck_pe, b_ck_w1, b_ck_w2, b_cv_pe, b_cv_w1, b_cv_w2, b_w_out,
           r_w_group, r_b_group, r_w_exp, r_b_exp, e_w_gate, e_w_up, e_w_down):
    B, S, _ = x.shape
    h = x.reshape(B * S, D_MODEL)
    h = _gmlp_layer(h, norm_mix[0], a_w_in[0], a_ln_g[0], a_ln_b[0], a_w_s[0], a_b_s[0], a_w_out[0])
    h, hn = _moe_layer(h, norm_ffn[0], r_w_group[0], r_b_group[0], r_w_exp[0], r_b_exp[0],
                       e_w_gate, e_w_up, e_w_down, 0, norm_mix[1], BF16)
    h = _nsa_layer(h, hn, B, S, b_w_in[0], b_ck_pe[0], b_ck_w1[0], b_ck_w2[0],
                   b_cv_pe[0], b_cv_w1[0], b_cv_w2[0], b_w_out[0])
    (out,) = _moe_layer(h, norm_ffn[1], r_w_group[1], r_b_group[1], r_w_exp[1], r_b_exp[1],
                        e_w_gate, e_w_up, e_w_down, 1, norm_final, F32, keep_sum=False)
    return out.reshape(B, S, D_MODEL)
```

```python
import functools

import numpy as np
import jax
import jax.numpy as jnp
from jax import lax
from jax.experimental import pallas as pl
from jax.experimental.pallas import tpu as pltpu

F32 = jnp.float32
BF16 = jnp.bfloat16

D_MODEL = 2048
LANE = 128
GM_GROUPS = 8
GM_CHUNK = 128
HEAD_DIM = 128
N_HEADS = 16
N_KV_HEADS = 4
GQA = N_HEADS // N_KV_HEADS
CMP_LEN = 32
CMP_STRIDE = 16
SEL_BLK = 64
SEL_TOPN = 16
SEL_LOCAL = 2
WINDOW = 512
Q_BLK = 256
FORCE = 1.0e4
N_GROUPS = 4
EXP_PER_GROUP = 8
N_EXPERTS = N_GROUPS * EXP_PER_GROUP
TOP_K = 2
D_EXPERT = 512
EPS = 1e-6
NEG = -1e30
LOG2E = 1.4426950408889634
ONES_ROWS = 16
SEL_TILE = 512
MOE_BLK = 512
VMEM_LIMIT = 56 * 1024 * 1024

_NT = (((1,), (1,)), ((), ()))


def _params(sem):
    return pltpu.CompilerParams(dimension_semantics=sem, vmem_limit_bytes=VMEM_LIMIT)


def _gelu(x):
    return 0.5 * x * (1.0 + jnp.tanh(0.7978845608028654 * (x + 0.044715 * (x * x * x))))


def _rms(x, g):
    y = x * lax.rsqrt(jnp.mean(x * x, axis=-1, keepdims=True) + EPS)
    return y * g


def _pack_rows(x):
    half = x.shape[1] // 2
    bits = lambda v: lax.bitcast_convert_type(v.astype(BF16).astype(F32), jnp.uint32)
    return (bits(x[:, :half]) >> 16) | (bits(x[:, half:]) & jnp.uint32(0xFFFF0000))


def _unpack_rows(w):
    lo = lax.bitcast_convert_type(w << 16, F32)
    hi = lax.bitcast_convert_type(w & jnp.uint32(0xFFFF0000), F32)
    return lo, hi


def _rmsnorm_kernel(x_ref, g_ref, o_ref):
    o_ref[...] = _rms(x_ref[...], g_ref[...]).astype(o_ref.dtype)


def rmsnorm(x, g, out_dtype=BF16, tm=512):
    T = x.shape[0]
    return pl.pallas_call(
        _rmsnorm_kernel,
        grid=(T // tm,),
        in_specs=[pl.BlockSpec((tm, D_MODEL), lambda i: (i, 0)),
                  pl.BlockSpec((1, D_MODEL), lambda i: (0, 0))],
        out_specs=pl.BlockSpec((tm, D_MODEL), lambda i: (i, 0)),
        out_shape=jax.ShapeDtypeStruct((T, D_MODEL), out_dtype),
        compiler_params=_params(("parallel",)),
    )(x, g.reshape(1, D_MODEL))


def _router_kernel(x_ref, g_ref, wh_ref, wl_ref, br_ref, e_ref, w_ref, xn_ref):
    y = _rms(x_ref[...], g_ref[...])
    xn_ref[...] = _pack_rows(y)
    y_hi = y.astype(BF16)
    y_lo = (y - y_hi.astype(F32)).astype(BF16)
    lg = (jnp.dot(y_hi, wh_ref[...], preferred_element_type=F32)
          + jnp.dot(y_hi, wl_ref[...], preferred_element_type=F32)
          + jnp.dot(y_lo, wh_ref[...], preferred_element_type=F32))
    lt = lg.T + br_ref[...]
    row = lambda i: lt[i:i + 1, :]

    def softmax(rows):
        m = functools.reduce(jnp.maximum, rows)
        e = [jnp.exp(r - m) for r in rows]
        s = functools.reduce(lambda a, b: a + b, e)
        return [v / s for v in e]

    def top1(vals, skip=None):
        best = jnp.full_like(vals[0], -1.0)
        idx = jnp.zeros(vals[0].shape, jnp.int32)
        for k, v in enumerate(vals):
            if skip is not None:
                v = jnp.where(skip == k, -1.0, v)
            upd = v > best
            best = jnp.where(upd, v, best)
            idx = jnp.where(upd, k, idx)
        return best, idx

    g_w, g_idx = top1(softmax([row(j) for j in range(N_GROUPS)]))
    e_logits = []
    for k in range(EXP_PER_GROUP):
        v = row(N_GROUPS + k)
        for gg in range(1, N_GROUPS):
            v = jnp.where(g_idx == gg, row(N_GROUPS + gg * EXP_PER_GROUP + k), v)
        e_logits.append(v)
    e_prob = softmax(e_logits)
    w1, i1 = top1(e_prob)
    w2, i2 = top1(e_prob, skip=i1)
    tot = w1 + w2
    zi = jnp.zeros((6, lt.shape[1]), jnp.int32)
    e_ref[...] = jnp.concatenate([g_idx * EXP_PER_GROUP + i1, g_idx * EXP_PER_GROUP + i2, zi], axis=0)
    w_ref[...] = jnp.concatenate([g_w * (w1 / tot), g_w * (w2 / tot), zi.astype(F32)], axis=0)


def router(x, g, wr, br, tm=512):
    T = x.shape[0]
    wr_hi = wr.astype(BF16)
    wr_lo = (wr - wr_hi.astype(F32)).astype(BF16)
    return pl.pallas_call(
        _router_kernel,
        grid=(T // tm,),
        in_specs=[pl.BlockSpec((tm, D_MODEL), lambda i: (i, 0)),
                  pl.BlockSpec((1, D_MODEL), lambda i: (0, 0)),
                  pl.BlockSpec((D_MODEL, LANE), lambda i: (0, 0)),
                  pl.BlockSpec((D_MODEL, LANE), lambda i: (0, 0)),
                  pl.BlockSpec((LANE, 1), lambda i: (0, 0))],
        out_specs=[pl.BlockSpec((8, tm), lambda i: (0, i)),
                   pl.BlockSpec((8, tm), lambda i: (0, i)),
                   pl.BlockSpec((tm, D_MODEL // 2), lambda i: (i, 0))],
        out_shape=[jax.ShapeDtypeStruct((8, T), jnp.int32),
                   jax.ShapeDtypeStruct((8, T), F32),
                   jax.ShapeDtypeStruct((T, D_MODEL // 2), jnp.uint32)],
        compiler_params=_params(("parallel",)),
    )(x, g.reshape(1, D_MODEL), wr_hi, wr_lo, br)


def _mm_kernel(*refs, act, scale, has_resid, layout):
    a_ref, w_ref = refs[0], refs[1]
    o_ref = refs[-2] if layout == "groups" else refs[-1]
    acc = jnp.dot(a_ref[...], w_ref[...], preferred_element_type=F32)
    if scale is not None:
        acc = acc * scale
    if act == "gelu":
        acc = _gelu(acc)
    elif act == "sigmoid":
        acc = jax.nn.sigmoid(acc)
    if has_resid:
        acc = acc + refs[2][...]
    tm, tn = acc.shape
    if layout == "plain":
        o_ref[...] = acc.astype(o_ref.dtype)
    elif layout == "heads":
        for h in range(tn // HEAD_DIM):
            o_ref[0, h] = acc[:, h * HEAD_DIM:(h + 1) * HEAD_DIM].astype(o_ref.dtype)
    elif layout == "groups":
        tmp = refs[-1]
        for h in range(tn // HEAD_DIM):
            tmp[...] = acc[:, h * HEAD_DIM:(h + 1) * HEAD_DIM]
            for t in range(CMP_STRIDE):
                o_ref[0, h, :, t * HEAD_DIM:(t + 1) * HEAD_DIM] = (
                    tmp[pl.ds(t, tm // CMP_STRIDE, stride=CMP_STRIDE), :].astype(o_ref.dtype))
    else:
        for h in range(tn // HEAD_DIM):
            for c in range(tm // LANE):
                blk = acc[c * LANE:(c + 1) * LANE, h * HEAD_DIM:(h + 1) * HEAD_DIM]
                o_ref[0, h, c] = blk.T.astype(o_ref.dtype)


def matmul(a, w, *, out_dtype, act=None, scale=None, resid=None, layout="plain",
           seq=None, tm=1024, tn=1024):
    M, K = a.shape
    N = w.shape[1]
    tn = min(tn, N)
    grid = (N // tn, M // tm)
    in_specs = [pl.BlockSpec((tm, K), lambda j, i: (i, 0)),
                pl.BlockSpec((K, tn), lambda j, i: (0, j))]
    args = [a, w]
    if resid is not None:
        in_specs.append(pl.BlockSpec((tm, tn), lambda j, i: (i, j)))
        args.append(resid)
    scratch = []
    if layout == "plain":
        out_shape = jax.ShapeDtypeStruct((M, N), out_dtype)
        out_spec = pl.BlockSpec((tm, tn), lambda j, i: (i, j))
    else:
        nb = seq // tm
        nh = tn // HEAD_DIM
        if layout == "heads":
            out_shape = jax.ShapeDtypeStruct((M // seq, N // HEAD_DIM, seq, HEAD_DIM), out_dtype)
            out_spec = pl.BlockSpec((1, nh, tm, HEAD_DIM), lambda j, i: (i // nb, j, i % nb, 0))
        elif layout == "groups":
            grp = CMP_STRIDE * HEAD_DIM
            out_shape = jax.ShapeDtypeStruct((M // seq, N // HEAD_DIM, seq // CMP_STRIDE, grp), out_dtype)
            out_spec = pl.BlockSpec((1, nh, tm // CMP_STRIDE, grp), lambda j, i: (i // nb, j, i % nb, 0))
            scratch = [pltpu.VMEM((tm, HEAD_DIM), F32)]
        else:
            out_shape = jax.ShapeDtypeStruct(
                (M // seq, N // HEAD_DIM, seq // LANE, HEAD_DIM, LANE), out_dtype)
            out_spec = pl.BlockSpec((1, nh, tm // LANE, HEAD_DIM, LANE),
                                    lambda j, i: (i // nb, j, i % nb, 0, 0))
    kern = functools.partial(_mm_kernel, act=act, scale=scale, has_resid=resid is not None,
                             layout=layout)
    return pl.pallas_call(
        kern, grid=grid, in_specs=in_specs, out_specs=out_spec, out_shape=out_shape,
        scratch_shapes=scratch, compiler_params=_params(("parallel", "parallel")),
    )(*args)


def _gate_kernel(x_ref, g_ref, w_ref, lng_ref, lnb_ref, ws_ref, bs_ref, y_ref):
    tm = x_ref.shape[0]
    a = _rms(x_ref[...], g_ref[...]).astype(BF16)
    z = _gelu(jnp.dot(a, w_ref[...], preferred_element_type=F32))
    u = z[:, :D_MODEL]
    v = z[:, D_MODEL:]
    mu = jnp.mean(v, axis=-1, keepdims=True)
    vc = v - mu
    vn = vc * lax.rsqrt(jnp.mean(vc * vc, axis=-1, keepdims=True) + EPS)
    vn = (vn * lng_ref[...] + lnb_ref[...]).astype(BF16)
    gd = D_MODEL // GM_GROUPS
    for c in range(tm // GM_CHUNK):
        rows = slice(c * GM_CHUNK, (c + 1) * GM_CHUNK)
        for g in range(GM_GROUPS):
            cols = slice(g * gd, (g + 1) * gd)
            sv = jnp.dot(ws_ref[g], vn[rows, cols], preferred_element_type=F32)
            y_ref[rows, cols] = (u[rows, cols] * (sv + bs_ref[:, cols])).astype(y_ref.dtype)


def gmlp_gate(x, g, w_in, ln_g, ln_b, ws_masked, bs_full, tm=512):
    T = x.shape[0]
    return pl.pallas_call(
        _gate_kernel,
        grid=(T // tm,),
        in_specs=[pl.BlockSpec((tm, D_MODEL), lambda i: (i, 0)),
                  pl.BlockSpec((1, D_MODEL), lambda i: (0, 0)),
                  pl.BlockSpec((D_MODEL, 2 * D_MODEL), lambda i: (0, 0), pipeline_mode=pl.Buffered(1)),
                  pl.BlockSpec((1, D_MODEL), lambda i: (0, 0)),
                  pl.BlockSpec((1, D_MODEL), lambda i: (0, 0)),
                  pl.BlockSpec((GM_GROUPS, GM_CHUNK, GM_CHUNK), lambda i: (0, 0, 0)),
                  pl.BlockSpec((GM_CHUNK, D_MODEL), lambda i: (0, 0))],
        out_specs=pl.BlockSpec((tm, D_MODEL), lambda i: (i, 0)),
        out_shape=jax.ShapeDtypeStruct((T, D_MODEL), BF16),
        compiler_params=_params(("parallel",)),
    )(x, g.reshape(1, D_MODEL), w_in, ln_g.reshape(1, D_MODEL), ln_b.reshape(1, D_MODEL), ws_masked,
      bs_full)


def _compress_kernel(x_ref, pe_ref, w1_ref, w2_ref, o_ref, ot_ref):
    x = x_ref[0]
    half = CMP_STRIDE * HEAD_DIM
    xa = (x + pe_ref[0, 0:1, :]).astype(BF16)
    xb = (x + pe_ref[0, 1:2, :]).astype(BF16)
    a = jnp.dot(xa, w1_ref[0, :half, :], preferred_element_type=F32)
    b = jnp.dot(xb, w1_ref[0, half:, :], preferred_element_type=F32)
    n = a.shape[0]
    pre = a + pltpu.roll(b, n - 1, axis=0)
    out = jnp.dot(_gelu(pre).astype(BF16), w2_ref[0], preferred_element_type=F32)
    o_ref[0] = out.astype(o_ref.dtype)
    ot_ref[0] = out.T.astype(ot_ref.dtype)


def compress_kv(kvc, pe2, w1, w2):
    B, H2, n_grp, _ = kvc.shape
    x = kvc.reshape(B * H2, n_grp, CMP_STRIDE * HEAD_DIM)
    sel = lambda i: ((i % H2) // N_KV_HEADS, 0, 0)
    return pl.pallas_call(
        _compress_kernel,
        grid=(B * H2,),
        in_specs=[pl.BlockSpec((1, n_grp, CMP_STRIDE * HEAD_DIM), lambda i: (i, 0, 0)),
                  pl.BlockSpec((1, 2, CMP_STRIDE * HEAD_DIM), sel),
                  pl.BlockSpec((1, CMP_LEN * HEAD_DIM, HEAD_DIM), sel),
                  pl.BlockSpec((1, HEAD_DIM, HEAD_DIM), sel)],
        out_specs=[pl.BlockSpec((1, n_grp, HEAD_DIM), lambda i: (i, 0, 0)),
                   pl.BlockSpec((1, HEAD_DIM, n_grp), lambda i: (i, 0, 0))],
        out_shape=[jax.ShapeDtypeStruct((B * H2, n_grp, HEAD_DIM), BF16),
                   jax.ShapeDtypeStruct((B * H2, HEAD_DIM, n_grp), BF16)],
        compiler_params=_params(("parallel",)),
    )(x, pe2, w1, w2)


def _nsa_kernel(q_ref, kc_ref, vct_ref, ksa_ref, vst_ref, kw_ref, vwt_ref, gt_ref, ovt_ref, wm_ref,
                o_ref, qaug_ref, bsel_ref, s_a, s_b):
    nq = GQA * Q_BLK
    qb = pl.program_id(2)
    q0 = qb * Q_BLK
    q_t = q_ref[0].reshape(nq, HEAD_DIM).astype(F32).T.astype(BF16)
    t_lane = q0 + (lax.broadcasted_iota(jnp.int32, (1, nq), 1) & (Q_BLK - 1))

    n_c = kc_ref.shape[2]

    def compressed(n):
        c_end = lax.broadcasted_iota(jnp.int32, (n, 1), 0) * CMP_STRIDE + (CMP_LEN - 1)
        outs, psum = [], None
        hw = nq // 2
        for lanes in (slice(0, hw), slice(hw, nq)):
            s = jnp.dot(kc_ref[0, 0, :n, :], q_t[:, lanes], preferred_element_type=F32)
            s = jnp.where(c_end <= t_lane[:, lanes], s, NEG)
            m = jnp.max(s, axis=0, keepdims=True)
            p = jnp.exp2(s - m)
            l = jnp.sum(p, axis=0, keepdims=True)
            p = p * jnp.where(m > 0.5 * NEG, 1.0 / l, 0.0)
            outs.append(jnp.dot(vct_ref[0, 0, :, :n], p.astype(BF16), preferred_element_type=F32))
            for h in range(hw // Q_BLK):
                part = p[:, h * Q_BLK:(h + 1) * Q_BLK]
                psum = part if psum is None else psum + part
        o = jnp.concatenate(outs, axis=1)
        p_hi = psum.astype(BF16)
        p_lo = (psum - p_hi.astype(F32)).astype(BF16)
        ov = ovt_ref[:, :n]
        return o, (jnp.dot(ov, p_hi, preferred_element_type=F32)
                   + jnp.dot(ov, p_lo, preferred_element_type=F32))

    n_bkt = 4 if n_c % (4 * LANE) == 0 else 1
    bkt = n_c // n_bkt
    if n_bkt == 1:
        o_cmp, imp = compressed(n_c)
    else:
        o_cmp, imp = lax.switch((q0 + Q_BLK - 1) // (bkt * CMP_STRIDE),
                                [functools.partial(compressed, bkt * (i + 1)) for i in range(n_bkt)])

    n_sel = imp.shape[0]
    blk = lax.broadcasted_iota(jnp.int32, (n_sel, Q_BLK), 0)
    t_q = q0 + lax.broadcasted_iota(jnp.int32, (n_sel, Q_BLK), 1)
    dist = (t_q // SEL_BLK) - blk
    forced = (blk == 0) | ((dist >= 0) & (dist < SEL_LOCAL))
    score = jnp.where(forced, -jnp.inf, jnp.where(blk * SEL_BLK <= t_q, imp, NEG))
    blk_f = blk.astype(F32)
    for _ in range(SEL_TOPN - 1 - SEL_LOCAL):
        top = jnp.max(score, axis=0, keepdims=True)
        first = jnp.min(jnp.where(score == top, blk_f, float(n_sel)), axis=0, keepdims=True)
        score = jnp.where(blk_f == first, -jnp.inf, score)
    picked = score == -jnp.inf
    bsel_ref[...] = jnp.where(picked, 0.0, NEG)
    bias = jnp.where(picked & (blk * SEL_BLK < q0), 0.0, NEG)

    n_half = qaug_ref.shape[0]
    if n_sel < LANE:
        bias = jnp.concatenate([bias, jnp.zeros((LANE - n_sel, Q_BLK), F32)], axis=0)
    for hf in range(n_half):
        bq = bias[hf * LANE:(hf + 1) * LANE, :].astype(BF16)
        qaug_ref[hf] = jnp.concatenate([q_t, jnp.concatenate([bq] * GQA, axis=1)], axis=0)

    wk = WINDOW + Q_BLK
    kstart = pl.multiple_of(jnp.maximum(q0 - WINDOW, 0), LANE)
    sw = jnp.dot(kw_ref[0, 0, pl.ds(kstart, wk), :], q_t, preferred_element_type=F32)
    sw = sw + jnp.concatenate([wm_ref[0]] * GQA, axis=1)
    m_w = jnp.max(sw, axis=0, keepdims=True)
    p_w = jnp.exp2((sw - m_w).astype(BF16))
    vw = jnp.concatenate([vwt_ref[0, 0, kstart // LANE + j] for j in range(wk // LANE)], axis=1)
    vw = jnp.concatenate([vw, jnp.ones((ONES_ROWS, wk), BF16)], axis=0)
    acc_w = jnp.dot(vw, p_w, preferred_element_type=F32)
    o_win = acc_w[:HEAD_DIM] * (1.0 / acc_w[HEAD_DIM:HEAD_DIM + 1])

    kpos = q0 + lax.broadcasted_iota(jnp.int32, (Q_BLK, 1), 0)
    s_d = jnp.dot(ksa_ref[0, 0, pl.ds(pl.multiple_of(q0, Q_BLK), Q_BLK), :HEAD_DIM], q_t,
                  preferred_element_type=F32)
    parts = []
    for j in range(Q_BLK // SEL_BLK):
        b_row = bsel_ref[pl.ds(q0 // SEL_BLK + j, 1), :]
        parts.append(s_d[j * SEL_BLK:(j + 1) * SEL_BLK, :] + jnp.concatenate([b_row] * GQA, axis=1))
    s_d = jnp.where(kpos <= t_lane, jnp.concatenate(parts, axis=0), NEG)
    m_d = jnp.max(s_d, axis=0, keepdims=True)
    p_d = jnp.exp2(s_d - m_d)
    v_d = jnp.concatenate([vst_ref[0, 0, q0 // LANE + j] for j in range(Q_BLK // LANE)], axis=1)
    init = (m_d, jnp.sum(p_d, axis=0, keepdims=True),
            jnp.dot(v_d, p_d.astype(BF16), preferred_element_type=F32))

    bpt = SEL_TILE // SEL_BLK
    vpt = SEL_TILE // LANE

    def scores(kt):
        k0 = pl.multiple_of(kt * SEL_TILE, SEL_TILE)
        return jnp.dot(ksa_ref[0, 0, pl.ds(k0, SEL_TILE), :], qaug_ref[(kt * bpt) // LANE],
                       preferred_element_type=F32)

    def softmax_pv(kt, st, carry):
        m_i, l_i, acc = carry
        m_new = jnp.maximum(m_i, jnp.max(st, axis=0, keepdims=True))
        alpha = jnp.exp2(m_i - m_new)
        pt = jnp.exp2(st - m_new)
        l_new = alpha * l_i + jnp.sum(pt, axis=0, keepdims=True)
        vt = jnp.concatenate([vst_ref[0, 0, kt * vpt + j] for j in range(vpt)], axis=1)
        acc = alpha * acc + jnp.dot(vt, pt.astype(BF16), preferred_element_type=F32)
        return m_new, l_new, acc

    def trip(j, carry):
        st = s_a[...]
        s_b[...] = scores(2 * j + 1)
        carry = softmax_pv(2 * j, st, carry)
        st = s_b[...]
        s_a[...] = scores(2 * j + 2)
        return softmax_pv(2 * j + 1, st, carry)

    n_tiles = (q0 + SEL_TILE - 1) // SEL_TILE
    n_trips = jnp.maximum((n_tiles + 1) // 2, 1)
    s_a[...] = scores(0)
    carry = lax.fori_loop(0, n_trips - 1, trip, init)
    last = 2 * n_trips - 1
    st = s_a[...]
    s_b[...] = scores(last)
    carry = softmax_pv(last - 1, st, carry)
    _, l_s, acc_s = softmax_pv(last, s_b[...], carry)
    o_sel = acc_s * (1.0 / l_s)

    for h in range(GQA):
        cols = slice(h * Q_BLK, (h + 1) * Q_BLK)
        o_t = (gt_ref[0, 0, 0, h:h + 1, :] * o_cmp[:, cols]
               + gt_ref[0, 0, 1, h:h + 1, :] * o_sel[:, cols]
               + gt_ref[0, 0, 2, h:h + 1, :] * o_win[:, cols])
        o_ref[0, :, h * HEAD_DIM:(h + 1) * HEAD_DIM] = o_t.T.astype(o_ref.dtype)


def _window_mask(S):
    wk = WINDOW + Q_BLK
    i = np.arange(wk)[:, None]
    qi = np.arange(Q_BLK)[None, :]
    out = []
    for v in range(WINDOW // Q_BLK + 1):
        t = Q_BLK * v + qi if v < WINDOW // Q_BLK else WINDOW + qi
        out.append(np.where((i <= t) & (i > t - WINDOW), 0.0, NEG))
    return jnp.asarray(np.stack(out), dtype=F32)


def nsa_attention_core(q, kc, vct, ks_aug, kw, vswt, gates_t, ov_t):
    B, _, S, _ = q.shape
    n_c = S // CMP_STRIDE
    n_sel = S // SEL_BLK
    nq = GQA * Q_BLK
    hk = N_KV_HEADS
    wk = WINDOW + Q_BLK
    nv = WINDOW // Q_BLK
    once = pl.Buffered(1)
    return pl.pallas_call(
        _nsa_kernel,
        grid=(B, hk, S // Q_BLK),
        in_specs=[
            pl.BlockSpec((1, GQA, Q_BLK, HEAD_DIM), lambda b, h, i: (b, h, i, 0)),
            pl.BlockSpec((1, 1, n_c, HEAD_DIM), lambda b, h, i: (b, h, 0, 0)),
            pl.BlockSpec((1, 1, HEAD_DIM, n_c), lambda b, h, i: (b, hk + h, 0, 0)),
            pl.BlockSpec((1, 1, S, 2 * HEAD_DIM), lambda b, h, i: (b, h, 0, 0), pipeline_mode=once),
            pl.BlockSpec((1, 1, S // LANE, HEAD_DIM, LANE), lambda b, h, i: (b, h, 0, 0, 0),
                         pipeline_mode=once),
            pl.BlockSpec((1, 1, S, HEAD_DIM), lambda b, h, i: (b, hk + h, 0, 0), pipeline_mode=once),
            pl.BlockSpec((1, 1, S // LANE, HEAD_DIM, LANE), lambda b, h, i: (b, hk + h, 0, 0, 0),
                         pipeline_mode=once),
            pl.BlockSpec((1, 1, 3, GQA, Q_BLK), lambda b, h, i: (b, h, 0, 0, i)),
            pl.BlockSpec((n_sel, n_c), lambda b, h, i: (0, 0)),
            pl.BlockSpec((1, wk, Q_BLK), lambda b, h, i: (jnp.minimum(i, nv), 0, 0)),
        ],
        out_specs=pl.BlockSpec((1, Q_BLK, GQA * HEAD_DIM), lambda b, h, i: (b, i, h)),
        out_shape=jax.ShapeDtypeStruct((B, S, N_HEADS * HEAD_DIM), BF16),
        scratch_shapes=[pltpu.VMEM((max(1, n_sel // LANE), 2 * HEAD_DIM, nq), BF16),
                        pltpu.VMEM((n_sel, Q_BLK), F32),
                        pltpu.VMEM((SEL_TILE, nq), F32), pltpu.VMEM((SEL_TILE, nq), F32)],
        compiler_params=_params(("parallel", "parallel", "arbitrary")),
    )(q, kc, vct, ks_aug, vswt, kw, vswt, gates_t, ov_t, _window_mask(S))


def _moe_kernel(nused_ref, bexp_ref, rtok_ref, h_hbm, wg_ref, wu_ref, wd_ref, y_ref,
                x_even, x_odd, wg_b, wu_b, wd_b, sem):
    i = pl.program_id(0)
    n_used = nused_ref[0]
    blk = x_even.shape[0]
    bufs = (x_even, x_odd)

    def row_copy(b, r, slot):
        tok = rtok_ref[b * blk + r]
        return pltpu.make_async_copy(h_hbm.at[pl.ds(tok, 1)], bufs[slot].at[pl.ds(r, 1)],
                                     sem.at[slot])

    def wait_block(slot):
        pltpu.make_async_copy(h_hbm.at[pl.ds(0, blk)], bufs[slot], sem.at[slot]).wait()

    @pl.when(i == 0)
    def _():
        def body(r, c):
            row_copy(0, r, 0).start()
            return c
        lax.fori_loop(0, blk, body, 0, unroll=8)

    @pl.when((i < n_used) & ((i == 0) | (bexp_ref[i] != bexp_ref[jnp.maximum(i - 1, 0)])))
    def _():
        wg_b[...] = wg_ref[0, 0].astype(BF16)
        wu_b[...] = wu_ref[0, 0].astype(BF16)
        wd_b[...] = wd_ref[0, 0].astype(BF16)

    for slot in range(2):
        @pl.when((i < n_used) & (i % 2 == slot))
        def _(slot=slot):
            wait_block(slot)
            for r in range(blk):
                row_copy(i + 1, r, 1 - slot).start(priority=r % 2)
            x = jnp.concatenate(_unpack_rows(bufs[slot][...]), axis=1).astype(BF16)
            gate = jnp.dot(x, wg_b[...], preferred_element_type=F32)
            up = jnp.dot(x, wu_b[...], preferred_element_type=F32)
            hid = (gate * jax.nn.sigmoid(gate) * up).astype(BF16)
            y_ref[...] = _pack_rows(jnp.dot(hid, wd_b[...], preferred_element_type=F32))

        @pl.when((i == n_used) & (i % 2 == slot))
        def _(slot=slot):
            wait_block(slot)

    @pl.when(i >= n_used)
    def _():
        y_ref[...] = jnp.zeros_like(y_ref)


def moe_experts(xn, n_used, block_exp, row_tok, wg, wu, wd, layer, blk=MOE_BLK):
    n_rows = row_tok.shape[0]
    n_steps = n_rows // blk
    grid_spec = pltpu.PrefetchScalarGridSpec(
        num_scalar_prefetch=3,
        grid=(n_steps,),
        in_specs=[
            pl.BlockSpec(memory_space=pl.ANY),
            pl.BlockSpec((1, 1, D_MODEL, D_EXPERT), lambda i, nu, be, rt: (layer, be[i], 0, 0)),
            pl.BlockSpec((1, 1, D_MODEL, D_EXPERT), lambda i, nu, be, rt: (layer, be[i], 0, 0)),
            pl.BlockSpec((1, 1, D_EXPERT, D_MODEL), lambda i, nu, be, rt: (layer, be[i], 0, 0)),
        ],
        out_specs=pl.BlockSpec((blk, D_MODEL // 2), lambda i, nu, be, rt: (i, 0)),
        scratch_shapes=[pltpu.VMEM((blk, D_MODEL // 2), jnp.uint32),
                        pltpu.VMEM((blk, D_MODEL // 2), jnp.uint32),
                        pltpu.VMEM((D_MODEL, D_EXPERT), BF16), pltpu.VMEM((D_MODEL, D_EXPERT), BF16),
                        pltpu.VMEM((D_EXPERT, D_MODEL), BF16), pltpu.SemaphoreType.DMA((2,))],
    )
    return pl.pallas_call(
        _moe_kernel, grid_spec=grid_spec,
        out_shape=jax.ShapeDtypeStruct((n_rows, D_MODEL // 2), jnp.uint32),
        compiler_params=_params(("arbitrary",)),
    )(n_used, block_exp, row_tok, xn, wg, wu, wd)


def _combine_kernel(pos_ref, h_ref, w_ref, g_ref, y_hbm, *rest):
    *o_refs, n_ref, y_even, y_odd, sem = rest
    i = pl.program_id(0)
    n = pl.num_programs(0)
    tm = h_ref.shape[0]
    bufs = (y_even, y_odd)

    def row_copy(b, j, k, slot):
        row = pos_ref[(b * tm + j) * TOP_K + k]
        return pltpu.make_async_copy(y_hbm.at[pl.ds(row, 1)], bufs[slot].at[pl.ds(k * tm + j, 1)],
                                     sem.at[slot])

    @pl.when(i == 0)
    def _():
        def body(j, c):
            for k in range(TOP_K):
                row_copy(0, j, k, 0).start()
            return c
        lax.fori_loop(0, tm, body, 0, unroll=4)

    def step(slot, prefetch):
        pltpu.make_async_copy(y_hbm.at[pl.ds(0, TOP_K * tm)], bufs[slot], sem.at[slot]).wait()
        if prefetch:
            for j in range(tm):
                for k in range(TOP_K):
                    row_copy(i + 1, j, k, 1 - slot).start(priority=k)
        half = D_MODEL // 2
        lo, hi = h_ref[:, :half], h_ref[:, half:]
        for k in range(TOP_K):
            y_lo, y_hi = _unpack_rows(bufs[slot][k * tm:(k + 1) * tm, :])
            lo = lo + y_lo * w_ref[:, k:k + 1]
            hi = hi + y_hi * w_ref[:, k:k + 1]
        out = jnp.concatenate([lo, hi], axis=1)
        for o_ref in o_refs:
            o_ref[...] = out
        n_ref[...] = _rms(out, g_ref[...]).astype(n_ref.dtype)

    for slot in range(2):
        for prefetch in (True, False):
            cond = (i % 2 == slot) & ((i + 1 < n) if prefetch else (i + 1 == n))
            pl.when(cond)(functools.partial(step, slot, prefetch))


def moe_combine(h, y, pos, weight, g_next, norm_dtype, keep_sum, tm=256):
    T = h.shape[0]
    n_out = 2 if keep_sum else 1
    grid_spec = pltpu.PrefetchScalarGridSpec(
        num_scalar_prefetch=1,
        grid=(T // tm,),
        in_specs=[
            pl.BlockSpec((tm, D_MODEL), lambda i, p: (i, 0)),
            pl.BlockSpec((tm, TOP_K), lambda i, p: (i, 0)),
            pl.BlockSpec((1, D_MODEL), lambda i, p: (0, 0)),
            pl.BlockSpec(memory_space=pl.ANY),
        ],
        out_specs=[pl.BlockSpec((tm, D_MODEL), lambda i, p: (i, 0))] * n_out,
        scratch_shapes=[pltpu.VMEM((TOP_K * tm, D_MODEL // 2), jnp.uint32),
                        pltpu.VMEM((TOP_K * tm, D_MODEL // 2), jnp.uint32),
                        pltpu.SemaphoreType.DMA((2,))],
    )
    return pl.pallas_call(
        _combine_kernel, grid_spec=grid_spec,
        out_shape=[jax.ShapeDtypeStruct((T, D_MODEL), F32)] * (n_out - 1)
        + [jax.ShapeDtypeStruct((T, D_MODEL), norm_dtype)],
        compiler_params=_params(("arbitrary",)),
    )(pos.reshape(-1), h, weight, g_next.reshape(1, D_MODEL), y)


def _dispatch(expert, T, blk):
    onehot = (expert[:, None] == jnp.arange(N_EXPERTS, dtype=jnp.int32)[None, :]).astype(jnp.int32)
    csum = jnp.cumsum(onehot, axis=0)
    rank = jnp.sum(csum * onehot, axis=1) - 1
    counts = csum[-1]
    padded = (counts + blk - 1) // blk * blk
    pad_end = jnp.cumsum(padded)
    pad_start = pad_end - padded
    pos = (jnp.sum(pad_start[None, :] * onehot, axis=1) + rank).astype(jnp.int32)
    n_rows = T * TOP_K + (N_EXPERTS + 1) * blk
    token = jnp.repeat(jnp.arange(T, dtype=jnp.int32), TOP_K)
    row_tok = jnp.zeros((n_rows,), jnp.int32).at[pos].set(token, unique_indices=True)
    n_blocks = n_rows // blk
    starts = jnp.arange(n_blocks, dtype=jnp.int32) * blk
    block_exp = jnp.minimum(jnp.sum((pad_end[None, :] <= starts[:, None]).astype(jnp.int32), axis=1),
                            N_EXPERTS - 1).astype(jnp.int32)
    n_used = (pad_end[-1] // blk).astype(jnp.int32).reshape(1)
    return pos.reshape(T, TOP_K), row_tok, block_exp, n_used


def _moe_layer(h, norm_g, w_group, b_group, w_exp, b_exp, wg, wu, wd, layer, g_next, norm_dtype,
               keep_sum=True):
    T = h.shape[0]
    wr = jnp.zeros((D_MODEL, LANE), F32)
    wr = wr.at[:, :N_GROUPS].set(w_group).at[:, N_GROUPS:N_GROUPS + N_EXPERTS].set(w_exp)
    br = jnp.zeros((LANE,), F32).at[:N_GROUPS].set(b_group).at[N_GROUPS:N_GROUPS + N_EXPERTS].set(b_exp)
    e_t, w_t, xn = router(h, norm_g, wr, br.reshape(LANE, 1))
    expert = e_t[:TOP_K].T.reshape(-1)
    weight = w_t[:TOP_K].T
    pos, row_tok, block_exp, n_used = _dispatch(expert, T, MOE_BLK)
    y = moe_experts(xn, n_used, block_exp, row_tok, wg, wu, wd, layer)
    return moe_combine(h, y, pos, weight, g_next, norm_dtype, keep_sum)


def _gmlp_layer(h, norm_g, w_in, ln_g, ln_b, w_s, b_s, w_out):
    causal = jnp.tril(jnp.ones((GM_CHUNK, GM_CHUNK), w_s.dtype))
    bs_full = jnp.repeat(b_s.T, D_MODEL // GM_GROUPS, axis=1)
    y = gmlp_gate(h, norm_g, w_in.astype(BF16), ln_g, ln_b, (w_s * causal).astype(BF16), bs_full)
    return matmul(y, w_out.astype(BF16), out_dtype=F32, resid=h)


def _overlap_t(S):
    n_c = S // CMP_STRIDE
    n_sel = S // SEL_BLK
    ci = np.arange(n_c)[None, :] * CMP_STRIDE
    sj = np.arange(n_sel)[:, None] * SEL_BLK
    ov = (ci < sj + SEL_BLK) & (ci + CMP_LEN > sj) & (np.arange(n_c)[None, :] < n_c - 1)
    return jnp.asarray(ov.astype(np.float32), dtype=BF16)


def _nsa_layer(h, hn, B, S, w_in, ck_pe, ck_w1, ck_w2, cv_pe, cv_w1, cv_w2, w_out):
    qd = N_HEADS * HEAD_DIM
    kvd = N_KV_HEADS * HEAD_DIM
    w = w_in.astype(BF16)
    o = [qd + i * kvd for i in range(7)]
    w_q, w_kc, w_vc, w_ks, w_vs, w_kw, w_vw = (
        w[:, :o[0]], w[:, o[0]:o[1]], w[:, o[1]:o[2]], w[:, o[2]:o[3]], w[:, o[3]:o[4]],
        w[:, o[4]:o[5]], w[:, o[5]:o[6]])
    w_g = jnp.zeros((D_MODEL, LANE), BF16).at[:, :3 * N_HEADS].set(w[:, o[6]:])
    q = matmul(hn, w_q, out_dtype=BF16, scale=HEAD_DIM ** -0.5 * LOG2E, layout="heads", seq=S)
    kvc = matmul(hn, jnp.concatenate([w_kc, w_vc], 1), out_dtype=F32, layout="groups", seq=S)
    ksw = matmul(hn, jnp.concatenate([w_ks, w_kw], 1), out_dtype=BF16, layout="heads", seq=S)
    vswt = matmul(hn, jnp.concatenate([w_vs, w_vw], 1), out_dtype=BF16, layout="heads_t", seq=S)
    gates = matmul(hn, w_g, out_dtype=F32, act="sigmoid")[:, :3 * N_HEADS]
    gates_t = gates.reshape(B, S, N_KV_HEADS, GQA, 3).transpose(0, 2, 4, 3, 1)

    half = CMP_STRIDE
    pe2 = jnp.stack([jnp.stack([pe[:half].reshape(-1), pe[half:].reshape(-1)])
                     for pe in (ck_pe, cv_pe)])
    w1 = jnp.stack([ck_w1, cv_w1]).astype(BF16)
    w2 = jnp.stack([ck_w2, cv_w2]).astype(BF16)
    cmp, cmp_t = compress_kv(kvc, pe2, w1, w2)
    n_c = S // CMP_STRIDE
    cmp = cmp.reshape(B, 2 * N_KV_HEADS, n_c, HEAD_DIM)
    cmp_t = cmp_t.reshape(B, 2 * N_KV_HEADS, HEAD_DIM, n_c)
    blk_id = (np.arange(S) // SEL_BLK) % LANE
    onehot = jnp.asarray(blk_id[:, None] == np.arange(LANE)[None, :], dtype=BF16)
    ks_aug = jnp.concatenate(
        [ksw[:, :N_KV_HEADS], jnp.broadcast_to(onehot, (B, N_KV_HEADS, S, LANE))], axis=-1)
    o_att = nsa_attention_core(q, cmp, cmp_t, ks_aug, ksw, vswt, gates_t, _overlap_t(S))
    return matmul(o_att.reshape(B * S, qd), w_out.astype(BF16), out_dtype=F32, resid=h)


def kernel(x, norm_mix, norm_ffn, norm_final, a_w_in, a_ln_g, a_ln_b, a_w_s, a_b_s, a_w_out,
           b_w_in, b_ck_pe, b_ck_w1, b_ck_w2, b_cv_pe, b_cv_w1, b_cv_w2, b_w_out,
           r_w_group, r_b_group, r_w_exp, r_b_exp, e_w_gate, e_w_up, e_w_down):
    B, S, _ = x.shape
    h = x.reshape(B * S, D_MODEL)
    h = _gmlp_layer(h, norm_mix[0], a_w_in[0], a_ln_g[0], a_ln_b[0], a_w_s[0], a_b_s[0], a_w_out[0])
    h, hn = _moe_layer(h, norm_ffn[0], r_w_group[0], r_b_group[0], r_w_exp[0], r_b_exp[0],
                       e_w_gate, e_w_up, e_w_down, 0, norm_mix[1], BF16)
    h = _nsa_layer(h, hn, B, S, b_w_in[0], b_ck_pe[0], b_ck_w1[0], b_ck_w2[0],
                   b_cv_pe[0], b_cv_w1[0], b_cv_w2[0], b_w_out[0])
    (out,) = _moe_layer(h, norm_ffn[1], r_w_group[1], r_b_group[1], r_w_exp[1], r_b_exp[1],
                        e_w_gate, e_w_up, e_w_down, 1, norm_final, F32, keep_sum=False)
    return out.reshape(B, S, D_MODEL)
```
